```python
import math
import jax
import jax.numpy as jnp
from jax import lax
import numpy as np

D_MODEL = 1024
BATCH = 4
SEQ = 4096
DEPTH = 4
DEC_BATCH = 32
DEC_SEQ = 32
PAST_LEN = 1024

CHUNK = 64
N_META = 16
EPS = 1e-6
H_FOX = 6
HD_FOX = 64
D_FOX = H_FOX * HD_FOX
Q_BLOCK = 128
POOL_WINDOWS = (2, 4, 8, 16)
N_POOL_GROUPS = 4
POOL_GROUP_DIM = 64
D_POOL = N_POOL_GROUPS * POOL_GROUP_DIM
POOL_BUF = 15
H_SSD = 6
HD_SSD = 64
D_SSD = H_SSD * HD_SSD
N_BC_GROUPS = 2
D_STATE = 64
CONV_W = 4
D_CONV = D_SSD + 2 * N_BC_GROUPS * D_STATE
D_MIX = D_FOX + D_POOL + D_SSD
SPLIT_SIZES = (D_FOX, D_FOX, D_FOX, H_FOX, D_POOL, D_SSD, D_CONV, H_SSD)
N_IN = D_FOX * 3 + H_FOX + D_POOL + D_SSD + D_CONV + H_SSD
D_FF = ((8 * D_MODEL + 3 * 256 - 1) // (3 * 256)) * 256

kernel_name = "hymba_fox_pool_ssd_streaming_step"


def rmsnorm(x, g):
    xf = x.astype(jnp.float32)
    y = xf * lax.rsqrt(jnp.mean(xf * xf, axis=-1, keepdims=True) + EPS)
    return (y * g.astype(jnp.float32)).astype(x.dtype)


def split_points():
    pts, acc = [], 0
    for s in SPLIT_SIZES[:-1]:
        acc += s
        pts.append(acc)
    return pts


def fox_block(q, k, v, c_q, c_k, q_pos, k_pos):
    s = jnp.einsum('bqhd,bkhd->bhqk', q, k).astype(jnp.float32) * (HD_FOX ** -0.5)
    s = s + jnp.transpose(c_q, (0, 2, 1))[:, :, :, None] - jnp.transpose(c_k, (0, 2, 1))[:, :, None, :]
    mask = (k_pos[None, :] <= q_pos[:, None])[None, None]
    p = jax.nn.softmax(jnp.where(mask, s, -jnp.inf), axis=-1)
    return jnp.einsum('bhqk,bkhd->bqhd', p.astype(v.dtype), v)


def fox_prompt(q, k, v, logf):
    b, t = q.shape[0], q.shape[1]
    c = jnp.cumsum(logf, axis=1)
    nb = -(-t // Q_BLOCK)
    tp = nb * Q_BLOCK
    pad = tp - t
    padt = lambda a: jnp.pad(a, [(0, 0), (0, pad)] + [(0, 0)] * (a.ndim - 2))
    qp, kp, vp, cp = padt(q), padt(k), padt(v), padt(c)
    k_pos = jnp.arange(tp)

    def one_block(i):
        s0 = i * Q_BLOCK
        qb = lax.dynamic_slice_in_dim(qp, s0, Q_BLOCK, axis=1)
        cb = lax.dynamic_slice_in_dim(cp, s0, Q_BLOCK, axis=1)
        return fox_block(qb, kp, vp, cb, cp, s0 + jnp.arange(Q_BLOCK), k_pos)

    out = lax.map(one_block, jnp.arange(nb))
    out = jnp.moveaxis(out, 0, 1).reshape(b, tp, H_FOX, HD_FOX)
    return out[:, :t]


def fox_sample(q, k, v, logf, ck, cv, cl):
    p_len, s_len = ck.shape[1], q.shape[1]
    k_all = jnp.concatenate([ck.astype(k.dtype), k], axis=1)
    v_all = jnp.concatenate([cv.astype(v.dtype), v], axis=1)
    c_all = jnp.cumsum(jnp.concatenate([cl.astype(jnp.float32), logf], axis=1), axis=1)
    return fox_block(q, k_all, v_all, c_all[:, p_len:], c_all,
                     p_len + jnp.arange(s_len), jnp.arange(p_len + s_len))


def pool_mixer(u, prefix, pos0, w_pool, scale):
    b, L, _ = u.shape
    ext = jnp.concatenate([prefix.astype(u.dtype), u], axis=1)
    csum = jnp.pad(jnp.cumsum(ext.astype(jnp.float32), axis=1), ((0, 0), (1, 0), (0, 0)))
    end = csum[:, POOL_BUF + 1:]
    pos = pos0 + jnp.arange(L)
    parts = []
    for g, w in enumerate(POOL_WINDOWS):
        lo, hi = g * POOL_GROUP_DIM, (g + 1) * POOL_GROUP_DIM
        start = csum[:, POOL_BUF + 1 - w:POOL_BUF + 1 - w + L, lo:hi]
        cnt = jnp.minimum(pos + 1, w).astype(jnp.float32)[None, :, None]
        parts.append((end[:, :, lo:hi] - start) / cnt)
    diff = jnp.concatenate(parts, axis=-1) - u.astype(jnp.float32)
    y = jnp.einsum('blgc,gce->blge', diff.reshape(b, L, N_POOL_GROUPS, POOL_GROUP_DIM),
                   w_pool.astype(jnp.float32)).reshape(b, L, D_POOL)
    y = y * scale.astype(jnp.float32)
    return y.astype(u.dtype), ext[:, -POOL_BUF:]


def causal_conv(xbc, prefix, w, bias):
    L = xbc.shape[1]
    ext = jnp.concatenate([prefix.astype(xbc.dtype), xbc], axis=1)
    out = bias
    for j in range(CONV_W):
        out = out + ext[:, j:j + L] * w[j]
    return jax.nn.silu(out), ext[:, -(CONV_W - 1):]


def ssd_scan(xs, dt, a, bm, cm, init, block):
    b, L, h, p = xs.shape
    n = bm.shape[-1]
    nc = L // block
    rep = h // bm.shape[2]
    bh = jnp.repeat(bm, rep, axis=2).reshape(b, nc, block, h, n)
    ch = jnp.repeat(cm, rep, axis=2).reshape(b, nc, block, h, n)
    la = (dt * a).reshape(b, nc, block, h)
    xd = (xs * dt[..., None]).reshape(b, nc, block, h, p)
    a_cs = jnp.cumsum(la, axis=2)
    seg = a_cs[:, :, :, None, :] - a_cs[:, :, None, :, :]
    causal = jnp.tril(jnp.ones((block, block), dtype=bool))[None, None, :, :, None]
    lmat = jnp.exp(jnp.where(causal, seg, -jnp.inf))
    gmat = jnp.einsum('bcqhn,bcshn->bcqsh', ch, bh) * lmat
    y_diag = jnp.einsum('bcqsh,bcshp->bcqhp', gmat, xd)
    decay_end = jnp.exp(a_cs[:, :, -1:, :] - a_cs)
    chunk_states = jnp.einsum('bcshn,bcsh,bcshp->bchpn', bh, decay_end, xd)
    chunk_decay = jnp.exp(a_cs[:, :, -1, :])

    def step(state, inp):
        dec, st = inp
        return dec[:, :, None, None] * state + st, state

    final, prev = lax.scan(step, init, (jnp.moveaxis(chunk_decay, 1, 0), jnp.moveaxis(chunk_states, 1, 0)))
    prev = jnp.moveaxis(prev, 0, 1)
    y_off = jnp.einsum('bcqhn,bchpn->bcqhp', ch, prev) * jnp.exp(a_cs)[..., None]
    return (y_diag + y_off).reshape(b, L, h, p), final


def trunk_layer(x, params, state, pos0):
    (ln_pre_mix, ln_post_mix, ln_pre_ffn, ln_post_ffn, w_in, fox_f_bias, pool_w, pool_scale,
     conv_w, conv_b, dt_bias, a_log, d_skip, ssd_norm, w_out, w_gate, w_up, w_down) = params
    f32 = jnp.float32
    b, L, _ = x.shape
    h = rmsnorm(x, ln_pre_mix)
    proj = h @ w_in
    q, k, v, f_raw, u, z, xbc, dt_raw = jnp.split(proj, split_points(), axis=-1)
    q = q.reshape(b, L, H_FOX, HD_FOX)
    k = k.reshape(b, L, H_FOX, HD_FOX)
    v = v.reshape(b, L, H_FOX, HD_FOX)
    logf = jax.nn.log_sigmoid(f_raw.astype(f32) + fox_f_bias.astype(f32))
    if state is None:
        attn = fox_prompt(q, k, v, logf)
        pool_prefix = jnp.zeros((b, POOL_BUF, D_POOL), x.dtype)
        conv_prefix = jnp.zeros((b, CONV_W - 1, D_CONV), x.dtype)
        ssd_init = jnp.zeros((b, H_SSD, HD_SSD, D_STATE), f32)
        ssd_pad = (-L) % CHUNK
        block = CHUNK
    else:
        ck, cv, cl, pool_prefix, conv_prefix, ssd_init = state
        attn = fox_sample(q, k, v, logf, ck, cv, cl)
        ssd_pad = 0
        block = L
    pool_out, pool_new = pool_mixer(u, pool_prefix, pos0, pool_w, pool_scale)
    xbc_a, conv_new = causal_conv(xbc, conv_prefix, conv_w, conv_b)
    nbc = N_BC_GROUPS * D_STATE
    xs = xbc_a[..., :D_SSD].reshape(b, L, H_SSD, HD_SSD).astype(f32)
    bm = xbc_a[..., D_SSD:D_SSD + nbc].reshape(b, L, N_BC_GROUPS, D_STATE).astype(f32)
    cm = xbc_a[..., D_SSD + nbc:].reshape(b, L, N_BC_GROUPS, D_STATE).astype(f32)
    dt = jax.nn.softplus(dt_raw.astype(f32) + dt_bias.astype(f32))
    a = -jnp.exp(a_log.astype(f32))
    lpad = lambda t: jnp.pad(t, [(0, 0), (ssd_pad, 0)] + [(0, 0)] * (t.ndim - 2))
    y, ssd_new = ssd_scan(lpad(xs), lpad(dt), a, lpad(bm), lpad(cm), ssd_init.astype(f32), block)
    y = y[:, ssd_pad:] + d_skip.astype(f32)[:, None] * xs
    y = y.reshape(b, L, D_SSD) * jax.nn.silu(z.astype(f32))
    ssd_out = rmsnorm(y, ssd_norm).astype(x.dtype)
    mix = jnp.concatenate([attn.reshape(b, L, D_FOX).astype(x.dtype), pool_out.astype(x.dtype), ssd_out], axis=-1)
    x = x + rmsnorm(mix @ w_out, ln_post_mix)
    h = rmsnorm(x, ln_pre_ffn)
    ff = (jax.nn.silu(h @ w_gate) * (h @ w_up)) @ w_down
    x = x + rmsnorm(ff, ln_post_ffn)
    return x, (k, v, logf, pool_new, conv_new, ssd_new)


def setup_inputs(seed: int = 0) -> dict:
    key = jax.random.key(seed)
    ks = jax.random.split(key, 32)
    nrm = lambda i, shape, s=1.0: s * jax.random.normal(ks[i], shape, jnp.float32)
    u01 = jax.random.uniform(ks[20], (DEPTH, H_SSD), jnp.float32)
    dt0 = jnp.exp(u01 * (math.log(0.1) - math.log(0.001)) + math.log(0.001))
    return {
        'x_prompt': nrm(0, (BATCH, SEQ, D_MODEL)),
        'x_sample': nrm(1, (DEC_BATCH, DEC_SEQ, D_MODEL)),
        'cache_fox_k': nrm(2, (DEPTH, DEC_BATCH, PAST_LEN, H_FOX, HD_FOX)),
        'cache_fox_v': nrm(3, (DEPTH, DEC_BATCH, PAST_LEN, H_FOX, HD_FOX)),
        'cache_fox_logf': jax.nn.log_sigmoid(nrm(4, (DEPTH, DEC_BATCH, PAST_LEN, H_FOX)) + 2.0),
        'state_pool': nrm(5, (DEPTH, DEC_BATCH, POOL_BUF, D_POOL)),
        'state_conv': nrm(6, (DEPTH, DEC_BATCH, CONV_W - 1, D_CONV)),
        'state_ssd': nrm(7, (DEPTH, DEC_BATCH, H_SSD, HD_SSD, D_STATE), 0.1),
        'meta_tokens': nrm(8, (N_META, D_MODEL)),
        'ln_pre_mix': 1.0 + nrm(9, (DEPTH, D_MODEL), 0.05),
        'ln_post_mix': 1.0 + nrm(10, (DEPTH, D_MODEL), 0.05),
        'ln_pre_ffn': 1.0 + nrm(11, (DEPTH, D_MODEL), 0.05),
        'ln_post_ffn': 1.0 + nrm(12, (DEPTH, D_MODEL), 0.05),
        'w_in': nrm(13, (DEPTH, D_MODEL, N_IN), D_MODEL ** -0.5),
        'fox_f_bias': nrm(14, (DEPTH, H_FOX), 0.1),
        'pool_w': nrm(15, (DEPTH, N_POOL_GROUPS, POOL_GROUP_DIM, POOL_GROUP_DIM), POOL_GROUP_DIM ** -0.5),
        'pool_scale': 1.0 + nrm(16, (DEPTH, D_POOL), 0.05),
        'conv_w': nrm(17, (DEPTH, CONV_W, D_CONV), CONV_W ** -0.5),
        'conv_b': nrm(18, (DEPTH, D_CONV), 0.01),
        'dt_bias': dt0 + jnp.log(-jnp.expm1(-dt0)),
        'a_log': jnp.log(jax.random.uniform(ks[21], (DEPTH, H_SSD), jnp.float32, 1.0, 16.0)),
        'd_skip': 1.0 + nrm(22, (DEPTH, H_SSD), 0.05),
        'ssd_norm': 1.0 + nrm(23, (DEPTH, D_SSD), 0.05),
        'w_out': nrm(24, (DEPTH, D_MIX, D_MODEL), D_MIX ** -0.5),
        'w_gate': nrm(25, (DEPTH, D_MODEL, D_FF), D_MODEL ** -0.5),
        'w_up': nrm(26, (DEPTH, D_MODEL, D_FF), D_MODEL ** -0.5),
        'w_down': nrm(27, (DEPTH, D_FF, D_MODEL), D_FF ** -0.5),
    }


def reference(x_prompt, x_sample, cache_fox_k, cache_fox_v, cache_fox_logf, state_pool, state_conv,
              state_ssd, meta_tokens, ln_pre_mix, ln_post_mix, ln_pre_ffn, ln_post_ffn, w_in,
              fox_f_bias, pool_w, pool_scale, conv_w, conv_b, dt_bias, a_log, d_skip, ssd_norm,
              w_out, w_gate, w_up, w_down):
    weights = (ln_pre_mix, ln_post_mix, ln_pre_ffn, ln_post_ffn, w_in, fox_f_bias, pool_w, pool_scale,
               conv_w, conv_b, dt_bias, a_log, d_skip, ssd_norm, w_out, w_gate, w_up, w_down)
    past = cache_fox_k.shape[2]
    meta = jnp.broadcast_to(meta_tokens.astype(x_prompt.dtype)[None],
                            (x_prompt.shape[0], N_META, D_MODEL))
    xp = jnp.concatenate([meta, x_prompt], axis=1)
    xs = x_sample
    new_p, new_s = [], []
    for l in range(DEPTH):
        p_l = tuple(w[l] for w in weights)
        xp, st_p = trunk_layer(xp, p_l, None, 0)
        xs, st_s = trunk_layer(xs, p_l, (cache_fox_k[l], cache_fox_v[l], cache_fox_logf[l],
                                         state_pool[l], state_conv[l], state_ssd[l]), past)
        new_p.append(st_p)
        new_s.append(st_s)
    stk = lambda lst, i: jnp.stack([s[i] for s in lst], axis=0)
    y_prompt = xp[:, N_META:]
    return (y_prompt, xs,
            stk(new_p, 0), stk(new_p, 1), stk(new_p, 2), stk(new_p, 3), stk(new_p, 4), stk(new_p, 5),
            stk(new_s, 0), stk(new_s, 1), stk(new_s, 2), stk(new_s, 3), stk(new_s, 4), stk(new_s, 5))
```

```python
import functools

import jax
import jax.numpy as jnp
from jax import lax
from jax.experimental import pallas as pl
from jax.experimental.pallas import tpu as pltpu

F32 = jnp.float32
BF16 = jnp.bfloat16

D_MODEL = 1024
N_META = 16
EPS = 1e-6
H = 6
HD = 64
D_FOX = H * HD
D_POOL = 256
POOL_BUF = 15
D_SSD = H * HD
D_BC = 128
D_CONV = D_SSD + 2 * D_BC
CONV_W = 4
D_FF = 2816
LANE = 128
NEG = -1e30

_Q0, _K0, _V0, _U0, _Z0, _X0, _MAIN = 0, 384, 768, 1152, 1408, 1792, 2432
_F_AT, _DT_AT = 0, 8

_VMEM_LIMIT = 56 * 1024 * 1024


def _cparams(sem):
    return pltpu.CompilerParams(dimension_semantics=sem, vmem_limit_bytes=_VMEM_LIMIT)


def _rms(x, g):
    ms = jnp.mean(x * x, axis=-1, keepdims=True)
    return x * lax.rsqrt(ms + EPS) * g


def _silu(x):
    return x * (1.0 / (1.0 + jnp.exp(-x)))


def _softplus_tail(x):
    return jnp.log1p(jnp.exp(-jnp.abs(x)))


def _nt(a, b):
    return lax.dot_general(a, b, (((1,), (1,)), ((), ())), preferred_element_type=F32)


def _tn(a, b):
    return lax.dot_general(a, b, (((0,), (0,)), ((), ())), preferred_element_type=F32)


def _scan_lanes(x):
    lane = lax.broadcasted_iota(jnp.int32, x.shape, 1)
    s = 1
    while s < x.shape[1]:
        x = x + jnp.where(lane >= s, pltpu.roll(x, s, 1), 0.0)
        s *= 2
    return x


def _scan_rows(x):
    row = lax.broadcasted_iota(jnp.int32, x.shape, 0)
    s = 1
    while s < x.shape[0]:
        x = x + jnp.where(row >= s, pltpu.roll(x, s, 0), 0.0)
        s *= 2
    return x


def _inproj_kernel(x_ref, g_ref, wm_ref, ws_ref, wst_ref, bs_ref, bst_ref,
                   qb_ref, k_ref, kb_ref, v_ref, vb_ref, u_ref, z_ref, xbc_ref, sm_ref, smt_ref):
    hb = _rms(x_ref[...], g_ref[...]).astype(BF16)

    def seg(a, b):
        return jnp.dot(hb, wm_ref[:, a:b], preferred_element_type=F32)

    qb_ref[...] = (seg(_Q0, _K0) * (HD ** -0.5)).astype(BF16)
    k = seg(_K0, _V0)
    k_ref[...] = k
    kb_ref[...] = k.astype(BF16)
    v = seg(_V0, _U0)
    v_ref[...] = v
    vb_ref[...] = v.astype(BF16)
    u_ref[...] = seg(_U0, _Z0)
    z_ref[...] = seg(_Z0, _X0)
    xbc_ref[...] = seg(_X0, _MAIN)

    sm = jnp.dot(hb, ws_ref[...], preferred_element_type=F32) + bs_ref[...]
    lane = lax.broadcasted_iota(jnp.int32, sm.shape, 1)
    tail = _softplus_tail(sm)
    sm_ref[...] = jnp.where(lane < _DT_AT, jnp.minimum(sm, 0.0) - tail, jnp.maximum(sm, 0.0) + tail)
    smt = _nt(wst_ref[...], hb) + bst_ref[...]
    row = lax.broadcasted_iota(jnp.int32, smt.shape, 0)
    tail = _softplus_tail(smt)
    smt_ref[...] = jnp.where(row < _DT_AT, jnp.minimum(smt, 0.0) - tail, jnp.maximum(smt, 0.0) + tail)


def _inproj(x, g, wm, ws, wst, bs, bst, tm):
    t = x.shape[0]
    nt = t // tm
    row = lambda c: pl.BlockSpec((tm, c), lambda i: (i, 0))
    full = lambda a: pl.BlockSpec(a.shape, lambda i: (0,) * a.ndim)
    out_shape = (
        jax.ShapeDtypeStruct((t, D_FOX), BF16),
        jax.ShapeDtypeStruct((t, D_FOX), F32),
        jax.ShapeDtypeStruct((t, D_FOX), BF16),
        jax.ShapeDtypeStruct((t, D_FOX), F32),
        jax.ShapeDtypeStruct((t, D_FOX), BF16),
        jax.ShapeDtypeStruct((t, D_POOL), F32),
        jax.ShapeDtypeStruct((t, D_SSD), F32),
        jax.ShapeDtypeStruct((t, D_CONV), F32),
        jax.ShapeDtypeStruct((t, LANE), F32),
        jax.ShapeDtypeStruct((16, t), F32),
    )
    out_specs = (row(D_FOX), row(D_FOX), row(D_FOX), row(D_FOX), row(D_FOX), row(D_POOL),
                 row(D_SSD), row(D_CONV), row(LANE), pl.BlockSpec((16, tm), lambda i: (0, i)))
    return pl.pallas_call(
        _inproj_kernel, grid=(nt,),
        in_specs=[row(D_MODEL), full(g), full(wm), full(ws), full(wst), full(bs), full(bst)],
        out_specs=out_specs, out_shape=out_shape,
        compiler_params=_cparams(("parallel",)), name="inproj",
    )(x, g, wm, ws, wst, bs, bst)


def _cumsum_kernel(l_ref, c_ref, *, nblk):
    carry = jnp.zeros((8, 1), F32)
    for j in range(nblk):
        c = _scan_lanes(l_ref[:, j * LANE:(j + 1) * LANE]) + carry
        c_ref[0, j] = c
        carry = c[:, LANE - 1:LANE]


def _cumsum_prompt(smt, nb, tp):
    nblk = tp // LANE
    return pl.pallas_call(
        functools.partial(_cumsum_kernel, nblk=nblk), grid=(nb,),
        in_specs=[pl.BlockSpec((8, tp), lambda b: (0, b))],
        out_specs=pl.BlockSpec((1, nblk, 8, LANE), lambda b: (b, 0, 0, 0)),
        out_shape=jax.ShapeDtypeStruct((nb, nblk, 8, LANE), F32),
        compiler_params=_cparams(("parallel",)), name="logf_cumsum",
    )(smt)


def _softmax_step(s, m_old, l_old):
    m_new = jnp.maximum(m_old, jnp.max(s, axis=-1, keepdims=True))
    alpha = jnp.exp(m_old - m_new)
    p = jnp.exp(s - m_new)
    l_new = alpha * l_old + jnp.sum(p, axis=-1, keepdims=True)
    return p, alpha, m_new, l_new


def _fox_prompt_kernel(q_ref, k_ref, v_ref, c_ref, o_ref, *, tq):
    i = pl.program_id(1)
    lo = lax.broadcasted_iota(jnp.int32, (1, LANE), 1) < HD
    causal = (lax.broadcasted_iota(jnp.int32, (tq, tq), 1)
              <= lax.broadcasted_iota(jnp.int32, (tq, tq), 0))
    cq = c_ref[0, i]
    for p in range(H // 2):
        cols = slice(p * LANE, (p + 1) * LANE)
        q2 = q_ref[:, cols]
        qm = (jnp.where(lo, q2, jnp.zeros_like(q2)), jnp.where(lo, jnp.zeros_like(q2), q2))
        c0 = tuple(cq[2 * p + hh:2 * p + hh + 1, 0:1] for hh in range(2))

        def step(j, carry, masked, cols=cols, qm=qm, c0=c0, p=p):
            m, l, acc = carry
            rows = pl.ds(pl.multiple_of(j * tq, tq), tq)
            k2 = k_ref[rows, cols]
            v2 = v_ref[rows, cols]
            cj = c_ref[0, j]
            pv, al, mn, ln = [], [], [], []
            for hh in range(2):
                s = _nt(qm[hh], k2) + (c0[hh] - cj[2 * p + hh:2 * p + hh + 1, :])
                if masked:
                    s = jnp.where(causal, s, NEG)
                pm, a, m1, l1 = _softmax_step(s, m[hh], l[hh])
                pv.append(jnp.dot(pm.astype(BF16), v2, preferred_element_type=F32))
                al.append(a); mn.append(m1); ln.append(l1)
            acc = jnp.where(lo, al[0], al[1]) * acc + jnp.where(lo, pv[0], pv[1])
            return tuple(mn), tuple(ln), acc

        init = ((jnp.full((tq, 1), NEG, F32),) * 2, (jnp.zeros((tq, 1), F32),) * 2,
                jnp.zeros((tq, LANE), F32))
        carry = lax.fori_loop(0, i, functools.partial(step, masked=False), init)
        _, l, acc = step(i, carry, True)
        o_ref[:, cols] = (acc / jnp.where(lo, l[0], l[1])).astype(o_ref.dtype)


def _fox_prompt(qb, kb, vb, ct, nb, tp, tq):
    nq = tp // tq
    return pl.pallas_call(
        functools.partial(_fox_prompt_kernel, tq=tq), grid=(nb, nq),
        in_specs=[pl.BlockSpec((tq, D_FOX), lambda b, i: (b * nq + i, 0)),
                  pl.BlockSpec((tp, D_FOX), lambda b, i: (b, 0)),
                  pl.BlockSpec((tp, D_FOX), lambda b, i: (b, 0)),
                  pl.BlockSpec((1, tp // LANE, 8, LANE), lambda b, i: (b, 0, 0, 0))],
        out_specs=pl.BlockSpec((tq, D_FOX), lambda b, i: (b * nq + i, 0)),
        out_shape=jax.ShapeDtypeStruct((nb * tp, D_FOX), BF16),
        compiler_params=_cparams(("parallel", "arbitrary")), name="fox_prompt",
    )(qb, kb, vb, ct)


def _fox_sample_kernel(q_ref, kn_ref, vn_ref, kc_ref, vc_ref, lf_ref, o_ref, *, past, ls):
    lo = lax.broadcasted_iota(jnp.int32, (1, LANE), 1) < HD
    causal = (lax.broadcasted_iota(jnp.int32, (ls, ls), 1)
              <= lax.broadcasted_iota(jnp.int32, (ls, ls), 0))
    nblk = lf_ref.shape[2] // LANE
    carry = jnp.zeros((8, 1), F32)
    cs = []
    for j in range(nblk):
        c = _scan_lanes(lf_ref[0, :, j * LANE:(j + 1) * LANE]) + carry
        cs.append(c)
        carry = c[:, LANE - 1:LANE]
    c_all = jnp.concatenate(cs, axis=1)
    for p in range(H // 2):
        cols = slice(p * LANE, (p + 1) * LANE)
        q2 = q_ref[:, cols]
        kc = kc_ref[0, :, cols].astype(BF16)
        vc = vc_ref[0, :, cols].astype(BF16)
        kn = kn_ref[:, cols]
        vn = vn_ref[:, cols]
        outs, ls_ = [], []
        for hh in range(2):
            hrow = slice(2 * p + hh, 2 * p + hh + 1)
            qm = jnp.where(lo, q2, jnp.zeros_like(q2)) if hh == 0 else jnp.where(lo, jnp.zeros_like(q2), q2)
            c0 = c_all[hrow, past - 1:past]
            s_c = _nt(qm, kc) + (c0 - c_all[hrow, :past])
            s_n = _nt(qm, kn) + (c0 - c_all[hrow, past:past + ls])
            s_n = jnp.where(causal, s_n, NEG)
            m = jnp.maximum(jnp.max(s_c, axis=-1, keepdims=True), jnp.max(s_n, axis=-1, keepdims=True))
            p_c = jnp.exp(s_c - m)
            p_n = jnp.exp(s_n - m)
            ls_.append(jnp.sum(p_c, axis=-1, keepdims=True) + jnp.sum(p_n, axis=-1, keepdims=True))
            outs.append(jnp.dot(p_c.astype(BF16), vc, preferred_element_type=F32)
                        + jnp.dot(p_n.astype(BF16), vn, preferred_element_type=F32))
        o = jnp.where(lo, outs[0], outs[1]) / jnp.where(lo, ls_[0], ls_[1])
        o_ref[:, cols] = o.astype(o_ref.dtype)


def _fox_sample(qb, kb, vb, kc, vc, lf_all, nb, ls, past):
    return pl.pallas_call(
        functools.partial(_fox_sample_kernel, past=past, ls=ls), grid=(nb,),
        in_specs=[pl.BlockSpec((ls, D_FOX), lambda b: (b, 0)),
                  pl.BlockSpec((ls, D_FOX), lambda b: (b, 0)),
                  pl.BlockSpec((ls, D_FOX), lambda b: (b, 0)),
                  pl.BlockSpec((1, past, D_FOX), lambda b: (b, 0, 0)),
                  pl.BlockSpec((1, past, D_FOX), lambda b: (b, 0, 0)),
                  pl.BlockSpec((1, 8, past + LANE), lambda b: (b, 0, 0))],
        out_specs=pl.BlockSpec((ls, D_FOX), lambda b: (b, 0)),
        out_shape=jax.ShapeDtypeStruct((nb * ls, D_FOX), BF16),
        compiler_params=_cparams(("parallel",)), name="fox_sample",
    )(qb, kb, vb, kc, vc, lf_all)


def _pool_kernel(u_ref, pre_ref, w_ref, sc_ref, o_ref, halo_sc, *, tm, pos0):
    t = pl.program_id(1)

    @pl.when(t == 0)
    def _():
        halo_sc[...] = pre_ref[0]

    u = u_ref[...]
    a = jnp.concatenate([halo_sc[...], u], axis=0)
    e1 = a + pltpu.roll(a, 1, 0)
    e2 = e1 + pltpu.roll(e1, 2, 0)
    e3 = e2 + pltpu.roll(e2, 4, 0)
    e4 = e3 + pltpu.roll(e3, 8, 0)
    lane = lax.broadcasted_iota(jnp.int32, (1, D_POOL), 1)
    win = jnp.where(lane < 64, e1, jnp.where(lane < 128, e2, jnp.where(lane < 192, e3, e4)))[16:]
    wsz = jnp.where(lane < 64, 2.0, jnp.where(lane < 128, 4.0, jnp.where(lane < 192, 8.0, 16.0)))
    pos = (pos0 + t * tm + lax.broadcasted_iota(jnp.int32, (tm, 1), 0)).astype(F32)
    diff = win / jnp.minimum(pos + 1.0, wsz) - u
    y = jnp.dot(diff.astype(BF16), w_ref[...], preferred_element_type=F32) * sc_ref[...]
    o_ref[...] = y.astype(o_ref.dtype)
    halo_sc[...] = u[tm - 16:]


def _pool(u, prefix, wbd, scale, nb, nt, tm, pos0):
    return pl.pallas_call(
        functools.partial(_pool_kernel, tm=tm, pos0=pos0), grid=(nb, nt),
        in_specs=[pl.BlockSpec((tm, D_POOL), lambda b, t: (b * nt + t, 0)),
                  pl.BlockSpec((1, 16, D_POOL), lambda b, t: (b, 0, 0)),
                  pl.BlockSpec((D_POOL, D_POOL), lambda b, t: (0, 0)),
                  pl.BlockSpec((1, D_POOL), lambda b, t: (0, 0))],
        out_specs=pl.BlockSpec((tm, D_POOL), lambda b, t: (b * nt + t, 0)),
        out_shape=jax.ShapeDtypeStruct((nb * nt * tm, D_POOL), BF16),
        scratch_shapes=[pltpu.VMEM((16, D_POOL), F32)],
        compiler_params=_cparams(("parallel", "arbitrary")), name="pool_mixer",
    )(u, prefix, wbd, scale)


def _ssd_kernel(xbc_ref, z_ref, sm_ref, smt_ref, pre_ref, init_ref, cw_ref, cb_ref, arow_ref,
                acol_ref, dsk_ref, gn_ref, y_ref, fin_ref, halo_sc, st_sc, *, q, l_valid):
    c = pl.program_id(1)

    @pl.when(c == 0)
    def _():
        halo_sc[...] = pre_ref[0]
        st_sc[...] = init_ref[0]

    x = xbc_ref[...]
    a = jnp.concatenate([halo_sc[...], x], axis=0)
    w = cw_ref[...]
    conv = (cb_ref[...] + w[3:4] * a + w[2:3] * pltpu.roll(a, 1, 0)
            + w[1:2] * pltpu.roll(a, 2, 0) + w[0:1] * pltpu.roll(a, 3, 0))
    halo_sc[...] = x[q - 8:]
    act = _silu(conv[8:])
    xs = act[:, :D_SSD]
    bb = act[:, D_SSD:D_SSD + D_BC].astype(BF16)
    cc = act[:, D_SSD + D_BC:].astype(BF16)

    rvalid = (c * q + lax.broadcasted_iota(jnp.int32, (q, 1), 0)) < l_valid
    dtc = jnp.where(rvalid, sm_ref[...], 0.0)
    acs_c = _scan_rows(dtc * (-jnp.exp(arow_ref[...])))
    cvalid = (c * q + lax.broadcasted_iota(jnp.int32, (1, LANE), 1)) < l_valid
    dtt = jnp.where(cvalid, smt_ref[...], 0.0)
    acs_t = _scan_lanes(dtt * (-jnp.exp(acol_ref[...])))

    causal = (lax.broadcasted_iota(jnp.int32, (q, q), 1)
              <= lax.broadcasted_iota(jnp.int32, (q, q), 0))
    cb = [_nt(cc[:, g * HD:(g + 1) * HD], bb[:, g * HD:(g + 1) * HD]) for g in range(2)]
    ys = []
    for h in range(H):
        g = h // (H // 2)
        hs = slice(h * HD, (h + 1) * HD)
        aq = acs_c[:, _DT_AT + h:_DT_AT + h + 1]
        ak = acs_t[_DT_AT + h:_DT_AT + h + 1, :q]
        lm = jnp.exp(jnp.where(causal, aq - ak, NEG))
        gm = (cb[g] * lm).astype(BF16)
        xh = xs[:, hs]
        xd = xh * dtc[:, _DT_AT + h:_DT_AT + h + 1]
        y = jnp.dot(gm, xd.astype(BF16), preferred_element_type=F32)
        s_prev = st_sc[h]
        cg = cc[:, g * HD:(g + 1) * HD]
        y = y + _nt(cg, s_prev.astype(BF16)) * jnp.exp(aq)
        alast = acs_c[q - 1:q, _DT_AT + h:_DT_AT + h + 1]
        xw = (xd * jnp.exp(alast - aq)).astype(BF16)
        st_sc[h] = jnp.exp(alast) * s_prev + _tn(xw, bb[:, g * HD:(g + 1) * HD])
        ys.append(y + dsk_ref[:, hs] * xh)
    yc = jnp.concatenate(ys, axis=1) * _silu(z_ref[...])
    y_ref[...] = _rms(yc, gn_ref[...]).astype(y_ref.dtype)

    @pl.when(c == pl.num_programs(1) - 1)
    def _():
        fin_ref[0] = st_sc[...]


def _ssd(xbc, z, sm, smt, prefix, init, cw, cb, arow, acol, dsk, gn, nb, nc, q, l_valid):
    rows = lambda w: pl.BlockSpec((q, w), lambda b, c: (b * nc + c, 0))
    const = lambda a: pl.BlockSpec(a.shape, lambda b, c: (0,) * a.ndim)
    if smt.ndim == 2:
        smt_spec = pl.BlockSpec((16, LANE), lambda b, c: (0, b * nc + c))
    else:
        smt_spec = pl.BlockSpec((None, 16, LANE), lambda b, c: (b, 0, 0))
    return pl.pallas_call(
        functools.partial(_ssd_kernel, q=q, l_valid=l_valid), grid=(nb, nc),
        in_specs=[rows(D_CONV), rows(D_SSD), rows(LANE), smt_spec,
                  pl.BlockSpec((1, 8, D_CONV), lambda b, c: (b, 0, 0)),
                  pl.BlockSpec((1, H, HD, HD), lambda b, c: (b, 0, 0, 0)),
                  const(cw), const(cb), const(arow), const(acol), const(dsk), const(gn)],
        out_specs=(rows(D_SSD), pl.BlockSpec((1, H, HD, HD), lambda b, c: (b, 0, 0, 0))),
        out_shape=(jax.ShapeDtypeStruct((nb * nc * q, D_SSD), BF16),
                   jax.ShapeDtypeStruct((nb, H, HD, HD), F32)),
        scratch_shapes=[pltpu.VMEM((8, D_CONV), F32), pltpu.VMEM((H, HD, HD), F32)],
        compiler_params=_cparams(("parallel", "arbitrary")), name="conv_ssd",
    )(xbc, z, sm, smt, prefix, init, cw, cb, arow, acol, dsk, gn)


def _ffn_kernel(x_ref, a_ref, p_ref, s_ref, woa_ref, wop_ref, wos_ref, g1_ref, g2_ref, g3_ref,
                wg_ref, wu_ref, wd_ref, o_ref):
    mp = (jnp.dot(a_ref[...], woa_ref[...], preferred_element_type=F32)
          + jnp.dot(p_ref[...], wop_ref[...], preferred_element_type=F32)
          + jnp.dot(s_ref[...], wos_ref[...], preferred_element_type=F32))
    x1 = x_ref[...] + _rms(mp, g1_ref[...])
    hb = _rms(x1, g2_ref[...]).astype(BF16)
    gate = jnp.dot(hb, wg_ref[...], preferred_element_type=F32)
    up = jnp.dot(hb, wu_ref[...], preferred_element_type=F32)
    act = (_silu(gate) * up).astype(BF16)
    ff = jnp.dot(act, wd_ref[...], preferred_element_type=F32)
    o_ref[...] = x1 + _rms(ff, g3_ref[...])


def _ffn(x, attn, pool, ssd, woa, wop, wos, g1, g2, g3, wg, wu, wd, tm):
    t = x.shape[0]
    row = lambda c: pl.BlockSpec((tm, c), lambda i: (i, 0))
    const = lambda a: pl.BlockSpec(a.shape, lambda i: (0,) * a.ndim, pipeline_mode=pl.Buffered(1))
    return pl.pallas_call(
        _ffn_kernel, grid=(t // tm,),
        in_specs=[row(D_MODEL), row(D_FOX), row(D_POOL), row(D_SSD), const(woa), const(wop), const(wos),
                  const(g1), const(g2), const(g3), const(wg), const(wu), const(wd)],
        out_specs=row(D_MODEL), out_shape=jax.ShapeDtypeStruct((t, D_MODEL), F32),
        compiler_params=_cparams(("parallel",)), name="outproj_ffn",
    )(x, attn, pool, ssd, woa, wop, wos, g1, g2, g3, wg, wu, wd)


def _lane_pack(f_vals, dt_vals):
    out = jnp.zeros((LANE,), F32)
    return out.at[_F_AT:_F_AT + H].set(f_vals.astype(F32)).at[_DT_AT:_DT_AT + H].set(dt_vals.astype(F32))


def _prep_layer(w_in, fox_f_bias, pool_w, pool_scale, conv_w, conv_b, dt_bias, a_log, d_skip, ssd_norm,
                w_out, w_gate, w_up, w_down, ln_pre_mix, ln_post_mix, ln_pre_ffn, ln_post_ffn):
    f0 = 3 * D_FOX
    u0 = f0 + H
    dt0 = u0 + D_POOL + D_SSD + D_CONV
    wm = jnp.concatenate([w_in[:, :f0], w_in[:, u0:dt0]], axis=1).astype(BF16)
    ws = jnp.zeros((D_MODEL, LANE), F32)
    ws = ws.at[:, _F_AT:_F_AT + H].set(w_in[:, f0:u0]).at[:, _DT_AT:_DT_AT + H].set(w_in[:, dt0:dt0 + H])
    ws = ws.astype(BF16)
    wst = ws[:, :16].T
    bs = _lane_pack(fox_f_bias, dt_bias)
    wbd = jnp.zeros((D_POOL, D_POOL), F32)
    for g in range(4):
        wbd = wbd.at[g * 64:(g + 1) * 64, g * 64:(g + 1) * 64].set(pool_w[g])
    alog = _lane_pack(jnp.zeros((H,), F32), a_log)
    row = lambda a: a.astype(F32).reshape(1, -1)
    return dict(
        g_pre=row(ln_pre_mix), wm=wm, ws=ws, wst=wst, bs=bs.reshape(1, LANE), bst=bs[:16].reshape(16, 1),
        wbd=wbd.astype(BF16), pscale=row(pool_scale),
        cw=jnp.zeros((8, D_CONV), F32).at[:CONV_W].set(conv_w), cb=row(conv_b),
        arow=alog.reshape(1, LANE), acol=alog[:16].reshape(16, 1),
        dsk=row(jnp.repeat(d_skip, HD)), gn=row(ssd_norm),
        woa=w_out[:D_FOX].astype(BF16), wop=w_out[D_FOX:D_FOX + D_POOL].astype(BF16),
        wos=w_out[D_FOX + D_POOL:].astype(BF16),
        g1=row(ln_post_mix), g2=row(ln_pre_ffn), g3=row(ln_post_ffn),
        wg=w_gate.astype(BF16), wu=w_up.astype(BF16), wd=w_down.astype(BF16))


def _mixers_and_ffn(x, w, attn_fn, nb, nc, q, l_valid, pos0, pool_prefix, conv_prefix, ssd_init, smt_fn,
                    tm_tok, tm_pool):
    qb, k, kb, v, vb, u, z, xbc, sm, smt = _inproj(x, w["g_pre"], w["wm"], w["ws"], w["wst"], w["bs"],
                                                   w["bst"], tm_tok)
    attn = attn_fn(qb, kb, vb, smt)
    seq = nc * q
    pool = _pool(u, pool_prefix, w["wbd"], w["pscale"], nb, seq // tm_pool, tm_pool, pos0)
    ssd, fin = _ssd(xbc, z, sm, smt_fn(smt), conv_prefix, ssd_init, w["cw"], w["cb"], w["arow"], w["acol"],
                    w["dsk"], w["gn"], nb, nc, q, l_valid)
    x = _ffn(x, attn, pool, ssd, w["woa"], w["wop"], w["wos"], w["g1"], w["g2"], w["g3"],
             w["wg"], w["wu"], w["wd"], tm_tok)
    return x, (k, v, sm, u, xbc, fin)


def kernel(x_prompt, x_sample, cache_fox_k, cache_fox_v, cache_fox_logf, state_pool, state_conv, state_ssd,
           meta_tokens, ln_pre_mix, ln_post_mix, ln_pre_ffn, ln_post_ffn, w_in, fox_f_bias, pool_w, pool_scale,
           conv_w, conv_b, dt_bias, a_log, d_skip, ssd_norm, w_out, w_gate, w_up, w_down):
    nbp, seq, _ = x_prompt.shape
    nbs, ls, _ = x_sample.shape
    depth, _, past = cache_fox_logf.shape[:3]
    lp = N_META + seq
    tq = 128
    tp = -(-lp // tq) * tq
    tm_tok = 256

    meta = jnp.broadcast_to(meta_tokens.astype(F32)[None], (nbp, N_META, D_MODEL))
    xp = jnp.concatenate([meta, x_prompt, jnp.zeros((nbp, tp - lp, D_MODEL), F32)], axis=1)
    xp = xp.reshape(nbp * tp, D_MODEL)
    xs = x_sample.reshape(nbs * ls, D_MODEL)

    zero_pool = jnp.zeros((nbp, 16, D_POOL), F32)
    zero_conv = jnp.zeros((nbp, 8, D_CONV), F32)
    zero_ssd = jnp.zeros((nbp, H, HD, HD), F32)

    outs_p, outs_s = [], []
    for l in range(depth):
        w = _prep_layer(w_in[l], fox_f_bias[l], pool_w[l], pool_scale[l], conv_w[l], conv_b[l], dt_bias[l],
                        a_log[l], d_skip[l], ssd_norm[l], w_out[l], w_gate[l], w_up[l], w_down[l],
                        ln_pre_mix[l], ln_post_mix[l], ln_pre_ffn[l], ln_post_ffn[l])

        def attn_p(qb, kb, vb, smt):
            return _fox_prompt(qb, kb, vb, _cumsum_prompt(smt, nbp, tp), nbp, tp, tq)

        xp, st = _mixers_and_ffn(xp, w, attn_p, nbp, tp // tq, tq, lp, 0, zero_pool, zero_conv, zero_ssd,
                                 lambda smt: smt, tm_tok, tq)
        outs_p.append(st)

        kc = cache_fox_k[l].reshape(nbs, past, D_FOX)
        vc = cache_fox_v[l].reshape(nbs, past, D_FOX)
        lfc = jnp.transpose(cache_fox_logf[l].astype(F32), (0, 2, 1))

        def to_seq_major(smt):
            return jnp.transpose(smt.reshape(16, nbs, ls), (1, 0, 2))

        def attn_s(qb, kb, vb, smt):
            lfn = to_seq_major(smt)[:, :8]
            lf_all = jnp.concatenate([jnp.pad(lfc, ((0, 0), (0, 8 - H), (0, 0))), lfn,
                                      jnp.zeros((nbs, 8, LANE - ls), F32)], axis=2)
            return _fox_sample(qb, kb, vb, kc, vc, lf_all, nbs, ls, past)

        pool_pre = jnp.pad(state_pool[l].astype(F32), ((0, 0), (16 - POOL_BUF, 0), (0, 0)))
        conv_pre = jnp.pad(state_conv[l].astype(F32), ((0, 0), (8 - (CONV_W - 1), 0), (0, 0)))
        xs, st = _mixers_and_ffn(xs, w, attn_s, nbs, 1, ls, ls, past, pool_pre, conv_pre,
                                 state_ssd[l].astype(F32),
                                 lambda smt: jnp.pad(to_seq_major(smt), ((0, 0), (0, 0), (0, LANE - ls))),
                                 tm_tok, ls)
        outs_s.append(st)

    def collect(outs, nb, rows, valid):
        seqv = lambda a: a.reshape(nb, rows, a.shape[-1])
        k = jnp.stack([seqv(o[0])[:, :valid].reshape(nb, valid, H, HD) for o in outs])
        v = jnp.stack([seqv(o[1])[:, :valid].reshape(nb, valid, H, HD) for o in outs])
        lf = jnp.stack([seqv(o[2])[:, :valid, _F_AT:_F_AT + H] for o in outs])
        pn = jnp.stack([seqv(o[3])[:, valid - POOL_BUF:valid] for o in outs])
        cn = jnp.stack([seqv(o[4])[:, valid - (CONV_W - 1):valid] for o in outs])
        sn = jnp.stack([o[5] for o in outs])
        return k, v, lf, pn, cn, sn

    y_prompt = xp.reshape(nbp, tp, D_MODEL)[:, N_META:lp]
    y_sample = xs.reshape(nbs, ls, D_MODEL)
    return (y_prompt, y_sample) + collect(outs_p, nbp, tp, lp) + collect(outs_s, nbs, ls, ls)
```

```python
import functools

import jax
import jax.numpy as jnp
from jax import lax
from jax.experimental import pallas as pl
from jax.experimental.pallas import tpu as pltpu

F32 = jnp.float32
BF16 = jnp.bfloat16

D_MODEL = 1024
N_META = 16
EPS = 1e-6
H = 6
HD = 64
D_FOX = H * HD
D_POOL = 256
POOL_BUF = 15
D_SSD = H * HD
D_BC = 128
D_CONV = D_SSD + 2 * D_BC
CONV_W = 4
D_FF = 2816
LANE = 128
NEG = -1e30

_Q0, _K0, _V0, _U0, _Z0, _X0, _MAIN = 0, 384, 768, 1152, 1408, 1792, 2432
_F_AT, _DT_AT, _F_COPIES = 0, 8, (0, 16, 32)

_VMEM_LIMIT = 56 * 1024 * 1024


def _cparams(sem):
    return pltpu.CompilerParams(dimension_semantics=sem, vmem_limit_bytes=_VMEM_LIMIT)


def _rms(x, g):
    ms = jnp.mean(x * x, axis=-1, keepdims=True)
    return x * lax.rsqrt(ms + EPS) * g


def _silu(x):
    return x * (1.0 / (1.0 + jnp.exp(-x)))


def _softplus_tail(x):
    return jnp.log1p(jnp.exp(-jnp.abs(x)))


def _nt(a, b):
    return lax.dot_general(a, b, (((1,), (1,)), ((), ())), preferred_element_type=F32)


def _tn(a, b):
    return lax.dot_general(a, b, (((0,), (0,)), ((), ())), preferred_element_type=F32)


def _scan_lanes(x):
    lane = lax.broadcasted_iota(jnp.int32, x.shape, 1)
    s = 1
    while s < x.shape[1]:
        x = x + jnp.where(lane >= s, pltpu.roll(x, s, 1), 0.0)
        s *= 2
    return x


def _scan_rows(x):
    row = lax.broadcasted_iota(jnp.int32, x.shape, 0)
    s = 1
    while s < x.shape[0]:
        x = x + jnp.where(row >= s, pltpu.roll(x, s, 0), 0.0)
        s *= 2
    return x


def _inproj_kernel(x_ref, g_ref, wm_ref, wvt_ref, ws_ref, wst_ref, bs_ref, bst_ref,
                   qb_ref, k_ref, kb_ref, v_ref, vb_ref, u_ref, z_ref, xbc_ref, sm_ref, smt_ref,
                   *, transposed_v):
    hb = _rms(x_ref[...], g_ref[...]).astype(BF16)

    def seg(a, b):
        return jnp.dot(hb, wm_ref[:, a:b], preferred_element_type=F32)

    qb_ref[...] = (seg(_Q0, _K0) * (HD ** -0.5)).astype(BF16)
    k = seg(_K0, _V0)
    k_ref[...] = k
    kb_ref[...] = k.astype(BF16)
    v = seg(_V0, _U0)
    v_ref[...] = v
    if transposed_v:
        vb_ref[0] = _nt(wvt_ref[...], hb).astype(BF16)
    else:
        vb_ref[...] = v.astype(BF16)
    u_ref[...] = seg(_U0, _Z0)
    z_ref[...] = seg(_Z0, _X0)
    xbc_ref[...] = seg(_X0, _MAIN)

    sm = jnp.dot(hb, ws_ref[...], preferred_element_type=F32) + bs_ref[...]
    lane = lax.broadcasted_iota(jnp.int32, sm.shape, 1)
    tail = _softplus_tail(sm)
    is_dt = (lane >= _DT_AT) & (lane < _DT_AT + 8)
    sm_ref[...] = jnp.where(is_dt, jnp.maximum(sm, 0.0) + tail, jnp.minimum(sm, 0.0) - tail)
    smt = _nt(wst_ref[...], hb) + bst_ref[...]
    row = lax.broadcasted_iota(jnp.int32, smt.shape, 0)
    tail = _softplus_tail(smt)
    smt_ref[...] = jnp.where(row < _DT_AT, jnp.minimum(smt, 0.0) - tail, jnp.maximum(smt, 0.0) + tail)


def _inproj(x, g, wm, wvt, ws, wst, bs, bst, tm, transposed_v):
    t = x.shape[0]
    nt = t // tm
    row = lambda c: pl.BlockSpec((tm, c), lambda i: (i, 0))
    full = lambda a: pl.BlockSpec(a.shape, lambda i: (0,) * a.ndim)
    if transposed_v:
        vb_shape = jax.ShapeDtypeStruct((nt, D_FOX, tm), BF16)
        vb_spec = pl.BlockSpec((1, D_FOX, tm), lambda i: (i, 0, 0))
    else:
        vb_shape = jax.ShapeDtypeStruct((t, D_FOX), BF16)
        vb_spec = row(D_FOX)
    out_shape = (
        jax.ShapeDtypeStruct((t, D_FOX), BF16),
        jax.ShapeDtypeStruct((t, D_FOX), F32),
        jax.ShapeDtypeStruct((t, D_FOX), BF16),
        jax.ShapeDtypeStruct((t, D_FOX), F32),
        vb_shape,
        jax.ShapeDtypeStruct((t, D_POOL), F32),
        jax.ShapeDtypeStruct((t, D_SSD), F32),
        jax.ShapeDtypeStruct((t, D_CONV), F32),
        jax.ShapeDtypeStruct((t, LANE), F32),
        jax.ShapeDtypeStruct((16, t), F32),
    )
    out_specs = (row(D_FOX), row(D_FOX), row(D_FOX), row(D_FOX), vb_spec, row(D_POOL),
                 row(D_SSD), row(D_CONV), row(LANE), pl.BlockSpec((16, tm), lambda i: (0, i)))
    return pl.pallas_call(
        functools.partial(_inproj_kernel, transposed_v=transposed_v), grid=(nt,),
        in_specs=[row(D_MODEL), full(g), full(wm), full(wvt), full(ws), full(wst), full(bs), full(bst)],
        out_specs=out_specs, out_shape=out_shape,
        compiler_params=_cparams(("parallel",)), name="inproj",
    )(x, g, wm, wvt, ws, wst, bs, bst)


def _forget_bias_kernel(sm_ref, e_ref, *, nblk, rows):
    lane = lax.broadcasted_iota(jnp.int32, (1, LANE), 1)
    keep = ((lane & 15) < H) & (lane < 48)
    carry = jnp.zeros((1, LANE), F32)
    for j in range(nblk):
        blk = slice(j * rows, (j + 1) * rows)
        c = _scan_rows(sm_ref[blk, :]) + carry
        carry = c[rows - 1:rows, :]
        hi = c.astype(BF16).astype(F32)
        mid = (c - hi).astype(BF16).astype(F32)
        low = (c - hi) - mid
        e = jnp.where(lane < 16, hi, jnp.where(lane < 32, mid, low))
        e_ref[blk, :] = jnp.where(keep, e, 0.0).astype(BF16)


def _forget_bias(sm, nb, tp):
    rows = LANE
    return pl.pallas_call(
        functools.partial(_forget_bias_kernel, nblk=tp // rows, rows=rows), grid=(nb,),
        in_specs=[pl.BlockSpec((tp, LANE), lambda b: (b, 0))],
        out_specs=pl.BlockSpec((tp, LANE), lambda b: (b, 0)),
        out_shape=jax.ShapeDtypeStruct((nb * tp, LANE), BF16),
        compiler_params=_cparams(("parallel",)), name="forget_bias",
    )(sm)


def _fox_prompt_kernel(q_ref, k_ref, e_ref, vt_ref, o_ref, qa_sc, m_sc, l_sc, acc_sc, sa_sc, sb_sc,
                       *, tq):
    i = pl.program_id(1)
    lane = lax.broadcasted_iota(jnp.int32, (1, LANE), 1)
    causal = (lax.broadcasted_iota(jnp.int32, (tq, tq), 0)
              <= lax.broadcasted_iota(jnp.int32, (tq, tq), 1))
    for h in range(H):
        q2 = q_ref[:, (h // 2) * LANE:(h // 2 + 1) * LANE]
        mine = (lane < HD) if h % 2 == 0 else (lane >= HD)
        qm = jnp.where(mine, q2, jnp.zeros_like(q2))
        pick = (lane == _F_COPIES[0] + h) | (lane == _F_COPIES[1] + h) | (lane == _F_COPIES[2] + h)
        qe = jnp.broadcast_to(jnp.where(pick, -1.0, 0.0), (tq, LANE)).astype(BF16)
        qa_sc[h] = jnp.concatenate([qm, qe], axis=1)
    m_sc[...] = jnp.full(m_sc.shape, NEG, F32)
    l_sc[...] = jnp.zeros(l_sc.shape, F32)
    acc_sc[...] = jnp.zeros(acc_sc.shape, F32)

    def logits(j, s_sc):
        rows = pl.ds(pl.multiple_of(j * tq, tq), tq)
        kk = k_ref[rows, :]
        ee = e_ref[rows, :]
        for h in range(H):
            ka = jnp.concatenate([kk[:, (h // 2) * LANE:(h // 2 + 1) * LANE], ee], axis=1)
            s_sc[h] = _nt(ka, qa_sc[h])

    def softmax_pv(j, s_sc, masked):
        for h in range(H):
            s = s_sc[h]
            if masked:
                s = jnp.where(causal, s, NEG)
            m_old = m_sc[h:h + 1, :]
            m_new = jnp.maximum(m_old, jnp.max(s, axis=0, keepdims=True))
            alpha = jnp.exp(m_old - m_new)
            pm = jnp.exp(s - m_new)
            l_sc[h:h + 1, :] = alpha * l_sc[h:h + 1, :] + jnp.sum(pm, axis=0, keepdims=True)
            m_sc[h:h + 1, :] = m_new
            hs = slice(h * HD, (h + 1) * HD)
            pv = jnp.dot(vt_ref[j, hs, :], pm.astype(BF16), preferred_element_type=F32)
            acc_sc[hs, :] = alpha * acc_sc[hs, :] + pv

    def body(jj, carry):
        j = 2 * jj
        logits(j + 1, sb_sc)
        softmax_pv(j, sa_sc, False)
        logits(j + 2, sa_sc)
        softmax_pv(j + 1, sb_sc, False)
        return carry

    logits(0, sa_sc)
    lax.fori_loop(0, i // 2, body, 0)

    @pl.when(i % 2 == 0)
    def _():
        softmax_pv(i, sa_sc, True)

    @pl.when(i % 2 == 1)
    def _():
        logits(i, sb_sc)
        softmax_pv(i - 1, sa_sc, False)
        softmax_pv(i, sb_sc, True)
    out = jnp.concatenate([acc_sc[h * HD:(h + 1) * HD, :] / l_sc[h:h + 1, :] for h in range(H)], axis=0)
    o_ref[...] = out.T.astype(o_ref.dtype)


def _fox_prompt(qb, kb, eb, vt, nb, tp, tq):
    nq = tp // tq
    return pl.pallas_call(
        functools.partial(_fox_prompt_kernel, tq=tq), grid=(nb, nq),
        in_specs=[pl.BlockSpec((tq, D_FOX), lambda b, i: (b * nq + i, 0)),
                  pl.BlockSpec((tp, D_FOX), lambda b, i: (b, 0)),
                  pl.BlockSpec((tp, LANE), lambda b, i: (b, 0)),
                  pl.BlockSpec((nq, D_FOX, tq), lambda b, i: (b, 0, 0))],
        out_specs=pl.BlockSpec((tq, D_FOX), lambda b, i: (b * nq + i, 0)),
        out_shape=jax.ShapeDtypeStruct((nb * tp, D_FOX), BF16),
        scratch_shapes=[pltpu.VMEM((H, tq, 2 * LANE), BF16), pltpu.VMEM((8, tq), F32),
                        pltpu.VMEM((8, tq), F32), pltpu.VMEM((D_FOX, tq), F32),
                        pltpu.VMEM((H, tq, tq), F32), pltpu.VMEM((H, tq, tq), F32)],
        compiler_params=_cparams(("parallel", "arbitrary")), name="fox_prompt",
    )(qb, kb, eb, vt)


def _fox_sample_kernel(q_ref, kn_ref, vn_ref, kc_ref, vc_ref, lf_ref, o_ref, *, past, ls):
    lo = lax.broadcasted_iota(jnp.int32, (1, LANE), 1) < HD
    causal = (lax.broadcasted_iota(jnp.int32, (ls, ls), 1)
              <= lax.broadcasted_iota(jnp.int32, (ls, ls), 0))
    nblk = lf_ref.shape[2] // LANE
    carry = jnp.zeros((8, 1), F32)
    cs = []
    for j in range(nblk):
        c = _scan_lanes(lf_ref[0, :, j * LANE:(j + 1) * LANE]) + carry
        cs.append(c)
        carry = c[:, LANE - 1:LANE]
    c_all = jnp.concatenate(cs, axis=1)
    for p in range(H // 2):
        cols = slice(p * LANE, (p + 1) * LANE)
        q2 = q_ref[:, cols]
        kc = kc_ref[0, :, cols].astype(BF16)
        vc = vc_ref[0, :, cols].astype(BF16)
        kn = kn_ref[:, cols]
        vn = vn_ref[:, cols]
        outs, ls_ = [], []
        for hh in range(2):
            hrow = slice(2 * p + hh, 2 * p + hh + 1)
            qm = jnp.where(lo, q2, jnp.zeros_like(q2)) if hh == 0 else jnp.where(lo, jnp.zeros_like(q2), q2)
            c0 = c_all[hrow, past - 1:past]
            s_c = _nt(qm, kc) + (c0 - c_all[hrow, :past])
            s_n = _nt(qm, kn) + (c0 - c_all[hrow, past:past + ls])
            s_n = jnp.where(causal, s_n, NEG)
            m = jnp.maximum(jnp.max(s_c, axis=-1, keepdims=True), jnp.max(s_n, axis=-1, keepdims=True))
            p_c = jnp.exp(s_c - m)
            p_n = jnp.exp(s_n - m)
            ls_.append(jnp.sum(p_c, axis=-1, keepdims=True) + jnp.sum(p_n, axis=-1, keepdims=True))
            outs.append(jnp.dot(p_c.astype(BF16), vc, preferred_element_type=F32)
                        + jnp.dot(p_n.astype(BF16), vn, preferred_element_type=F32))
        o = jnp.where(lo, outs[0], outs[1]) / jnp.where(lo, ls_[0], ls_[1])
        o_ref[:, cols] = o.astype(o_ref.dtype)


def _fox_sample(qb, kb, vb, kc, vc, lf_all, nb, ls, past):
    return pl.pallas_call(
        functools.partial(_fox_sample_kernel, past=past, ls=ls), grid=(nb,),
        in_specs=[pl.BlockSpec((ls, D_FOX), lambda b: (b, 0)),
                  pl.BlockSpec((ls, D_FOX), lambda b: (b, 0)),
                  pl.BlockSpec((ls, D_FOX), lambda b: (b, 0)),
                  pl.BlockSpec((1, past, D_FOX), lambda b: (b, 0, 0)),
                  pl.BlockSpec((1, past, D_FOX), lambda b: (b, 0, 0)),
                  pl.BlockSpec((1, 8, past + LANE), lambda b: (b, 0, 0))],
        out_specs=pl.BlockSpec((ls, D_FOX), lambda b: (b, 0)),
        out_shape=jax.ShapeDtypeStruct((nb * ls, D_FOX), BF16),
        compiler_params=_cparams(("parallel",)), name="fox_sample",
    )(qb, kb, vb, kc, vc, lf_all)


def _pool_kernel(u_ref, pre_ref, w_ref, sc_ref, o_ref, halo_sc, *, tm, pos0):
    t = pl.program_id(1)

    @pl.when(t == 0)
    def _():
        halo_sc[...] = pre_ref[0]

    u = u_ref[...]
    a = jnp.concatenate([halo_sc[...], u], axis=0)
    e1 = a + pltpu.roll(a, 1, 0)
    e2 = e1 + pltpu.roll(e1, 2, 0)
    e3 = e2 + pltpu.roll(e2, 4, 0)
    e4 = e3 + pltpu.roll(e3, 8, 0)
    lane = lax.broadcasted_iota(jnp.int32, (1, D_POOL), 1)
    win = jnp.where(lane < 64, e1, jnp.where(lane < 128, e2, jnp.where(lane < 192, e3, e4)))[16:]
    wsz = jnp.where(lane < 64, 2.0, jnp.where(lane < 128, 4.0, jnp.where(lane < 192, 8.0, 16.0)))
    pos = (pos0 + t * tm + lax.broadcasted_iota(jnp.int32, (tm, 1), 0)).astype(F32)
    diff = win / jnp.minimum(pos + 1.0, wsz) - u
    y = jnp.dot(diff.astype(BF16), w_ref[...], preferred_element_type=F32) * sc_ref[...]
    o_ref[...] = y.astype(o_ref.dtype)
    halo_sc[...] = u[tm - 16:]


def _pool(u, prefix, wbd, scale, nb, nt, tm, pos0):
    return pl.pallas_call(
        functools.partial(_pool_kernel, tm=tm, pos0=pos0), grid=(nb, nt),
        in_specs=[pl.BlockSpec((tm, D_POOL), lambda b, t: (b * nt + t, 0)),
                  pl.BlockSpec((1, 16, D_POOL), lambda b, t: (b, 0, 0)),
                  pl.BlockSpec((D_POOL, D_POOL), lambda b, t: (0, 0)),
                  pl.BlockSpec((1, D_POOL), lambda b, t: (0, 0))],
        out_specs=pl.BlockSpec((tm, D_POOL), lambda b, t: (b * nt + t, 0)),
        out_shape=jax.ShapeDtypeStruct((nb * nt * tm, D_POOL), BF16),
        scratch_shapes=[pltpu.VMEM((16, D_POOL), F32)],
        compiler_params=_cparams(("parallel", "arbitrary")), name="pool_mixer",
    )(u, prefix, wbd, scale)


def _ssd_kernel(xbc_ref, z_ref, sm_ref, smt_ref, pre_ref, init_ref, cw_ref, cb_ref, arow_ref,
                acol_ref, dsk_ref, gn_ref, y_ref, fin_ref, halo_sc, st_sc, *, q, l_valid):
    c = pl.program_id(1)

    @pl.when(c == 0)
    def _():
        halo_sc[...] = pre_ref[0]
        st_sc[...] = init_ref[0]

    x = xbc_ref[...]
    a = jnp.concatenate([halo_sc[...], x], axis=0)
    w = cw_ref[...]
    conv = (cb_ref[...] + w[3:4] * a + w[2:3] * pltpu.roll(a, 1, 0)
            + w[1:2] * pltpu.roll(a, 2, 0) + w[0:1] * pltpu.roll(a, 3, 0))
    halo_sc[...] = x[q - 8:]
    act = _silu(conv[8:])
    xs = act[:, :D_SSD]
    bb = act[:, D_SSD:D_SSD + D_BC].astype(BF16)
    cc = act[:, D_SSD + D_BC:].astype(BF16)

    rvalid = (c * q + lax.broadcasted_iota(jnp.int32, (q, 1), 0)) < l_valid
    dtc = jnp.where(rvalid, sm_ref[...], 0.0)
    acs_c = _scan_rows(dtc * (-jnp.exp(arow_ref[...])))
    cvalid = (c * q + lax.broadcasted_iota(jnp.int32, (1, LANE), 1)) < l_valid
    dtt = jnp.where(cvalid, smt_ref[...], 0.0)
    acs_t = _scan_lanes(dtt * (-jnp.exp(acol_ref[...])))

    causal = (lax.broadcasted_iota(jnp.int32, (q, q), 1)
              <= lax.broadcasted_iota(jnp.int32, (q, q), 0))
    cb = [_nt(cc[:, g * HD:(g + 1) * HD], bb[:, g * HD:(g + 1) * HD]) for g in range(2)]
    ys = []
    for h in range(H):
        g = h // (H // 2)
        hs = slice(h * HD, (h + 1) * HD)
        aq = acs_c[:, _DT_AT + h:_DT_AT + h + 1]
        ak = acs_t[_DT_AT + h:_DT_AT + h + 1, :q]
        lm = jnp.exp(jnp.where(causal, aq - ak, NEG))
        gm = (cb[g] * lm).astype(BF16)
        xh = xs[:, hs]
        xd = xh * dtc[:, _DT_AT + h:_DT_AT + h + 1]
        y = jnp.dot(gm, xd.astype(BF16), preferred_element_type=F32)
        s_prev = st_sc[h]
        cg = cc[:, g * HD:(g + 1) * HD]
        y = y + _nt(cg, s_prev.astype(BF16)) * jnp.exp(aq)
        alast = acs_c[q - 1:q, _DT_AT + h:_DT_AT + h + 1]
        xw = (xd * jnp.exp(alast - aq)).astype(BF16)
        st_sc[h] = jnp.exp(alast) * s_prev + _tn(xw, bb[:, g * HD:(g + 1) * HD])
        ys.append(y + dsk_ref[:, hs] * xh)
    yc = jnp.concatenate(ys, axis=1) * _silu(z_ref[...])
    y_ref[...] = _rms(yc, gn_ref[...]).astype(y_ref.dtype)

    @pl.when(c == pl.num_programs(1) - 1)
    def _():
        fin_ref[0] = st_sc[...]


def _ssd(xbc, z, sm, smt, prefix, init, cw, cb, arow, acol, dsk, gn, nb, nc, q, l_valid):
    rows = lambda w: pl.BlockSpec((q, w), lambda b, c: (b * nc + c, 0))
    const = lambda a: pl.BlockSpec(a.shape, lambda b, c: (0,) * a.ndim)
    if smt.ndim == 2:
        smt_spec = pl.BlockSpec((16, LANE), lambda b, c: (0, b * nc + c))
    else:
        smt_spec = pl.BlockSpec((None, 16, LANE), lambda b, c: (b, 0, 0))
    return pl.pallas_call(
        functools.partial(_ssd_kernel, q=q, l_valid=l_valid), grid=(nb, nc),
        in_specs=[rows(D_CONV), rows(D_SSD), rows(LANE), smt_spec,
                  pl.BlockSpec((1, 8, D_CONV), lambda b, c: (b, 0, 0)),
                  pl.BlockSpec((1, H, HD, HD), lambda b, c: (b, 0, 0, 0)),
                  const(cw), const(cb), const(arow), const(acol), const(dsk), const(gn)],
        out_specs=(rows(D_SSD), pl.BlockSpec((1, H, HD, HD), lambda b, c: (b, 0, 0, 0))),
        out_shape=(jax.ShapeDtypeStruct((nb * nc * q, D_SSD), BF16),
                   jax.ShapeDtypeStruct((nb, H, HD, HD), F32)),
        scratch_shapes=[pltpu.VMEM((8, D_CONV), F32), pltpu.VMEM((H, HD, HD), F32)],
        compiler_params=_cparams(("parallel", "arbitrary")), name="conv_ssd",
    )(xbc, z, sm, smt, prefix, init, cw, cb, arow, acol, dsk, gn)


def _ffn_kernel(x_ref, a_ref, p_ref, s_ref, woa_ref, wop_ref, wos_ref, g1_ref, g2_ref, g3_ref,
                wg_ref, wu_ref, wd_ref, o_ref):
    mp = (jnp.dot(a_ref[...], woa_ref[...], preferred_element_type=F32)
          + jnp.dot(p_ref[...], wop_ref[...], preferred_element_type=F32)
          + jnp.dot(s_ref[...], wos_ref[...], preferred_element_type=F32))
    x1 = x_ref[...] + _rms(mp, g1_ref[...])
    hb = _rms(x1, g2_ref[...]).astype(BF16)
    gate = jnp.dot(hb, wg_ref[...], preferred_element_type=F32)
    up = jnp.dot(hb, wu_ref[...], preferred_element_type=F32)
    act = (_silu(gate) * up).astype(BF16)
    ff = jnp.dot(act, wd_ref[...], preferred_element_type=F32)
    o_ref[...] = x1 + _rms(ff, g3_ref[...])


def _ffn(x, attn, pool, ssd, woa, wop, wos, g1, g2, g3, wg, wu, wd, tm):
    t = x.shape[0]
    row = lambda c: pl.BlockSpec((tm, c), lambda i: (i, 0))
    const = lambda a: pl.BlockSpec(a.shape, lambda i: (0,) * a.ndim, pipeline_mode=pl.Buffered(1))
    return pl.pallas_call(
        _ffn_kernel, grid=(t // tm,),
        in_specs=[row(D_MODEL), row(D_FOX), row(D_POOL), row(D_SSD), const(woa), const(wop), const(wos),
                  const(g1), const(g2), const(g3), const(wg), const(wu), const(wd)],
        out_specs=row(D_MODEL), out_shape=jax.ShapeDtypeStruct((t, D_MODEL), F32),
        compiler_params=_cparams(("parallel",)), name="outproj_ffn",
    )(x, attn, pool, ssd, woa, wop, wos, g1, g2, g3, wg, wu, wd)


def _lane_pack(f_vals, dt_vals):
    out = jnp.zeros((LANE,), F32).at[_DT_AT:_DT_AT + H].set(dt_vals.astype(F32))
    for at in _F_COPIES:
        out = out.at[at:at + H].set(f_vals.astype(F32))
    return out


def _prep_layer(w_in, fox_f_bias, pool_w, pool_scale, conv_w, conv_b, dt_bias, a_log, d_skip, ssd_norm,
                w_out, w_gate, w_up, w_down, ln_pre_mix, ln_post_mix, ln_pre_ffn, ln_post_ffn):
    f0 = 3 * D_FOX
    u0 = f0 + H
    dt0 = u0 + D_POOL + D_SSD + D_CONV
    wm = jnp.concatenate([w_in[:, :f0], w_in[:, u0:dt0]], axis=1).astype(BF16)
    ws = jnp.zeros((D_MODEL, LANE), F32).at[:, _DT_AT:_DT_AT + H].set(w_in[:, dt0:dt0 + H])
    for at in _F_COPIES:
        ws = ws.at[:, at:at + H].set(w_in[:, f0:u0])
    ws = ws.astype(BF16)
    bs = _lane_pack(fox_f_bias, dt_bias)
    wbd = jnp.zeros((D_POOL, D_POOL), F32)
    for g in range(4):
        wbd = wbd.at[g * 64:(g + 1) * 64, g * 64:(g + 1) * 64].set(pool_w[g])
    alog = _lane_pack(jnp.zeros((H,), F32), a_log)
    row = lambda a: a.astype(F32).reshape(1, -1)
    return dict(
        g_pre=row(ln_pre_mix), wm=wm, wvt=wm[:, _V0:_U0].T, ws=ws, wst=ws[:, :16].T,
        bs=bs.reshape(1, LANE), bst=bs[:16].reshape(16, 1),
        wbd=wbd.astype(BF16), pscale=row(pool_scale),
        cw=jnp.zeros((8, D_CONV), F32).at[:CONV_W].set(conv_w), cb=row(conv_b),
        arow=alog.reshape(1, LANE), acol=alog[:16].reshape(16, 1),
        dsk=row(jnp.repeat(d_skip, HD)), gn=row(ssd_norm),
        woa=w_out[:D_FOX].astype(BF16), wop=w_out[D_FOX:D_FOX + D_POOL].astype(BF16),
        wos=w_out[D_FOX + D_POOL:].astype(BF16),
        g1=row(ln_post_mix), g2=row(ln_pre_ffn), g3=row(ln_post_ffn),
        wg=w_gate.astype(BF16), wu=w_up.astype(BF16), wd=w_down.astype(BF16))


def _mixers_and_ffn(x, w, attn_fn, transposed_v, nb, nc, q, l_valid, pos0, pool_prefix, conv_prefix, ssd_init,
                    smt_fn, tm_tok, tm_pool):
    qb, k, kb, v, vb, u, z, xbc, sm, smt = _inproj(x, w["g_pre"], w["wm"], w["wvt"], w["ws"], w["wst"],
                                                   w["bs"], w["bst"], tm_tok, transposed_v)
    attn = attn_fn(qb, kb, vb, sm, smt)
    seq = nc * q
    pool = _pool(u, pool_prefix, w["wbd"], w["pscale"], nb, seq // tm_pool, tm_pool, pos0)
    ssd, fin = _ssd(xbc, z, sm, smt_fn(smt), conv_prefix, ssd_init, w["cw"], w["cb"], w["arow"], w["acol"],
                    w["dsk"], w["gn"], nb, nc, q, l_valid)
    x = _ffn(x, attn, pool, ssd, w["woa"], w["wop"], w["wos"], w["g1"], w["g2"], w["g3"],
             w["wg"], w["wu"], w["wd"], tm_tok)
    return x, (k, v, sm, u, xbc, fin)


def kernel(x_prompt, x_sample, cache_fox_k, cache_fox_v, cache_fox_logf, state_pool, state_conv, state_ssd,
           meta_tokens, ln_pre_mix, ln_post_mix, ln_pre_ffn, ln_post_ffn, w_in, fox_f_bias, pool_w, pool_scale,
           conv_w, conv_b, dt_bias, a_log, d_skip, ssd_norm, w_out, w_gate, w_up, w_down):
    nbp, seq, _ = x_prompt.shape
    nbs, ls, _ = x_sample.shape
    depth, _, past = cache_fox_logf.shape[:3]
    lp = N_META + seq
    tq = 256
    q_ssd = 128
    tp = -(-lp // tq) * tq

    meta = jnp.broadcast_to(meta_tokens.astype(F32)[None], (nbp, N_META, D_MODEL))
    xp = jnp.concatenate([meta, x_prompt, jnp.zeros((nbp, tp - lp, D_MODEL), F32)], axis=1)
    xp = xp.reshape(nbp * tp, D_MODEL)
    xs = x_sample.reshape(nbs * ls, D_MODEL)

    zero_pool = jnp.zeros((nbp, 16, D_POOL), F32)
    zero_conv = jnp.zeros((nbp, 8, D_CONV), F32)
    zero_ssd = jnp.zeros((nbp, H, HD, HD), F32)

    outs_p, outs_s = [], []
    for l in range(depth):
        w = _prep_layer(w_in[l], fox_f_bias[l], pool_w[l], pool_scale[l], conv_w[l], conv_b[l], dt_bias[l],
                        a_log[l], d_skip[l], ssd_norm[l], w_out[l], w_gate[l], w_up[l], w_down[l],
                        ln_pre_mix[l], ln_post_mix[l], ln_pre_ffn[l], ln_post_ffn[l])

        def attn_p(qb, kb, vt, sm, smt):
            return _fox_prompt(qb, kb, _forget_bias(sm, nbp, tp), vt, nbp, tp, tq)

        xp, st = _mixers_and_ffn(xp, w, attn_p, True, nbp, tp // q_ssd, q_ssd, lp, 0, zero_pool, zero_conv,
                                 zero_ssd, lambda smt: smt, tq, q_ssd)
        outs_p.append(st)

        kc = cache_fox_k[l].reshape(nbs, past, D_FOX)
        vc = cache_fox_v[l].reshape(nbs, past, D_FOX)
        lfc = jnp.transpose(cache_fox_logf[l].astype(F32), (0, 2, 1))

        def to_seq_major(smt):
            return jnp.transpose(smt.reshape(16, nbs, ls), (1, 0, 2))

        def attn_s(qb, kb, vb, sm, smt):
            lfn = to_seq_major(smt)[:, :8]
            lf_all = jnp.concatenate([jnp.pad(lfc, ((0, 0), (0, 8 - H), (0, 0))), lfn,
                                      jnp.zeros((nbs, 8, LANE - ls), F32)], axis=2)
            return _fox_sample(qb, kb, vb, kc, vc, lf_all, nbs, ls, past)

        pool_pre = jnp.pad(state_pool[l].astype(F32), ((0, 0), (16 - POOL_BUF, 0), (0, 0)))
        conv_pre = jnp.pad(state_conv[l].astype(F32), ((0, 0), (8 - (CONV_W - 1), 0), (0, 0)))
        xs, st = _mixers_and_ffn(xs, w, attn_s, False, nbs, 1, ls, ls, past, pool_pre, conv_pre,
                                 state_ssd[l].astype(F32),
                                 lambda smt: jnp.pad(to_seq_major(smt), ((0, 0), (0, 0), (0, LANE - ls))),
                                 tq, ls)
        outs_s.append(st)

    def collect(outs, nb, rows, valid):
        seqv = lambda a: a.reshape(nb, rows, a.shape[-1])
        k = jnp.stack([seqv(o[0])[:, :valid].reshape(nb, valid, H, HD) for o in outs])
        v = jnp.stack([seqv(o[1])[:, :valid].reshape(nb, valid, H, HD) for o in outs])
        lf = jnp.stack([seqv(o[2])[:, :valid, _F_AT:_F_AT + H] for o in outs])
        pn = jnp.stack([seqv(o[3])[:, valid - POOL_BUF:valid] for o in outs])
        cn = jnp.stack([seqv(o[4])[:, valid - (CONV_W - 1):valid] for o in outs])
        sn = jnp.stack([o[5] for o in outs])
        return k, v, lf, pn, cn, sn

    y_prompt = xp.reshape(nbp, tp, D_MODEL)[:, N_META:lp]
    y_sample = xs.reshape(nbs, ls, D_MODEL)
    return (y_prompt, y_sample) + collect(outs_p, nbp, tp, lp) + collect(outs_s, nbs, ls, ls)
```

```python
import functools
import math

import jax
import jax.numpy as jnp
from jax import lax
from jax.experimental import pallas as pl
from jax.experimental.pallas import tpu as pltpu

F32 = jnp.float32
BF16 = jnp.bfloat16

D_MODEL = 1024
N_META = 16
EPS = 1e-6
H = 6
HD = 64
D_FOX = H * HD
D_POOL = 256
POOL_BUF = 15
D_SSD = H * HD
D_BC = 128
D_CONV = D_SSD + 2 * D_BC
CONV_W = 4
D_FF = 2816
LANE = 128
SUBLANE = 8
NEG = -1e30
LOG2E = math.log2(math.e)

_Q0, _K0, _V0, _U0, _Z0, _X0, _MAIN = 0, 384, 768, 1152, 1408, 1792, 2432
_F_AT, _DT_AT, _F_COPIES = 0, 8, (0, 16, 32)
_VROWS = HD + SUBLANE

_VMEM_LIMIT = 56 * 1024 * 1024


def _cparams(sem):
    return pltpu.CompilerParams(dimension_semantics=sem, vmem_limit_bytes=_VMEM_LIMIT)


def _layer_spec(a, l, ngrid, single_buffer=False):
    idx = (l,) + (0,) * (a.ndim - 1)
    kw = dict(pipeline_mode=pl.Buffered(1)) if single_buffer else {}
    return pl.BlockSpec((None,) + a.shape[1:], lambda *_: idx, **kw)


def _rms(x, g):
    ms = jnp.mean(x * x, axis=-1, keepdims=True)
    return x * lax.rsqrt(ms + EPS) * g


def _silu(x):
    return x * (1.0 / (1.0 + jnp.exp(-x)))


def _softplus_tail(x):
    return jnp.log1p(jnp.exp(-jnp.abs(x)))


def _nt(a, b):
    return lax.dot_general(a, b, (((1,), (1,)), ((), ())), preferred_element_type=F32)


def _tn(a, b):
    return lax.dot_general(a, b, (((0,), (0,)), ((), ())), preferred_element_type=F32)


def _scan_lanes(x):
    lane = lax.broadcasted_iota(jnp.int32, x.shape, 1)
    s = 1
    while s < x.shape[1]:
        x = x + jnp.where(lane >= s, pltpu.roll(x, s, 1), 0.0)
        s *= 2
    return x


def _scan_rows(x):
    row = lax.broadcasted_iota(jnp.int32, x.shape, 0)
    s = 1
    while s < x.shape[0]:
        x = x + jnp.where(row >= s, pltpu.roll(x, s, 0), 0.0)
        s *= 2
    return x


def _inproj_kernel(x_ref, g_ref, wm_ref, wkt_ref, wvt_ref, ws_ref, wst_ref, bs_ref, bst_ref, *outs,
                   prompt, qscale):
    hb = _rms(x_ref[...], g_ref[...]).astype(BF16)

    def seg(a, b):
        return jnp.dot(hb, wm_ref[:, a:b], preferred_element_type=F32)

    if prompt:
        qb_ref, kb_ref, kt_ref, vt_ref, vtb_ref, u_ref, z_ref, xbc_ref, sm_ref, smt_ref = outs
        kb_ref[...] = seg(_K0, _V0).astype(BF16)
        kt_ref[...] = _nt(wkt_ref[...], hb)
        vt = _nt(wvt_ref[...], hb)
        vt_ref[...] = vt
        tm = vt.shape[1]
        ones_row = jnp.where(lax.broadcasted_iota(jnp.int32, (SUBLANE, tm), 0) == 0, 1.0, 0.0).astype(BF16)
        for h in range(H):
            vtb_ref[0, h * _VROWS:h * _VROWS + HD, :] = vt[h * HD:(h + 1) * HD].astype(BF16)
            vtb_ref[0, h * _VROWS + HD:(h + 1) * _VROWS, :] = ones_row
    else:
        qb_ref, k_ref, kb_ref, v_ref, vb_ref, u_ref, z_ref, xbc_ref, sm_ref, smt_ref = outs
        k = seg(_K0, _V0)
        k_ref[...] = k
        kb_ref[...] = k.astype(BF16)
        v = seg(_V0, _U0)
        v_ref[...] = v
        vb_ref[...] = v.astype(BF16)
    qb_ref[...] = (seg(_Q0, _K0) * qscale).astype(BF16)
    u_ref[...] = seg(_U0, _Z0)
    z_ref[...] = seg(_Z0, _X0)
    xbc_ref[...] = seg(_X0, _MAIN)

    sm = jnp.dot(hb, ws_ref[...], preferred_element_type=F32) + bs_ref[...]
    lane = lax.broadcasted_iota(jnp.int32, sm.shape, 1)
    tail = _softplus_tail(sm)
    is_dt = (lane >= _DT_AT) & (lane < _DT_AT + 8)
    sm_ref[...] = jnp.where(is_dt, jnp.maximum(sm, 0.0) + tail, jnp.minimum(sm, 0.0) - tail)
    smt = _nt(wst_ref[...], hb) + bst_ref[...]
    row = lax.broadcasted_iota(jnp.int32, smt.shape, 0)
    tail = _softplus_tail(smt)
    smt_ref[...] = jnp.where(row < _DT_AT, jnp.minimum(smt, 0.0) - tail, jnp.maximum(smt, 0.0) + tail)


def _inproj(x, w, l, tm, prompt_dims=None):
    t = x.shape[0]
    nt = t // tm
    row = lambda c: pl.BlockSpec((tm, c), lambda i: (i, 0))
    sds = jax.ShapeDtypeStruct
    tail_shapes = (sds((t, D_POOL), F32), sds((t, D_SSD), F32), sds((t, D_CONV), F32),
                   sds((t, LANE), F32), sds((16, t), F32))
    tail_specs = (row(D_POOL), row(D_SSD), row(D_CONV), row(LANE), pl.BlockSpec((16, tm), lambda i: (0, i)))
    if prompt_dims is not None:
        nb, tp, lp = prompt_dims
        per = tp // tm
        tmin = pl.BlockSpec((None, D_FOX, tm), lambda i: (i // per, 0, i % per))
        out_shape = (sds((t, D_FOX), BF16), sds((t, D_FOX), BF16), sds((nb, D_FOX, lp), F32),
                     sds((nb, D_FOX, lp), F32), sds((nt, H * _VROWS, tm), BF16)) + tail_shapes
        out_specs = (row(D_FOX), row(D_FOX), tmin, tmin,
                     pl.BlockSpec((1, H * _VROWS, tm), lambda i: (i, 0, 0))) + tail_specs
        qscale = (HD ** -0.5) * LOG2E
    else:
        out_shape = (sds((t, D_FOX), BF16), sds((t, D_FOX), F32), sds((t, D_FOX), BF16),
                     sds((t, D_FOX), F32), sds((t, D_FOX), BF16)) + tail_shapes
        out_specs = (row(D_FOX),) * 5 + tail_specs
        qscale = HD ** -0.5
    names = ("g_pre", "wm", "wkt", "wvt", "ws", "wst", "bs", "bst")
    return pl.pallas_call(
        functools.partial(_inproj_kernel, prompt=prompt_dims is not None, qscale=qscale), grid=(nt,),
        in_specs=[row(D_MODEL)] + [_layer_spec(w[n], l, 1) for n in names],
        out_specs=out_specs, out_shape=out_shape,
        compiler_params=_cparams(("parallel",)), name="inproj",
    )(x, *[w[n] for n in names])


def _forget_bias_kernel(sm_ref, e_ref, *, nblk, rows):
    lane = lax.broadcasted_iota(jnp.int32, (1, LANE), 1)
    keep = ((lane & 15) < H) & (lane < 48)
    carry = jnp.zeros((1, LANE), F32)
    for j in range(nblk):
        blk = slice(j * rows, (j + 1) * rows)
        c = _scan_rows(sm_ref[blk, :]) + carry
        carry = c[rows - 1:rows, :]
        c2 = c * LOG2E
        hi = c2.astype(BF16).astype(F32)
        mid = (c2 - hi).astype(BF16).astype(F32)
        low = (c2 - hi) - mid
        e = jnp.where(lane < 16, hi, jnp.where(lane < 32, mid, low))
        e_ref[blk, :] = jnp.where(keep, e, 0.0).astype(BF16)


def _forget_bias(sm, nb, tp):
    rows = LANE
    return pl.pallas_call(
        functools.partial(_forget_bias_kernel, nblk=tp // rows, rows=rows), grid=(nb,),
        in_specs=[pl.BlockSpec((tp, LANE), lambda b: (b, 0))],
        out_specs=pl.BlockSpec((tp, LANE), lambda b: (b, 0)),
        out_shape=jax.ShapeDtypeStruct((nb * tp, LANE), BF16),
        compiler_params=_cparams(("parallel",)), name="forget_bias",
    )(sm)


def _fox_prompt_kernel(q_ref, k_ref, e_ref, vt_ref, o_ref, qa_sc, m_sc, acc_sc, sa_sc, sb_sc, *, tq):
    i = pl.program_id(1)
    lane = lax.broadcasted_iota(jnp.int32, (1, LANE), 1)
    causal = (lax.broadcasted_iota(jnp.int32, (tq, tq), 0)
              <= lax.broadcasted_iota(jnp.int32, (tq, tq), 1))
    for h in range(H):
        q2 = q_ref[:, (h // 2) * LANE:(h // 2 + 1) * LANE]
        mine = (lane < HD) if h % 2 == 0 else (lane >= HD)
        qm = jnp.where(mine, q2, jnp.zeros_like(q2))
        pick = (lane == _F_COPIES[0] + h) | (lane == _F_COPIES[1] + h) | (lane == _F_COPIES[2] + h)
        qe = jnp.broadcast_to(jnp.where(pick, -1.0, 0.0), (tq, LANE)).astype(BF16)
        qa_sc[h] = jnp.concatenate([qm, qe], axis=1)
    m_sc[...] = jnp.full(m_sc.shape, NEG, F32)
    acc_sc[...] = jnp.zeros(acc_sc.shape, F32)

    def logits(j, s_sc):
        rows = pl.ds(pl.multiple_of(j * tq, tq), tq)
        kk = k_ref[rows, :]
        ee = e_ref[rows, :]
        for h in range(H):
            ka = jnp.concatenate([kk[:, (h // 2) * LANE:(h // 2 + 1) * LANE], ee], axis=1)
            s_sc[h] = _nt(ka, qa_sc[h])

    def softmax_pv(j, s_sc, masked):
        for h in range(H):
            s = s_sc[h]
            if masked:
                s = jnp.where(causal, s, NEG)
            m_old = m_sc[h:h + 1, :]
            m_new = jnp.maximum(m_old, jnp.max(s, axis=0, keepdims=True))
            m_sc[h:h + 1, :] = m_new
            pm = jnp.exp2(s - m_new).astype(BF16)
            hs = slice(h * _VROWS, (h + 1) * _VROWS)
            pv = jnp.dot(vt_ref[j, hs, :], pm, preferred_element_type=F32)
            acc_sc[hs, :] = jnp.exp2(m_old - m_new) * acc_sc[hs, :] + pv

    def body(jj, carry):
        j = 2 * jj
        logits(j + 1, sb_sc)
        softmax_pv(j, sa_sc, False)
        logits(j + 2, sa_sc)
        softmax_pv(j + 1, sb_sc, False)
        return carry

    logits(0, sa_sc)
    lax.fori_loop(0, i // 2, body, 0)

    @pl.when(i % 2 == 0)
    def _():
        softmax_pv(i, sa_sc, True)

    @pl.when(i % 2 == 1)
    def _():
        logits(i, sb_sc)
        softmax_pv(i - 1, sa_sc, False)
        softmax_pv(i, sb_sc, True)

    out = jnp.concatenate([acc_sc[h * _VROWS:h * _VROWS + HD, :] / acc_sc[h * _VROWS + HD:h * _VROWS + HD + 1, :]
                           for h in range(H)], axis=0)
    o_ref[...] = out.T.astype(o_ref.dtype)


def _fox_prompt(qb, kb, eb, vt, nb, tp, tq):
    nq = tp // tq
    return pl.pallas_call(
        functools.partial(_fox_prompt_kernel, tq=tq), grid=(nb, nq),
        in_specs=[pl.BlockSpec((tq, D_FOX), lambda b, i: (b * nq + i, 0)),
                  pl.BlockSpec((tp, D_FOX), lambda b, i: (b, 0)),
                  pl.BlockSpec((tp, LANE), lambda b, i: (b, 0)),
                  pl.BlockSpec((nq, H * _VROWS, tq), lambda b, i: (b, 0, 0))],
        out_specs=pl.BlockSpec((tq, D_FOX), lambda b, i: (b * nq + i, 0)),
        out_shape=jax.ShapeDtypeStruct((nb * tp, D_FOX), BF16),
        scratch_shapes=[pltpu.VMEM((H, tq, 2 * LANE), BF16), pltpu.VMEM((8, tq), F32),
                        pltpu.VMEM((H * _VROWS, tq), F32),
                        pltpu.VMEM((H, tq, tq), F32), pltpu.VMEM((H, tq, tq), F32)],
        compiler_params=_cparams(("parallel", "arbitrary")), name="fox_prompt",
    )(qb, kb, eb, vt)


def _fox_sample_kernel(q_ref, kn_ref, vn_ref, kct_ref, vct_ref, lf_ref, o_ref, *, past, ls):
    lo = lax.broadcasted_iota(jnp.int32, (1, LANE), 1) < HD
    causal = (lax.broadcasted_iota(jnp.int32, (ls, ls), 1)
              <= lax.broadcasted_iota(jnp.int32, (ls, ls), 0))
    nblk = lf_ref.shape[1] // LANE
    carry = jnp.zeros((8, 1), F32)
    cs = []
    for j in range(nblk):
        c = _scan_lanes(lf_ref[:, j * LANE:(j + 1) * LANE]) + carry
        cs.append(c)
        carry = c[:, LANE - 1:LANE]
    c_all = jnp.concatenate(cs, axis=1)
    for p in range(H // 2):
        cols = slice(p * LANE, (p + 1) * LANE)
        q2 = q_ref[:, cols]
        kct = kct_ref[cols, :].astype(BF16)
        vct = vct_ref[cols, :].astype(BF16)
        kn = kn_ref[:, cols]
        vn = vn_ref[:, cols]
        outs, ls_ = [], []
        for hh in range(2):
            hrow = slice(2 * p + hh, 2 * p + hh + 1)
            qm = jnp.where(lo, q2, jnp.zeros_like(q2)) if hh == 0 else jnp.where(lo, jnp.zeros_like(q2), q2)
            c0 = c_all[hrow, past - 1:past]
            s_c = jnp.dot(qm, kct, preferred_element_type=F32) + (c0 - c_all[hrow, :past])
            s_n = _nt(qm, kn) + (c0 - c_all[hrow, past:past + ls])
            s_n = jnp.where(causal, s_n, NEG)
            m = jnp.maximum(jnp.max(s_c, axis=-1, keepdims=True), jnp.max(s_n, axis=-1, keepdims=True))
            p_c = jnp.exp(s_c - m)
            p_n = jnp.exp(s_n - m)
            ls_.append(jnp.sum(p_c, axis=-1, keepdims=True) + jnp.sum(p_n, axis=-1, keepdims=True))
            outs.append(_nt(p_c.astype(BF16), vct) + jnp.dot(p_n.astype(BF16), vn, preferred_element_type=F32))
        o = jnp.where(lo, outs[0], outs[1]) / jnp.where(lo, ls_[0], ls_[1])
        o_ref[:, cols] = o.astype(o_ref.dtype)


def _fox_sample(qb, kb, vb, kct, vct, lf_all, l, nb, ls, past):
    new = pl.BlockSpec((ls, D_FOX), lambda b: (b, 0))
    cache = pl.BlockSpec((None, None, D_FOX, past), lambda b: (l, b, 0, 0))
    return pl.pallas_call(
        functools.partial(_fox_sample_kernel, past=past, ls=ls), grid=(nb,),
        in_specs=[new, new, new, cache, cache, pl.BlockSpec((None, 8, past + LANE), lambda b: (b, 0, 0))],
        out_specs=new, out_shape=jax.ShapeDtypeStruct((nb * ls, D_FOX), BF16),
        compiler_params=_cparams(("parallel",)), name="fox_sample",
    )(qb, kb, vb, kct, vct, lf_all)


def _pool_kernel(u_ref, pre_ref, w_ref, sc_ref, o_ref, halo_sc, *, tm, pos0):
    t = pl.program_id(1)

    @pl.when(t == 0)
    def _():
        halo_sc[...] = pre_ref[0]

    u = u_ref[...]
    a = jnp.concatenate([halo_sc[...], u], axis=0)
    e1 = a + pltpu.roll(a, 1, 0)
    e2 = e1 + pltpu.roll(e1, 2, 0)
    e3 = e2 + pltpu.roll(e2, 4, 0)
    e4 = e3 + pltpu.roll(e3, 8, 0)
    lane = lax.broadcasted_iota(jnp.int32, (1, D_POOL), 1)
    win = jnp.where(lane < 64, e1, jnp.where(lane < 128, e2, jnp.where(lane < 192, e3, e4)))[16:]
    wsz = jnp.where(lane < 64, 2.0, jnp.where(lane < 128, 4.0, jnp.where(lane < 192, 8.0, 16.0)))
    pos = (pos0 + t * tm + lax.broadcasted_iota(jnp.int32, (tm, 1), 0)).astype(F32)
    diff = win / jnp.minimum(pos + 1.0, wsz) - u
    y = jnp.dot(diff.astype(BF16), w_ref[...], preferred_element_type=F32) * sc_ref[...]
    o_ref[...] = y.astype(o_ref.dtype)
    halo_sc[...] = u[tm - 16:]


def _pool(u, prefix, w, l, nb, nt, tm, pos0):
    return pl.pallas_call(
        functools.partial(_pool_kernel, tm=tm, pos0=pos0), grid=(nb, nt),
        in_specs=[pl.BlockSpec((tm, D_POOL), lambda b, t: (b * nt + t, 0)),
                  pl.BlockSpec((1, 16, D_POOL), lambda b, t: (b, 0, 0)),
                  _layer_spec(w["wbd"], l, 2), _layer_spec(w["pscale"], l, 2)],
        out_specs=pl.BlockSpec((tm, D_POOL), lambda b, t: (b * nt + t, 0)),
        out_shape=jax.ShapeDtypeStruct((nb * nt * tm, D_POOL), BF16),
        scratch_shapes=[pltpu.VMEM((16, D_POOL), F32)],
        compiler_params=_cparams(("parallel", "arbitrary")), name="pool_mixer",
    )(u, prefix, w["wbd"], w["pscale"])


def _ssd_kernel(xbc_ref, z_ref, sm_ref, smt_ref, pre_ref, init_ref, cw_ref, cb_ref, arow_ref,
                acol_ref, dsk_ref, gn_ref, y_ref, fin_ref, halo_sc, st_sc, *, q, l_valid):
    c = pl.program_id(1)

    @pl.when(c == 0)
    def _():
        halo_sc[...] = pre_ref[0]
        st_sc[...] = init_ref[0]

    x = xbc_ref[...]
    a = jnp.concatenate([halo_sc[...], x], axis=0)
    w = cw_ref[...]
    conv = (cb_ref[...] + w[3:4] * a + w[2:3] * pltpu.roll(a, 1, 0)
            + w[1:2] * pltpu.roll(a, 2, 0) + w[0:1] * pltpu.roll(a, 3, 0))
    halo_sc[...] = x[q - 8:]
    act = _silu(conv[8:])
    xs = act[:, :D_SSD]
    bb = act[:, D_SSD:D_SSD + D_BC].astype(BF16)
    cc = act[:, D_SSD + D_BC:].astype(BF16)

    rvalid = (c * q + lax.broadcasted_iota(jnp.int32, (q, 1), 0)) < l_valid
    dtc = jnp.where(rvalid, sm_ref[...], 0.0)
    acs_c = _scan_rows(dtc * (-jnp.exp(arow_ref[...])))
    cvalid = (c * q + lax.broadcasted_iota(jnp.int32, (1, LANE), 1)) < l_valid
    dtt = jnp.where(cvalid, smt_ref[...], 0.0)
    acs_t = _scan_lanes(dtt * (-jnp.exp(acol_ref[...])))

    causal = (lax.broadcasted_iota(jnp.int32, (q, q), 1)
              <= lax.broadcasted_iota(jnp.int32, (q, q), 0))
    cb = [_nt(cc[:, g * HD:(g + 1) * HD], bb[:, g * HD:(g + 1) * HD]) for g in range(2)]
    ys = []
    for h in range(H):
        g = h // (H // 2)
        hs = slice(h * HD, (h + 1) * HD)
        aq = acs_c[:, _DT_AT + h:_DT_AT + h + 1]
        ak = acs_t[_DT_AT + h:_DT_AT + h + 1, :q]
        lm = jnp.exp(jnp.where(causal, aq - ak, NEG))
        gm = (cb[g] * lm).astype(BF16)
        xh = xs[:, hs]
        xd = xh * dtc[:, _DT_AT + h:_DT_AT + h + 1]
        y = jnp.dot(gm, xd.astype(BF16), preferred_element_type=F32)
        s_prev = st_sc[h]
        cg = cc[:, g * HD:(g + 1) * HD]
        y = y + _nt(cg, s_prev.astype(BF16)) * jnp.exp(aq)
        alast = acs_c[q - 1:q, _DT_AT + h:_DT_AT + h + 1]
        xw = (xd * jnp.exp(alast - aq)).astype(BF16)
        st_sc[h] = jnp.exp(alast) * s_prev + _tn(xw, bb[:, g * HD:(g + 1) * HD])
        ys.append(y + dsk_ref[:, hs] * xh)
    yc = jnp.concatenate(ys, axis=1) * _silu(z_ref[...])
    y_ref[...] = _rms(yc, gn_ref[...]).astype(y_ref.dtype)

    @pl.when(c == pl.num_programs(1) - 1)
    def _():
        fin_ref[0] = st_sc[...]


def _ssd(xbc, z, sm, smt, prefix, init, w, l, nb, nc, q, l_valid):
    rows = lambda c: pl.BlockSpec((q, c), lambda b, i: (b * nc + i, 0))
    if smt.ndim == 2:
        smt_spec = pl.BlockSpec((16, LANE), lambda b, i: (0, b * nc + i))
    else:
        smt_spec = pl.BlockSpec((None, 16, LANE), lambda b, i: (b, 0, 0))
    names = ("cw", "cb", "arow", "acol", "dsk", "gn")
    return pl.pallas_call(
        functools.partial(_ssd_kernel, q=q, l_valid=l_valid), grid=(nb, nc),
        in_specs=[rows(D_CONV), rows(D_SSD), rows(LANE), smt_spec,
                  pl.BlockSpec((1, 8, D_CONV), lambda b, i: (b, 0, 0)),
                  pl.BlockSpec((1, H, HD, HD), lambda b, i: (b, 0, 0, 0))]
                 + [_layer_spec(w[n], l, 2) for n in names],
        out_specs=(rows(D_SSD), pl.BlockSpec((1, H, HD, HD), lambda b, i: (b, 0, 0, 0))),
        out_shape=(jax.ShapeDtypeStruct((nb * nc * q, D_SSD), BF16),
                   jax.ShapeDtypeStruct((nb, H, HD, HD), F32)),
        scratch_shapes=[pltpu.VMEM((8, D_CONV), F32), pltpu.VMEM((H, HD, HD), F32)],
        compiler_params=_cparams(("parallel", "arbitrary")), name="conv_ssd",
    )(xbc, z, sm, smt, prefix, init, *[w[n] for n in names])


def _ffn_kernel(x_ref, a_ref, p_ref, s_ref, wo_ref, g1_ref, g2_ref, g3_ref, wg_ref, wu_ref, wd_ref, o_ref):
    mp = (jnp.dot(a_ref[...], wo_ref[:D_FOX, :], preferred_element_type=F32)
          + jnp.dot(p_ref[...], wo_ref[D_FOX:D_FOX + D_POOL, :], preferred_element_type=F32)
          + jnp.dot(s_ref[...], wo_ref[D_FOX + D_POOL:, :], preferred_element_type=F32))
    x1 = x_ref[...] + _rms(mp, g1_ref[...])
    hb = _rms(x1, g2_ref[...]).astype(BF16)
    gate = jnp.dot(hb, wg_ref[...], preferred_element_type=F32)
    up = jnp.dot(hb, wu_ref[...], preferred_element_type=F32)
    act = (_silu(gate) * up).astype(BF16)
    ff = jnp.dot(act, wd_ref[...], preferred_element_type=F32)
    o_ref[...] = x1 + _rms(ff, g3_ref[...])


def _ffn(x, attn, pool, ssd, w, l, tm):
    t = x.shape[0]
    row = lambda c: pl.BlockSpec((tm, c), lambda i: (i, 0))
    names = ("wo", "g1", "g2", "g3", "wg", "wu", "wd")
    return pl.pallas_call(
        _ffn_kernel, grid=(t // tm,),
        in_specs=[row(D_MODEL), row(D_FOX), row(D_POOL), row(D_SSD)]
                 + [_layer_spec(w[n], l, 1, single_buffer=True) for n in names],
        out_specs=row(D_MODEL), out_shape=jax.ShapeDtypeStruct((t, D_MODEL), F32),
        compiler_params=_cparams(("parallel",)), name="outproj_ffn",
    )(x, attn, pool, ssd, *[w[n] for n in names])


def _lane_pack(f_vals, dt_vals):
    out = jnp.zeros((f_vals.shape[0], LANE), F32).at[:, _DT_AT:_DT_AT + H].set(dt_vals.astype(F32))
    for at in _F_COPIES:
        out = out.at[:, at:at + H].set(f_vals.astype(F32))
    return out


def _prep_weights(w_in, fox_f_bias, pool_w, pool_scale, conv_w, conv_b, dt_bias, a_log, d_skip, ssd_norm,
                  w_out, w_gate, w_up, w_down, ln_pre_mix, ln_post_mix, ln_pre_ffn, ln_post_ffn):
    depth = w_in.shape[0]
    f0 = 3 * D_FOX
    u0 = f0 + H
    dt0 = u0 + D_POOL + D_SSD + D_CONV
    wm = jnp.concatenate([w_in[:, :, :f0], w_in[:, :, u0:dt0]], axis=2).astype(BF16)
    ws = jnp.zeros((depth, D_MODEL, LANE), F32).at[:, :, _DT_AT:_DT_AT + H].set(w_in[:, :, dt0:dt0 + H])
    for at in _F_COPIES:
        ws = ws.at[:, :, at:at + H].set(w_in[:, :, f0:u0])
    ws = ws.astype(BF16)
    bs = _lane_pack(fox_f_bias, dt_bias)
    wbd = jnp.zeros((depth, D_POOL, D_POOL), F32)
    for g in range(4):
        wbd = wbd.at[:, g * 64:(g + 1) * 64, g * 64:(g + 1) * 64].set(pool_w[:, g])
    alog = _lane_pack(jnp.zeros_like(a_log), a_log)
    row = lambda a: a.astype(F32).reshape(depth, 1, -1)
    tr = lambda a: jnp.transpose(a, (0, 2, 1))
    return dict(
        g_pre=row(ln_pre_mix), wm=wm, wkt=tr(w_in[:, :, _K0:_V0]).astype(BF16),
        wvt=tr(w_in[:, :, _V0:_U0]).astype(BF16), ws=ws, wst=tr(ws[:, :, :16]),
        bs=bs.reshape(depth, 1, LANE), bst=bs[:, :16].reshape(depth, 16, 1),
        wbd=wbd.astype(BF16), pscale=row(pool_scale),
        cw=jnp.zeros((depth, 8, D_CONV), F32).at[:, :CONV_W].set(conv_w), cb=row(conv_b),
        arow=alog.reshape(depth, 1, LANE), acol=alog[:, :16].reshape(depth, 16, 1),
        dsk=row(jnp.repeat(d_skip, HD, axis=1)), gn=row(ssd_norm),
        wo=w_out.astype(BF16), g1=row(ln_post_mix), g2=row(ln_pre_ffn), g3=row(ln_post_ffn),
        wg=w_gate.astype(BF16), wu=w_up.astype(BF16), wd=w_down.astype(BF16))


def _mixers_and_ffn(x, w, l, attn_fn, prompt_dims, nb, nc, q, l_valid, pos0, pool_prefix, conv_prefix,
                    ssd_init, smt_fn, tm_tok, tm_pool):
    proj = _inproj(x, w, l, tm_tok, prompt_dims)
    u, z, xbc, sm, smt = proj[5:]
    attn = attn_fn(proj[:5], sm, smt)
    seq = nc * q
    pool = _pool(u, pool_prefix, w, l, nb, seq // tm_pool, tm_pool, pos0)
    ssd, fin = _ssd(xbc, z, sm, smt_fn(smt), conv_prefix, ssd_init, w, l, nb, nc, q, l_valid)
    x = _ffn(x, attn, pool, ssd, w, l, tm_tok)
    return x, proj, fin


def kernel(x_prompt, x_sample, cache_fox_k, cache_fox_v, cache_fox_logf, state_pool, state_conv, state_ssd,
           meta_tokens, ln_pre_mix, ln_post_mix, ln_pre_ffn, ln_post_ffn, w_in, fox_f_bias, pool_w, pool_scale,
           conv_w, conv_b, dt_bias, a_log, d_skip, ssd_norm, w_out, w_gate, w_up, w_down):
    nbp, seq, _ = x_prompt.shape
    nbs, ls, _ = x_sample.shape
    depth, _, past = cache_fox_logf.shape[:3]
    lp = N_META + seq
    tq = 256
    q_ssd = 128
    tp = -(-lp // tq) * tq

    w = _prep_weights(w_in, fox_f_bias, pool_w, pool_scale, conv_w, conv_b, dt_bias, a_log, d_skip, ssd_norm,
                      w_out, w_gate, w_up, w_down, ln_pre_mix, ln_post_mix, ln_pre_ffn, ln_post_ffn)

    meta = jnp.broadcast_to(meta_tokens.astype(F32)[None], (nbp, N_META, D_MODEL))
    xp = jnp.concatenate([meta, x_prompt, jnp.zeros((nbp, tp - lp, D_MODEL), F32)], axis=1)
    xp = xp.reshape(nbp * tp, D_MODEL)
    xs = x_sample.reshape(nbs * ls, D_MODEL)

    zero_pool = jnp.zeros((nbp, 16, D_POOL), F32)
    zero_conv = jnp.zeros((nbp, 8, D_CONV), F32)
    zero_ssd = jnp.zeros((nbp, H, HD, HD), F32)

    kct = jnp.transpose(cache_fox_k, (0, 1, 3, 4, 2)).reshape(depth, nbs, D_FOX, past)
    vct = jnp.transpose(cache_fox_v, (0, 1, 3, 4, 2)).reshape(depth, nbs, D_FOX, past)
    lfc = jnp.pad(jnp.transpose(cache_fox_logf.astype(F32), (0, 1, 3, 2)),
                  ((0, 0), (0, 0), (0, 8 - H), (0, 0)))
    pool_pre = jnp.pad(state_pool.astype(F32), ((0, 0), (0, 0), (16 - POOL_BUF, 0), (0, 0)))
    conv_pre = jnp.pad(state_conv.astype(F32), ((0, 0), (0, 0), (8 - (CONV_W - 1), 0), (0, 0)))

    def to_seq_major(smt):
        return jnp.transpose(smt.reshape(16, nbs, ls), (1, 0, 2))

    outs_p, outs_s = [], []
    for l in range(depth):
        def attn_p(qkv, sm, smt):
            qb, kb, _, _, vtb = qkv
            return _fox_prompt(qb, kb, _forget_bias(sm, nbp, tp), vtb, nbp, tp, tq)

        xp, proj, fin = _mixers_and_ffn(xp, w, l, attn_p, (nbp, tp, lp), nbp, tp // q_ssd, q_ssd, lp, 0,
                                        zero_pool, zero_conv, zero_ssd, lambda smt: smt, tq, q_ssd)
        outs_p.append((proj[2], proj[3], proj[9], proj[5], proj[7], fin))

        def attn_s(qkv, sm, smt):
            qb, _, kb, _, vb = qkv
            lf_all = jnp.concatenate([lfc[l], to_seq_major(smt)[:, :8], jnp.zeros((nbs, 8, LANE - ls), F32)],
                                     axis=2)
            return _fox_sample(qb, kb, vb, kct, vct, lf_all, l, nbs, ls, past)

        xs, proj, fin = _mixers_and_ffn(xs, w, l, attn_s, None, nbs, 1, ls, ls, past, pool_pre[l], conv_pre[l],
                                        state_ssd[l].astype(F32),
                                        lambda smt: jnp.pad(to_seq_major(smt), ((0, 0), (0, 0), (0, LANE - ls))),
                                        tq, ls)
        outs_s.append((proj[1], proj[3], proj[9], proj[5], proj[7], fin))

    def tails(outs, nb, rows, valid):
        seqv = lambda a: a.reshape(nb, rows, a.shape[-1])
        lf = jnp.stack([jnp.transpose(o[2][:H].reshape(H, nb, rows)[:, :, :valid], (1, 2, 0)) for o in outs])
        pn = jnp.stack([seqv(o[3])[:, valid - POOL_BUF:valid] for o in outs])
        cn = jnp.stack([seqv(o[4])[:, valid - (CONV_W - 1):valid] for o in outs])
        sn = jnp.stack([o[5] for o in outs])
        return lf, pn, cn, sn

    def token_minor(outs, i):
        a = jnp.stack([o[i] for o in outs]).reshape(depth, nbp, H, HD, lp)
        return jnp.transpose(a, (0, 1, 4, 2, 3))

    def token_major(outs, i):
        return jnp.stack([o[i].reshape(nbs, ls, H, HD) for o in outs])

    y_prompt = xp.reshape(nbp, tp, D_MODEL)[:, N_META:lp]
    y_sample = xs.reshape(nbs, ls, D_MODEL)
    return ((y_prompt, y_sample, token_minor(outs_p, 0), token_minor(outs_p, 1)) + tails(outs_p, nbp, tp, lp)
            + (token_major(outs_s, 0), token_major(outs_s, 1)) + tails(outs_s, nbs, ls, ls))
```

```python
import functools
import math

import jax
import jax.numpy as jnp
from jax import lax
from jax.experimental import pallas as pl
from jax.experimental.pallas import tpu as pltpu

F32 = jnp.float32
BF16 = jnp.bfloat16

D_MODEL = 1024
N_META = 16
EPS = 1e-6
H = 6
HD = 64
D_FOX = H * HD
D_POOL = 256
POOL_BUF = 15
D_SSD = H * HD
D_BC = 128
D_CONV = D_SSD + 2 * D_BC
CONV_W = 4
D_FF = 2816
LANE = 128
SUBLANE = 8
NEG = -1e30
LOG2E = math.log2(math.e)

_Q0, _S0, _U0, _Z0, _X0, _MAIN = 0, 384, 512, 768, 1152, 1792
_MXU_N = 256
_GROUPS = ((_Q0, _U0), (_U0, _Z0), (_Z0, _X0 + LANE), (_X0 + LANE, _MAIN))
_F_AT, _DT_AT, _F_COPIES = 0, 8, (0, 16, 32)
_VROWS = HD + SUBLANE

_VMEM_LIMIT = 56 * 1024 * 1024


def _cparams(sem):
    return pltpu.CompilerParams(dimension_semantics=sem, vmem_limit_bytes=_VMEM_LIMIT)


def _layer_spec(a, l, ngrid, single_buffer=False):
    idx = (l,) + (0,) * (a.ndim - 1)
    kw = dict(pipeline_mode=pl.Buffered(1)) if single_buffer else {}
    return pl.BlockSpec((None,) + a.shape[1:], lambda *_: idx, **kw)


def _rms(x, g):
    ms = jnp.mean(x * x, axis=-1, keepdims=True)
    return x * lax.rsqrt(ms + EPS) * g


def _silu(x):
    return x * jax.nn.sigmoid(x)


def _softplus_tail(x):
    return jnp.log1p(jnp.exp(-jnp.abs(x)))


def _nt(a, b):
    return lax.dot_general(a, b, (((1,), (1,)), ((), ())), preferred_element_type=F32)


def _tn(a, b):
    return lax.dot_general(a, b, (((0,), (0,)), ((), ())), preferred_element_type=F32)


def _scan_lanes(x):
    lane = lax.broadcasted_iota(jnp.int32, x.shape, 1)
    s = 1
    while s < x.shape[1]:
        x = x + jnp.where(lane >= s, pltpu.roll(x, s, 1), 0.0)
        s *= 2
    return x


def _scan_rows(x):
    row = lax.broadcasted_iota(jnp.int32, x.shape, 0)
    s = 1
    while s < x.shape[0]:
        x = x + jnp.where(row >= s, pltpu.roll(x, s, 0), 0.0)
        s *= 2
    return x


def _inproj_kernel(x_ref, g_ref, wm_ref, wst_ref, bs_ref, bst_ref, *rest, prompt, qscale):
    hb = _rms(x_ref[...], g_ref[...]).astype(BF16)

    if prompt:
        wkt_ref, wvt_ref, qb_ref, kb_ref, kt_ref, vt_ref, vtb_ref, u_ref, z_ref, xbc_ref, sm_ref, smt_ref = rest
        kt = _nt(wkt_ref[...], hb)
        kt_ref[...] = kt
        kb_ref[...] = kt.T.astype(BF16)
        vt = _nt(wvt_ref[...], hb)
        vt_ref[...] = vt
        tm = vt.shape[1]
        ones_row = jnp.where(lax.broadcasted_iota(jnp.int32, (SUBLANE, tm), 0) == 0, 1.0, 0.0).astype(BF16)
        for h in range(H):
            vtb_ref[0, h * _VROWS:h * _VROWS + HD, :] = vt[h * HD:(h + 1) * HD].astype(BF16)
            vtb_ref[0, h * _VROWS + HD:(h + 1) * _VROWS, :] = ones_row
    else:
        wkv_ref, qb_ref, k_ref, kb_ref, v_ref, vb_ref, u_ref, z_ref, xbc_ref, sm_ref, smt_ref = rest
        kv = jnp.dot(hb, wkv_ref[...], preferred_element_type=F32)
        k_ref[...] = kv[:, :D_FOX]
        kb_ref[...] = kv[:, :D_FOX].astype(BF16)
        v_ref[...] = kv[:, D_FOX:]
        vb_ref[...] = kv[:, D_FOX:].astype(BF16)
    g0, g1, g2, g3 = (jnp.dot(hb, wm_ref[:, a:b], preferred_element_type=F32) for a, b in _GROUPS)
    qb_ref[...] = (g0[:, :_S0] * qscale).astype(BF16)
    u_ref[...] = g1
    z_ref[...] = g2[:, :_X0 - _Z0]
    xbc_ref[:, :LANE] = g2[:, _X0 - _Z0:]
    xbc_ref[:, LANE:] = g3

    sm = g0[:, _S0:] + bs_ref[...]
    lane = lax.broadcasted_iota(jnp.int32, sm.shape, 1)
    tail = _softplus_tail(sm)
    is_dt = (lane >= _DT_AT) & (lane < _DT_AT + 8)
    sm_ref[...] = jnp.where(is_dt, jnp.maximum(sm, 0.0) + tail, jnp.minimum(sm, 0.0) - tail)
    smt = _nt(wst_ref[...], hb) + bst_ref[...]
    row = lax.broadcasted_iota(jnp.int32, smt.shape, 0)
    tail = _softplus_tail(smt)
    smt_ref[...] = jnp.where(row < _DT_AT, jnp.minimum(smt, 0.0) - tail, jnp.maximum(smt, 0.0) + tail)


def _inproj(x, w, l, tm, prompt_dims=None):
    t = x.shape[0]
    nt = t // tm
    row = lambda c: pl.BlockSpec((tm, c), lambda i: (i, 0))
    sds = jax.ShapeDtypeStruct
    tail_shapes = (sds((t, D_POOL), F32), sds((t, D_SSD), F32), sds((t, D_CONV), F32),
                   sds((t, LANE), F32), sds((16, t), F32))
    tail_specs = (row(D_POOL), row(D_SSD), row(D_CONV), row(LANE), pl.BlockSpec((16, tm), lambda i: (0, i)))
    if prompt_dims is not None:
        nb, tp, lp = prompt_dims
        per = tp // tm
        tmin = pl.BlockSpec((None, D_FOX, tm), lambda i: (i // per, 0, i % per))
        out_shape = (sds((t, D_FOX), BF16), sds((t, D_FOX), BF16), sds((nb, D_FOX, lp), F32),
                     sds((nb, D_FOX, lp), F32), sds((nt, H * _VROWS, tm), BF16)) + tail_shapes
        out_specs = (row(D_FOX), row(D_FOX), tmin, tmin,
                     pl.BlockSpec((1, H * _VROWS, tm), lambda i: (i, 0, 0))) + tail_specs
        qscale = (HD ** -0.5) * LOG2E
    else:
        out_shape = (sds((t, D_FOX), BF16), sds((t, D_FOX), F32), sds((t, D_FOX), BF16),
                     sds((t, D_FOX), F32), sds((t, D_FOX), BF16)) + tail_shapes
        out_specs = (row(D_FOX),) * 5 + tail_specs
        qscale = HD ** -0.5
    names = ("g_pre", "wm", "wst", "bs", "bst") + (("wkt", "wvt") if prompt_dims is not None else ("wkv",))
    return pl.pallas_call(
        functools.partial(_inproj_kernel, prompt=prompt_dims is not None, qscale=qscale), grid=(nt,),
        in_specs=[row(D_MODEL)] + [_layer_spec(w[n], l, 1) for n in names],
        out_specs=out_specs, out_shape=out_shape,
        compiler_params=_cparams(("parallel",)), name="inproj",
    )(x, *[w[n] for n in names])


def _forget_bias_kernel(sm_ref, e_ref, *, nblk, rows):
    lane = lax.broadcasted_iota(jnp.int32, (1, LANE), 1)
    keep = ((lane & 15) < H) & (lane < 48)
    carry = jnp.zeros((1, LANE), F32)
    for j in range(nblk):
        blk = slice(j * rows, (j + 1) * rows)
        c = _scan_rows(sm_ref[blk, :]) + carry
        carry = c[rows - 1:rows, :]
        c2 = c * LOG2E
        hi = c2.astype(BF16).astype(F32)
        mid = (c2 - hi).astype(BF16).astype(F32)
        low = (c2 - hi) - mid
        e = jnp.where(lane < 16, hi, jnp.where(lane < 32, mid, low))
        e_ref[blk, :] = jnp.where(keep, e, 0.0).astype(BF16)


def _forget_bias(sm, nb, tp):
    rows = LANE
    return pl.pallas_call(
        functools.partial(_forget_bias_kernel, nblk=tp // rows, rows=rows), grid=(nb,),
        in_specs=[pl.BlockSpec((tp, LANE), lambda b: (b, 0))],
        out_specs=pl.BlockSpec((tp, LANE), lambda b: (b, 0)),
        out_shape=jax.ShapeDtypeStruct((nb * tp, LANE), BF16),
        compiler_params=_cparams(("parallel",)), name="forget_bias",
    )(sm)


def _fox_prompt_kernel(q_ref, k_ref, e_ref, vt_ref, o_ref, qa_sc, m_sc, acc_sc, sa_sc, sb_sc, *, tq):
    i = pl.program_id(1)
    lane = lax.broadcasted_iota(jnp.int32, (1, LANE), 1)
    causal = (lax.broadcasted_iota(jnp.int32, (tq, tq), 0)
              <= lax.broadcasted_iota(jnp.int32, (tq, tq), 1))
    for h in range(H):
        q2 = q_ref[:, (h // 2) * LANE:(h // 2 + 1) * LANE]
        mine = (lane < HD) if h % 2 == 0 else (lane >= HD)
        qm = jnp.where(mine, q2, jnp.zeros_like(q2))
        pick = (lane == _F_COPIES[0] + h) | (lane == _F_COPIES[1] + h) | (lane == _F_COPIES[2] + h)
        qe = jnp.broadcast_to(jnp.where(pick, -1.0, 0.0), (tq, LANE)).astype(BF16)
        qa_sc[h] = jnp.concatenate([qm, qe], axis=1)
    m_sc[...] = jnp.full(m_sc.shape, NEG, F32)
    acc_sc[...] = jnp.zeros(acc_sc.shape, F32)

    def logits(j, s_sc):
        rows = pl.ds(pl.multiple_of(j * tq, tq), tq)
        kk = k_ref[rows, :]
        ee = e_ref[rows, :]
        for h in range(H):
            ka = jnp.concatenate([kk[:, (h // 2) * LANE:(h // 2 + 1) * LANE], ee], axis=1)
            s_sc[h] = _nt(ka, qa_sc[h])

    def softmax_pv(j, s_sc, masked):
        for h in range(H):
            s = s_sc[h]
            if masked:
                s = jnp.where(causal, s, NEG)
            m_old = m_sc[h:h + 1, :]
            m_new = jnp.maximum(m_old, jnp.max(s, axis=0, keepdims=True))
            m_sc[h:h + 1, :] = m_new
            pm = jnp.exp2(s - m_new).astype(BF16)
            hs = slice(h * _VROWS, (h + 1) * _VROWS)
            pv = jnp.dot(vt_ref[j, hs, :], pm, preferred_element_type=F32)
            acc_sc[hs, :] = jnp.exp2(m_old - m_new) * acc_sc[hs, :] + pv

    def body(jj, carry):
        j = 2 * jj
        logits(j + 1, sb_sc)
        softmax_pv(j, sa_sc, False)
        logits(j + 2, sa_sc)
        softmax_pv(j + 1, sb_sc, False)
        return carry

    logits(0, sa_sc)
    lax.fori_loop(0, i // 2, body, 0)

    @pl.when(i % 2 == 0)
    def _():
        softmax_pv(i, sa_sc, True)

    @pl.when(i % 2 == 1)
    def _():
        logits(i, sb_sc)
        softmax_pv(i - 1, sa_sc, False)
        softmax_pv(i, sb_sc, True)

    out = jnp.concatenate([acc_sc[h * _VROWS:h * _VROWS + HD, :] / acc_sc[h * _VROWS + HD:h * _VROWS + HD + 1, :]
                           for h in range(H)], axis=0)
    o_ref[...] = out.T.astype(o_ref.dtype)


def _fox_prompt(qb, kb, eb, vt, nb, tp, tq):
    nq = tp // tq
    return pl.pallas_call(
        functools.partial(_fox_prompt_kernel, tq=tq), grid=(nb, nq),
        in_specs=[pl.BlockSpec((tq, D_FOX), lambda b, i: (b * nq + i, 0)),
                  pl.BlockSpec((tp, D_FOX), lambda b, i: (b, 0)),
                  pl.BlockSpec((tp, LANE), lambda b, i: (b, 0)),
                  pl.BlockSpec((nq, H * _VROWS, tq), lambda b, i: (b, 0, 0))],
        out_specs=pl.BlockSpec((tq, D_FOX), lambda b, i: (b * nq + i, 0)),
        out_shape=jax.ShapeDtypeStruct((nb * tp, D_FOX), BF16),
        scratch_shapes=[pltpu.VMEM((H, tq, 2 * LANE), BF16), pltpu.VMEM((8, tq), F32),
                        pltpu.VMEM((H * _VROWS, tq), F32),
                        pltpu.VMEM((H, tq, tq), F32), pltpu.VMEM((H, tq, tq), F32)],
        compiler_params=_cparams(("parallel", "arbitrary")), name="fox_prompt",
    )(qb, kb, eb, vt)


def _fox_sample_kernel(q_ref, kn_ref, vn_ref, kct_ref, vct_ref, lf_ref, o_ref, *, past, ls, nseq):
    for s in range(nseq):
        rows = slice(s * ls, (s + 1) * ls)
        _fox_sample_one(q_ref.at[rows], kn_ref.at[rows], vn_ref.at[rows], kct_ref.at[s], vct_ref.at[s],
                        lf_ref.at[s], o_ref.at[rows], past, ls)


def _fox_sample_one(q_ref, kn_ref, vn_ref, kct_ref, vct_ref, lf_ref, o_ref, past, ls):
    lo = lax.broadcasted_iota(jnp.int32, (1, LANE), 1) < HD
    causal = (lax.broadcasted_iota(jnp.int32, (ls, ls), 1)
              <= lax.broadcasted_iota(jnp.int32, (ls, ls), 0))
    nblk = lf_ref.shape[1] // LANE
    loc = _scan_lanes(jnp.concatenate([lf_ref[:, j * LANE:(j + 1) * LANE] for j in range(nblk)], axis=0))
    carry = jnp.zeros((8, 1), F32)
    cs = []
    for j in range(nblk):
        blk = loc[j * 8:(j + 1) * 8]
        cs.append(blk + carry)
        carry = carry + blk[:, LANE - 1:LANE]
    c_all = jnp.concatenate(cs, axis=1)
    for p in range(H // 2):
        cols = slice(p * LANE, (p + 1) * LANE)
        q2 = q_ref[:, cols]
        kct = kct_ref[cols, :].astype(BF16)
        vct = vct_ref[cols, :].astype(BF16)
        kn = kn_ref[:, cols]
        vn = vn_ref[:, cols]
        outs, ls_ = [], []
        for hh in range(2):
            hrow = slice(2 * p + hh, 2 * p + hh + 1)
            qm = jnp.where(lo, q2, jnp.zeros_like(q2)) if hh == 0 else jnp.where(lo, jnp.zeros_like(q2), q2)
            c0 = c_all[hrow, past - 1:past]
            s_c = jnp.dot(qm, kct, preferred_element_type=F32) + (c0 - c_all[hrow, :past])
            s_n = _nt(qm, kn) + (c0 - c_all[hrow, past:past + ls])
            s_n = jnp.where(causal, s_n, NEG)
            m = jnp.maximum(jnp.max(s_c, axis=-1, keepdims=True), jnp.max(s_n, axis=-1, keepdims=True))
            p_c = jnp.exp(s_c - m)
            p_n = jnp.exp(s_n - m)
            ls_.append(jnp.sum(p_c, axis=-1, keepdims=True) + jnp.sum(p_n, axis=-1, keepdims=True))
            outs.append(_nt(p_c.astype(BF16), vct) + jnp.dot(p_n.astype(BF16), vn, preferred_element_type=F32))
        o = jnp.where(lo, outs[0], outs[1]) / jnp.where(lo, ls_[0], ls_[1])
        o_ref[:, cols] = o.astype(o_ref.dtype)


def _fox_sample(qb, kb, vb, kct, vct, lf_all, l, nb, ls, past):
    nseq = 4
    new = pl.BlockSpec((nseq * ls, D_FOX), lambda b: (b, 0))
    cache = pl.BlockSpec((None, nseq, D_FOX, past), lambda b: (l, b, 0, 0))
    return pl.pallas_call(
        functools.partial(_fox_sample_kernel, past=past, ls=ls, nseq=nseq), grid=(nb // nseq,),
        in_specs=[new, new, new, cache, cache, pl.BlockSpec((nseq, 8, past + LANE), lambda b: (b, 0, 0))],
        out_specs=new, out_shape=jax.ShapeDtypeStruct((nb * ls, D_FOX), BF16),
        compiler_params=_cparams(("parallel",)), name="fox_sample",
    )(qb, kb, vb, kct, vct, lf_all)


def _pool_kernel(u_ref, pre_ref, w_ref, sc_ref, o_ref, halo_sc, *, tm, pos0):
    t = pl.program_id(1)

    @pl.when(t == 0)
    def _():
        halo_sc[...] = pre_ref[0]

    u = u_ref[...]
    a = jnp.concatenate([halo_sc[...], u], axis=0)
    e1 = a + pltpu.roll(a, 1, 0)
    e2 = e1 + pltpu.roll(e1, 2, 0)
    e3 = e2 + pltpu.roll(e2, 4, 0)
    e4 = e3 + pltpu.roll(e3, 8, 0)
    lane = lax.broadcasted_iota(jnp.int32, (1, D_POOL), 1)
    win = jnp.where(lane < 64, e1, jnp.where(lane < 128, e2, jnp.where(lane < 192, e3, e4)))[16:]
    wsz = jnp.where(lane < 64, 2.0, jnp.where(lane < 128, 4.0, jnp.where(lane < 192, 8.0, 16.0)))
    pos = (pos0 + t * tm + lax.broadcasted_iota(jnp.int32, (tm, 1), 0)).astype(F32)
    diff = win / jnp.minimum(pos + 1.0, wsz) - u
    y = jnp.dot(diff.astype(BF16), w_ref[...], preferred_element_type=F32) * sc_ref[...]
    o_ref[...] = y.astype(o_ref.dtype)
    halo_sc[...] = u[tm - 16:]


def _pool(u, prefix, w, l, nb, nt, tm, pos0):
    return pl.pallas_call(
        functools.partial(_pool_kernel, tm=tm, pos0=pos0), grid=(nb, nt),
        in_specs=[pl.BlockSpec((tm, D_POOL), lambda b, t: (b * nt + t, 0)),
                  pl.BlockSpec((1, 16, D_POOL), lambda b, t: (b, 0, 0)),
                  _layer_spec(w["wbd"], l, 2), _layer_spec(w["pscale"], l, 2)],
        out_specs=pl.BlockSpec((tm, D_POOL), lambda b, t: (b * nt + t, 0)),
        out_shape=jax.ShapeDtypeStruct((nb * nt * tm, D_POOL), BF16),
        scratch_shapes=[pltpu.VMEM((16, D_POOL), F32)],
        compiler_params=_cparams(("parallel", "arbitrary")), name="pool_mixer",
    )(u, prefix, w["wbd"], w["pscale"])


def _ssd_kernel(xbc_ref, z_ref, sm_ref, smt_ref, pre_ref, init_ref, cw_ref, cb_ref, arow_ref,
                acol_ref, dsk_ref, gn_ref, y_ref, fin_ref, ext_sc, st_sc, *, q, l_valid):
    c = pl.program_id(1)

    @pl.when(c == 0)
    def _():
        ext_sc[:SUBLANE, :] = pre_ref[0]
        st_sc[...] = init_ref[0]

    ext_sc[SUBLANE:, :] = xbc_ref[...]
    w = cw_ref[...]
    conv = cb_ref[...]
    for j in range(CONV_W):
        conv = conv + w[j:j + 1] * ext_sc[pl.ds(SUBLANE - (CONV_W - 1) + j, q), :]
    ext_sc[:SUBLANE, :] = ext_sc[q:, :]
    act = _silu(conv)
    xs = act[:, :D_SSD]
    bbf = act[:, D_SSD:D_SSD + D_BC]
    bb = bbf.astype(BF16)
    cc = act[:, D_SSD + D_BC:].astype(BF16)
    bbt = bbf.T

    rvalid = (c * q + lax.broadcasted_iota(jnp.int32, (q, 1), 0)) < l_valid
    dtc = jnp.where(rvalid, sm_ref[...], 0.0)
    acs_c = _scan_rows(dtc * (-jnp.exp(arow_ref[...])))
    cvalid = (c * q + lax.broadcasted_iota(jnp.int32, (1, LANE), 1)) < l_valid
    dtt = jnp.where(cvalid, smt_ref[...], 0.0)
    acs_t = _scan_lanes(dtt * (-jnp.exp(acol_ref[...])))

    lane = lax.broadcasted_iota(jnp.int32, (1, LANE), 1)
    lo = lane < HD
    n_lo = lax.broadcasted_iota(jnp.int32, (LANE, 1), 0) < HD
    causal = (lax.broadcasted_iota(jnp.int32, (q, q), 1)
              <= lax.broadcasted_iota(jnp.int32, (q, q), 0))
    zc = jnp.zeros_like(cc)
    cb = [_nt(jnp.where(lo, cc, zc), bb), _nt(jnp.where(lo, zc, cc), bb)]
    ys = []
    for p in range(H // 2):
        cols = slice(p * LANE, (p + 1) * LANE)
        xp = xs[:, cols]
        xpb = xp.astype(BF16)
        st = st_sc[p]
        y_in = jnp.dot(cc, st.astype(BF16), preferred_element_type=F32)
        y_h, upd, dec = [], [], []
        for hh in range(2):
            h = 2 * p + hh
            g = h // (H // 2)
            arep = jnp.broadcast_to(acs_c[:, _DT_AT + h:_DT_AT + h + 1], (q, LANE))
            ak = acs_t[_DT_AT + h:_DT_AT + h + 1, :q]
            dtr = dtt[_DT_AT + h:_DT_AT + h + 1, :q]
            lm = jnp.exp(jnp.where(causal, arep[:, :q] - ak, NEG))
            gm = (cb[g] * lm * dtr).astype(BF16)
            y_h.append(jnp.dot(gm, xpb, preferred_element_type=F32) + y_in * jnp.exp(arep))
            alast = acs_t[_DT_AT + h:_DT_AT + h + 1, q - 1:q]
            bw = (bbt * (jnp.exp(alast - ak) * dtr)).astype(BF16)
            keep = (lo if hh == 0 else ~lo) & (n_lo if g == 0 else ~n_lo)
            upd.append(jnp.where(keep, jnp.dot(bw, xpb, preferred_element_type=F32), 0.0))
            dec.append(jnp.exp(alast))
        st_sc[p] = jnp.where(lo, dec[0], dec[1]) * st + upd[0] + upd[1]
        ys.append(jnp.where(lo, y_h[0], y_h[1]) + dsk_ref[:, cols] * xp)
    yc = jnp.concatenate(ys, axis=1) * _silu(z_ref[...])
    y_ref[...] = _rms(yc, gn_ref[...]).astype(y_ref.dtype)

    @pl.when(c == pl.num_programs(1) - 1)
    def _():
        fin_ref[0] = st_sc[...]


def _ssd(xbc, z, sm, smt, prefix, init, w, l, nb, nc, q, l_valid):
    rows = lambda c: pl.BlockSpec((q, c), lambda b, i: (b * nc + i, 0))
    if smt.ndim == 2:
        smt_spec = pl.BlockSpec((16, LANE), lambda b, i: (0, b * nc + i))
    else:
        smt_spec = pl.BlockSpec((None, 16, LANE), lambda b, i: (b, 0, 0))
    names = ("cw", "cb", "arow", "acol", "dsk", "gn")
    state = pl.BlockSpec((1, H // 2, LANE, LANE), lambda b, i: (b, 0, 0, 0))
    return pl.pallas_call(
        functools.partial(_ssd_kernel, q=q, l_valid=l_valid), grid=(nb, nc),
        in_specs=[rows(D_CONV), rows(D_SSD), rows(LANE), smt_spec,
                  pl.BlockSpec((1, 8, D_CONV), lambda b, i: (b, 0, 0)),
                  state]
                 + [_layer_spec(w[n], l, 2) for n in names],
        out_specs=(rows(D_SSD), state),
        out_shape=(jax.ShapeDtypeStruct((nb * nc * q, D_SSD), BF16),
                   jax.ShapeDtypeStruct((nb, H // 2, LANE, LANE), F32)),
        scratch_shapes=[pltpu.VMEM((q + SUBLANE, D_CONV), F32), pltpu.VMEM((H // 2, LANE, LANE), F32)],
        compiler_params=_cparams(("parallel", "arbitrary")), name="conv_ssd",
    )(xbc, z, sm, smt, prefix, init, *[w[n] for n in names])


def _state_to_wide(s):
    wide = jnp.zeros((s.shape[0], H // 2, LANE, LANE), F32)
    for h in range(H):
        g, hh = h // (H // 2), h % 2
        wide = wide.at[:, h // 2, g * HD:(g + 1) * HD, hh * HD:(hh + 1) * HD].set(
            jnp.swapaxes(s[:, h].astype(F32), 1, 2))
    return wide


def _state_from_wide(wide):
    return jnp.stack([jnp.swapaxes(wide[:, h // 2, (h // (H // 2)) * HD:(h // (H // 2) + 1) * HD,
                                        (h % 2) * HD:(h % 2 + 1) * HD], 1, 2) for h in range(H)], axis=1)


def _ffn_kernel(x_ref, a_ref, p_ref, s_ref, wo_ref, g1_ref, g2_ref, g3_ref, wg_ref, wu_ref, wd_ref, o_ref):
    mp = (jnp.dot(a_ref[...], wo_ref[:D_FOX, :], preferred_element_type=F32)
          + jnp.dot(p_ref[...], wo_ref[D_FOX:D_FOX + D_POOL, :], preferred_element_type=F32)
          + jnp.dot(s_ref[...], wo_ref[D_FOX + D_POOL:, :], preferred_element_type=F32))
    x1 = x_ref[...] + _rms(mp, g1_ref[...])
    hb = _rms(x1, g2_ref[...]).astype(BF16)
    gate = jnp.dot(hb, wg_ref[...], preferred_element_type=F32)
    up = jnp.dot(hb, wu_ref[...], preferred_element_type=F32)
    act = (_silu(gate) * up).astype(BF16)
    ff = jnp.dot(act, wd_ref[...], preferred_element_type=F32)
    o_ref[...] = x1 + _rms(ff, g3_ref[...])


def _ffn(x, attn, pool, ssd, w, l, tm):
    t = x.shape[0]
    row = lambda c: pl.BlockSpec((tm, c), lambda i: (i, 0))
    names = ("wo", "g1", "g2", "g3", "wg", "wu", "wd")
    return pl.pallas_call(
        _ffn_kernel, grid=(t // tm,),
        in_specs=[row(D_MODEL), row(D_FOX), row(D_POOL), row(D_SSD)]
                 + [_layer_spec(w[n], l, 1, single_buffer=True) for n in names],
        out_specs=row(D_MODEL), out_shape=jax.ShapeDtypeStruct((t, D_MODEL), F32),
        compiler_params=_cparams(("parallel",)), name="outproj_ffn",
    )(x, attn, pool, ssd, *[w[n] for n in names])


def _lane_pack(f_vals, dt_vals):
    out = jnp.zeros((f_vals.shape[0], LANE), F32).at[:, _DT_AT:_DT_AT + H].set(dt_vals.astype(F32))
    for at in _F_COPIES:
        out = out.at[:, at:at + H].set(f_vals.astype(F32))
    return out


def _prep_weights(w_in, fox_f_bias, pool_w, pool_scale, conv_w, conv_b, dt_bias, a_log, d_skip, ssd_norm,
                  w_out, w_gate, w_up, w_down, ln_pre_mix, ln_post_mix, ln_pre_ffn, ln_post_ffn):
    depth = w_in.shape[0]
    f0 = 3 * D_FOX
    u0 = f0 + H
    dt0 = u0 + D_POOL + D_SSD + D_CONV
    ws = jnp.zeros((depth, D_MODEL, LANE), F32).at[:, :, _DT_AT:_DT_AT + H].set(w_in[:, :, dt0:dt0 + H])
    for at in _F_COPIES:
        ws = ws.at[:, :, at:at + H].set(w_in[:, :, f0:u0])
    ws = ws.astype(BF16)
    wm = jnp.concatenate([w_in[:, :, :D_FOX].astype(BF16), ws, w_in[:, :, u0:dt0].astype(BF16)], axis=2)
    bs = _lane_pack(fox_f_bias, dt_bias)
    wbd = jnp.zeros((depth, D_POOL, D_POOL), F32)
    for g in range(4):
        wbd = wbd.at[:, g * 64:(g + 1) * 64, g * 64:(g + 1) * 64].set(pool_w[:, g])
    alog = _lane_pack(jnp.zeros_like(a_log), a_log)
    row = lambda a: a.astype(F32).reshape(depth, 1, -1)
    tr = lambda a: jnp.transpose(a, (0, 2, 1))
    return dict(
        g_pre=row(ln_pre_mix), wm=wm, wkv=w_in[:, :, D_FOX:f0].astype(BF16),
        wkt=tr(w_in[:, :, D_FOX:2 * D_FOX]).astype(BF16), wvt=tr(w_in[:, :, 2 * D_FOX:f0]).astype(BF16),
        wst=tr(ws[:, :, :16]),
        bs=bs.reshape(depth, 1, LANE), bst=bs[:, :16].reshape(depth, 16, 1),
        wbd=wbd.astype(BF16), pscale=row(pool_scale),
        cw=jnp.zeros((depth, 8, D_CONV), F32).at[:, :CONV_W].set(conv_w), cb=row(conv_b),
        arow=alog.reshape(depth, 1, LANE), acol=alog[:, :16].reshape(depth, 16, 1),
        dsk=row(jnp.repeat(d_skip, HD, axis=1)), gn=row(ssd_norm),
        wo=w_out.astype(BF16), g1=row(ln_post_mix), g2=row(ln_pre_ffn), g3=row(ln_post_ffn),
        wg=w_gate.astype(BF16), wu=w_up.astype(BF16), wd=w_down.astype(BF16))


def _mixers_and_ffn(x, w, l, attn_fn, prompt_dims, nb, nc, q, l_valid, pos0, pool_prefix, conv_prefix,
                    ssd_init, smt_fn, tm_tok, tm_pool):
    proj = _inproj(x, w, l, tm_tok, prompt_dims)
    u, z, xbc, sm, smt = proj[5:]
    attn = attn_fn(proj[:5], sm, smt)
    seq = nc * q
    pool = _pool(u, pool_prefix, w, l, nb, seq // tm_pool, tm_pool, pos0)
    ssd, fin = _ssd(xbc, z, sm, smt_fn(smt), conv_prefix, ssd_init, w, l, nb, nc, q, l_valid)
    x = _ffn(x, attn, pool, ssd, w, l, tm_tok)
    return x, proj, fin


def kernel(x_prompt, x_sample, cache_fox_k, cache_fox_v, cache_fox_logf, state_pool, state_conv, state_ssd,
           meta_tokens, ln_pre_mix, ln_post_mix, ln_pre_ffn, ln_post_ffn, w_in, fox_f_bias, pool_w, pool_scale,
           conv_w, conv_b, dt_bias, a_log, d_skip, ssd_norm, w_out, w_gate, w_up, w_down):
    nbp, seq, _ = x_prompt.shape
    nbs, ls, _ = x_sample.shape
    depth, _, past = cache_fox_logf.shape[:3]
    lp = N_META + seq
    tq = 256
    q_ssd = 128
    tp = -(-lp // tq) * tq
    pool_tile = tp // 4
    assert pool_tile % 16 == 0

    w = _prep_weights(w_in, fox_f_bias, pool_w, pool_scale, conv_w, conv_b, dt_bias, a_log, d_skip, ssd_norm,
                      w_out, w_gate, w_up, w_down, ln_pre_mix, ln_post_mix, ln_pre_ffn, ln_post_ffn)

    meta = jnp.broadcast_to(meta_tokens.astype(F32)[None], (nbp, N_META, D_MODEL))
    xp = jnp.concatenate([meta, x_prompt, jnp.zeros((nbp, tp - lp, D_MODEL), F32)], axis=1)
    xp = xp.reshape(nbp * tp, D_MODEL)
    xs = x_sample.reshape(nbs * ls, D_MODEL)

    zero_pool = jnp.zeros((nbp, 16, D_POOL), F32)
    zero_conv = jnp.zeros((nbp, 8, D_CONV), F32)
    zero_ssd = jnp.zeros((nbp, H // 2, LANE, LANE), F32)
    ssd_init = _state_to_wide(state_ssd.reshape(depth * nbs, H, HD, HD)).reshape(depth, nbs, H // 2, LANE, LANE)

    kct = jnp.transpose(cache_fox_k, (0, 1, 3, 4, 2)).reshape(depth, nbs, D_FOX, past)
    vct = jnp.transpose(cache_fox_v, (0, 1, 3, 4, 2)).reshape(depth, nbs, D_FOX, past)
    lfc = jnp.pad(jnp.transpose(cache_fox_logf.astype(F32), (0, 1, 3, 2)),
                  ((0, 0), (0, 0), (0, 8 - H), (0, 0)))
    pool_pre = jnp.pad(state_pool.astype(F32), ((0, 0), (0, 0), (16 - POOL_BUF, 0), (0, 0)))
    conv_pre = jnp.pad(state_conv.astype(F32), ((0, 0), (0, 0), (8 - (CONV_W - 1), 0), (0, 0)))

    def to_seq_major(smt):
        return jnp.transpose(smt.reshape(16, nbs, ls), (1, 0, 2))

    outs_p, outs_s = [], []
    for l in range(depth):
        def attn_p(qkv, sm, smt):
            qb, kb, _, _, vtb = qkv
            return _fox_prompt(qb, kb, _forget_bias(sm, nbp, tp), vtb, nbp, tp, tq)

        xp, proj, fin = _mixers_and_ffn(xp, w, l, attn_p, (nbp, tp, lp), nbp, tp // q_ssd, q_ssd, lp, 0,
                                        zero_pool, zero_conv, zero_ssd, lambda smt: smt, tq, pool_tile)
        outs_p.append((proj[2], proj[3], proj[9], proj[5], proj[7], fin))

        def attn_s(qkv, sm, smt):
            qb, _, kb, _, vb = qkv
            lf_all = jnp.concatenate([lfc[l], to_seq_major(smt)[:, :8], jnp.zeros((nbs, 8, LANE - ls), F32)],
                                     axis=2)
            return _fox_sample(qb, kb, vb, kct, vct, lf_all, l, nbs, ls, past)

        xs, proj, fin = _mixers_and_ffn(xs, w, l, attn_s, None, nbs, 1, ls, ls, past, pool_pre[l], conv_pre[l],
                                        ssd_init[l],
                                        lambda smt: jnp.pad(to_seq_major(smt), ((0, 0), (0, 0), (0, LANE - ls))),
                                        tq, ls)
        outs_s.append((proj[1], proj[3], proj[9], proj[5], proj[7], fin))

    def tails(outs, nb, rows, valid):
        seqv = lambda a: a.reshape(nb, rows, a.shape[-1])
        lf = jnp.stack([jnp.transpose(o[2][:H].reshape(H, nb, rows)[:, :, :valid], (1, 2, 0)) for o in outs])
        pn = jnp.stack([seqv(o[3])[:, valid - POOL_BUF:valid] for o in outs])
        cn = jnp.stack([seqv(o[4])[:, valid - (CONV_W - 1):valid] for o in outs])
        sn = jnp.stack([_state_from_wide(o[5]) for o in outs])
        return lf, pn, cn, sn

    def token_minor(outs, i):
        a = jnp.stack([o[i] for o in outs]).reshape(depth, nbp, H, HD, lp)
        return jnp.transpose(a, (0, 1, 4, 2, 3))

    def token_major(outs, i):
        return jnp.stack([o[i].reshape(nbs, ls, H, HD) for o in outs])

    y_prompt = xp.reshape(nbp, tp, D_MODEL)[:, N_META:lp]
    y_sample = xs.reshape(nbs, ls, D_MODEL)
    return ((y_prompt, y_sample, token_minor(outs_p, 0), token_minor(outs_p, 1)) + tails(outs_p, nbp, tp, lp)
            + (token_major(outs_s, 0), token_major(outs_s, 1)) + tails(outs_s, nbs, ls, ls))
```

```python
import functools
import math

import jax
import jax.numpy as jnp
from jax import lax
from jax.experimental import pallas as pl
from jax.experimental.pallas import tpu as pltpu

F32 = jnp.float32
BF16 = jnp.bfloat16

D_MODEL = 1024
N_META = 16
EPS = 1e-6
H = 6
HD = 64
D_FOX = H * HD
D_POOL = 256
POOL_BUF = 15
D_SSD = H * HD
D_BC = 128
D_CONV = D_SSD + 2 * D_BC
CONV_W = 4
D_FF = 2816
LANE = 128
SUBLANE = 8
NEG = -1e30
LOG2E = math.log2(math.e)

_Q0, _S0, _U0, _Z0, _X0, _MAIN = 0, 384, 512, 768, 1152, 1792
_MXU_N = 256
_GROUPS = ((_Q0, _U0), (_U0, _Z0), (_Z0, _X0 + LANE), (_X0 + LANE, _MAIN))
_F_AT, _DT_AT, _F_COPIES = 0, 8, (0, 16, 32)
_VROWS = HD + SUBLANE

_VMEM_LIMIT = 56 * 1024 * 1024


def _cparams(sem):
    return pltpu.CompilerParams(dimension_semantics=sem, vmem_limit_bytes=_VMEM_LIMIT)


def _layer_spec(a, l, ngrid, single_buffer=False):
    idx = (l,) + (0,) * (a.ndim - 1)
    kw = dict(pipeline_mode=pl.Buffered(1)) if single_buffer else {}
    return pl.BlockSpec((None,) + a.shape[1:], lambda *_: idx, **kw)


def _rms(x, g):
    ms = jnp.mean(x * x, axis=-1, keepdims=True)
    return x * lax.rsqrt(ms + EPS) * g


def _silu(x):
    return x * jax.nn.sigmoid(x)


def _softplus_tail(x):
    return jnp.log1p(jnp.exp(-jnp.abs(x)))


def _nt(a, b):
    return lax.dot_general(a, b, (((1,), (1,)), ((), ())), preferred_element_type=F32)


def _tn(a, b):
    return lax.dot_general(a, b, (((0,), (0,)), ((), ())), preferred_element_type=F32)


def _scan_lanes(x):
    lane = lax.broadcasted_iota(jnp.int32, x.shape, 1)
    s = 1
    while s < x.shape[1]:
        x = x + jnp.where(lane >= s, pltpu.roll(x, s, 1), 0.0)
        s *= 2
    return x


def _scan_rows(x):
    row = lax.broadcasted_iota(jnp.int32, x.shape, 0)
    s = 1
    while s < x.shape[0]:
        x = x + jnp.where(row >= s, pltpu.roll(x, s, 0), 0.0)
        s *= 2
    return x


def _inproj_kernel(x_ref, g_ref, wm_ref, wst_ref, bs_ref, bst_ref, *rest, prompt, qscale):
    hb = _rms(x_ref[...], g_ref[...]).astype(BF16)

    if prompt:
        wkt_ref, wvt_ref, qb_ref, kb_ref, kt_ref, vt_ref, vtb_ref, u_ref, z_ref, xbc_ref, sm_ref, smt_ref = rest
        kt = _nt(wkt_ref[...], hb)
        kt_ref[...] = kt
        kb_ref[...] = kt.T.astype(BF16)
        vt = _nt(wvt_ref[...], hb)
        vt_ref[...] = vt
        tm = vt.shape[1]
        ones_row = jnp.where(lax.broadcasted_iota(jnp.int32, (SUBLANE, tm), 0) == 0, 1.0, 0.0).astype(BF16)
        for h in range(H):
            vtb_ref[0, h * _VROWS:h * _VROWS + HD, :] = vt[h * HD:(h + 1) * HD].astype(BF16)
            vtb_ref[0, h * _VROWS + HD:(h + 1) * _VROWS, :] = ones_row
    else:
        wkv_ref, qb_ref, k_ref, kb_ref, v_ref, vb_ref, u_ref, z_ref, xbc_ref, sm_ref, smt_ref = rest
        kv = jnp.dot(hb, wkv_ref[...], preferred_element_type=F32)
        k_ref[...] = kv[:, :D_FOX]
        kb_ref[...] = kv[:, :D_FOX].astype(BF16)
        v_ref[...] = kv[:, D_FOX:]
        vb_ref[...] = kv[:, D_FOX:].astype(BF16)
    g0, g1, g2, g3 = (jnp.dot(hb, wm_ref[:, a:b], preferred_element_type=F32) for a, b in _GROUPS)
    qb_ref[...] = (g0[:, :_S0] * qscale).astype(BF16)
    u_ref[...] = g1
    z_ref[...] = g2[:, :_X0 - _Z0]
    xbc_ref[:, :LANE] = g2[:, _X0 - _Z0:]
    xbc_ref[:, LANE:] = g3

    sm = g0[:, _S0:] + bs_ref[...]
    lane = lax.broadcasted_iota(jnp.int32, sm.shape, 1)
    tail = _softplus_tail(sm)
    is_dt = (lane >= _DT_AT) & (lane < _DT_AT + 8)
    sm_ref[...] = jnp.where(is_dt, jnp.maximum(sm, 0.0) + tail, jnp.minimum(sm, 0.0) - tail)
    smt = _nt(wst_ref[...], hb) + bst_ref[...]
    row = lax.broadcasted_iota(jnp.int32, smt.shape, 0)
    tail = _softplus_tail(smt)
    smt_ref[...] = jnp.where(row < _DT_AT, jnp.minimum(smt, 0.0) - tail, jnp.maximum(smt, 0.0) + tail)


def _inproj(x, w, l, tm, prompt_dims=None):
    t = x.shape[0]
    nt = t // tm
    row = lambda c: pl.BlockSpec((tm, c), lambda i: (i, 0))
    sds = jax.ShapeDtypeStruct
    tail_shapes = (sds((t, D_POOL), F32), sds((t, D_SSD), F32), sds((t, D_CONV), F32),
                   sds((t, LANE), F32), sds((16, t), F32))
    tail_specs = (row(D_POOL), row(D_SSD), row(D_CONV), row(LANE), pl.BlockSpec((16, tm), lambda i: (0, i)))
    if prompt_dims is not None:
        nb, tp, lp = prompt_dims
        per = tp // tm
        tmin = pl.BlockSpec((None, D_FOX, tm), lambda i: (i // per, 0, i % per))
        out_shape = (sds((t, D_FOX), BF16), sds((t, D_FOX), BF16), sds((nb, D_FOX, lp), F32),
                     sds((nb, D_FOX, lp), F32), sds((nt, H * _VROWS, tm), BF16)) + tail_shapes
        out_specs = (row(D_FOX), row(D_FOX), tmin, tmin,
                     pl.BlockSpec((1, H * _VROWS, tm), lambda i: (i, 0, 0))) + tail_specs
        qscale = (HD ** -0.5) * LOG2E
    else:
        out_shape = (sds((t, D_FOX), BF16), sds((t, D_FOX), F32), sds((t, D_FOX), BF16),
                     sds((t, D_FOX), F32), sds((t, D_FOX), BF16)) + tail_shapes
        out_specs = (row(D_FOX),) * 5 + tail_specs
        qscale = HD ** -0.5
    names = ("g_pre", "wm", "wst", "bs", "bst") + (("wkt", "wvt") if prompt_dims is not None else ("wkv",))
    return pl.pallas_call(
        functools.partial(_inproj_kernel, prompt=prompt_dims is not None, qscale=qscale), grid=(nt,),
        in_specs=[row(D_MODEL)] + [_layer_spec(w[n], l, 1) for n in names],
        out_specs=out_specs, out_shape=out_shape,
        compiler_params=_cparams(("parallel",)), name="inproj",
    )(x, *[w[n] for n in names])


def _forget_bias_kernel(sm_ref, e_ref, *, nblk, rows):
    lane = lax.broadcasted_iota(jnp.int32, (1, LANE), 1)
    keep = ((lane & 15) < H) & (lane < 48)
    carry = jnp.zeros((1, LANE), F32)
    for j in range(nblk):
        blk = slice(j * rows, (j + 1) * rows)
        c = _scan_rows(sm_ref[blk, :]) + carry
        carry = c[rows - 1:rows, :]
        c2 = c * LOG2E
        hi = c2.astype(BF16).astype(F32)
        mid = (c2 - hi).astype(BF16).astype(F32)
        low = (c2 - hi) - mid
        e = jnp.where(lane < 16, hi, jnp.where(lane < 32, mid, low))
        e_ref[blk, :] = jnp.where(keep, e, 0.0).astype(BF16)


def _forget_bias(sm, nb, tp):
    rows = LANE
    return pl.pallas_call(
        functools.partial(_forget_bias_kernel, nblk=tp // rows, rows=rows), grid=(nb,),
        in_specs=[pl.BlockSpec((tp, LANE), lambda b: (b, 0))],
        out_specs=pl.BlockSpec((tp, LANE), lambda b: (b, 0)),
        out_shape=jax.ShapeDtypeStruct((nb * tp, LANE), BF16),
        compiler_params=_cparams(("parallel",)), name="forget_bias",
    )(sm)


def _fox_prompt_kernel(q_ref, k_ref, e_ref, vt_ref, o_ref, qa_sc, m_sc, acc_sc, sa_sc, sb_sc, *, tq):
    i = pl.program_id(1)
    lane = lax.broadcasted_iota(jnp.int32, (1, LANE), 1)
    causal = (lax.broadcasted_iota(jnp.int32, (tq, tq), 0)
              <= lax.broadcasted_iota(jnp.int32, (tq, tq), 1))
    for h in range(H):
        q2 = q_ref[:, (h // 2) * LANE:(h // 2 + 1) * LANE]
        mine = (lane < HD) if h % 2 == 0 else (lane >= HD)
        qm = jnp.where(mine, q2, jnp.zeros_like(q2))
        pick = (lane == _F_COPIES[0] + h) | (lane == _F_COPIES[1] + h) | (lane == _F_COPIES[2] + h)
        qe = jnp.broadcast_to(jnp.where(pick, -1.0, 0.0), (tq, LANE)).astype(BF16)
        qa_sc[h] = jnp.concatenate([qm, qe], axis=1)
    m_sc[...] = jnp.full(m_sc.shape, NEG, F32)
    acc_sc[...] = jnp.zeros(acc_sc.shape, F32)

    def logits(j, s_sc):
        rows = pl.ds(pl.multiple_of(j * tq, tq), tq)
        kk = k_ref[rows, :]
        ee = e_ref[rows, :]
        for h in range(H):
            ka = jnp.concatenate([kk[:, (h // 2) * LANE:(h // 2 + 1) * LANE], ee], axis=1)
            s_sc[h] = _nt(ka, qa_sc[h])

    def softmax_pv(j, s_sc, masked):
        for h in range(H):
            s = s_sc[h]
            if masked:
                s = jnp.where(causal, s, NEG)
            m_old = m_sc[h:h + 1, :]
            m_new = jnp.maximum(m_old, jnp.max(s, axis=0, keepdims=True))
            m_sc[h:h + 1, :] = m_new
            pm = jnp.exp2(s - m_new).astype(BF16)
            hs = slice(h * _VROWS, (h + 1) * _VROWS)
            pv = jnp.dot(vt_ref[j, hs, :], pm, preferred_element_type=F32)
            acc_sc[hs, :] = jnp.exp2(m_old - m_new) * acc_sc[hs, :] + pv

    def body(jj, carry):
        j = 2 * jj
        logits(j + 1, sb_sc)
        softmax_pv(j, sa_sc, False)
        logits(j + 2, sa_sc)
        softmax_pv(j + 1, sb_sc, False)
        return carry

    logits(0, sa_sc)
    lax.fori_loop(0, i // 2, body, 0)

    @pl.when(i % 2 == 0)
    def _():
        softmax_pv(i, sa_sc, True)

    @pl.when(i % 2 == 1)
    def _():
        logits(i, sb_sc)
        softmax_pv(i - 1, sa_sc, False)
        softmax_pv(i, sb_sc, True)

    out = jnp.concatenate([acc_sc[h * _VROWS:h * _VROWS + HD, :] / acc_sc[h * _VROWS + HD:h * _VROWS + HD + 1, :]
                           for h in range(H)], axis=0)
    o_ref[...] = out.T.astype(o_ref.dtype)


def _fox_prompt(qb, kb, eb, vt, nb, tp, tq):
    nq = tp // tq
    return pl.pallas_call(
        functools.partial(_fox_prompt_kernel, tq=tq), grid=(nb, nq),
        in_specs=[pl.BlockSpec((tq, D_FOX), lambda b, i: (b * nq + i, 0)),
                  pl.BlockSpec((tp, D_FOX), lambda b, i: (b, 0)),
                  pl.BlockSpec((tp, LANE), lambda b, i: (b, 0)),
                  pl.BlockSpec((nq, H * _VROWS, tq), lambda b, i: (b, 0, 0))],
        out_specs=pl.BlockSpec((tq, D_FOX), lambda b, i: (b * nq + i, 0)),
        out_shape=jax.ShapeDtypeStruct((nb * tp, D_FOX), BF16),
        scratch_shapes=[pltpu.VMEM((H, tq, 2 * LANE), BF16), pltpu.VMEM((8, tq), F32),
                        pltpu.VMEM((H * _VROWS, tq), F32),
                        pltpu.VMEM((H, tq, tq), F32), pltpu.VMEM((H, tq, tq), F32)],
        compiler_params=_cparams(("parallel", "arbitrary")), name="fox_prompt",
    )(qb, kb, eb, vt)


def _fox_sample_kernel(q_ref, kn_ref, vn_ref, kct_ref, vct_ref, lf_ref, o_ref, *, past, ls, nseq):
    for s in range(nseq):
        rows = slice(s * ls, (s + 1) * ls)
        _fox_sample_one(q_ref.at[rows], kn_ref.at[rows], vn_ref.at[rows], kct_ref.at[s], vct_ref.at[s],
                        lf_ref.at[s], o_ref.at[rows], past, ls)


def _fox_sample_one(q_ref, kn_ref, vn_ref, kct_ref, vct_ref, lf_ref, o_ref, past, ls):
    lo = lax.broadcasted_iota(jnp.int32, (1, LANE), 1) < HD
    causal = (lax.broadcasted_iota(jnp.int32, (ls, ls), 1)
              <= lax.broadcasted_iota(jnp.int32, (ls, ls), 0))
    nblk = lf_ref.shape[1] // LANE
    loc = _scan_lanes(jnp.concatenate([lf_ref[:, j * LANE:(j + 1) * LANE] for j in range(nblk)], axis=0))
    carry = jnp.zeros((8, 1), F32)
    cs = []
    for j in range(nblk):
        blk = loc[j * 8:(j + 1) * 8]
        cs.append(blk + carry)
        carry = carry + blk[:, LANE - 1:LANE]
    c_all = jnp.concatenate(cs, axis=1)
    for p in range(H // 2):
        cols = slice(p * LANE, (p + 1) * LANE)
        q2 = q_ref[:, cols]
        kct = kct_ref[cols, :].astype(BF16)
        vct = vct_ref[cols, :].astype(BF16)
        kn = kn_ref[:, cols]
        vn = vn_ref[:, cols]
        outs, ls_ = [], []
        for hh in range(2):
            hrow = slice(2 * p + hh, 2 * p + hh + 1)
            qm = jnp.where(lo, q2, jnp.zeros_like(q2)) if hh == 0 else jnp.where(lo, jnp.zeros_like(q2), q2)
            c0 = c_all[hrow, past - 1:past]
            s_c = jnp.dot(qm, kct, preferred_element_type=F32) + (c0 - c_all[hrow, :past])
            s_n = _nt(qm, kn) + (c0 - c_all[hrow, past:past + ls])
            s_n = jnp.where(causal, s_n, NEG)
            m = jnp.maximum(jnp.max(s_c, axis=-1, keepdims=True), jnp.max(s_n, axis=-1, keepdims=True))
            p_c = jnp.exp(s_c - m)
            p_n = jnp.exp(s_n - m)
            ls_.append(jnp.sum(p_c, axis=-1, keepdims=True) + jnp.sum(p_n, axis=-1, keepdims=True))
            outs.append(_nt(p_c.astype(BF16), vct) + jnp.dot(p_n.astype(BF16), vn, preferred_element_type=F32))
        o = jnp.where(lo, outs[0], outs[1]) / jnp.where(lo, ls_[0], ls_[1])
        o_ref[:, cols] = o.astype(o_ref.dtype)


def _fox_sample(qb, kb, vb, kct, vct, lf_all, l, nb, ls, past):
    nseq = 4
    new = pl.BlockSpec((nseq * ls, D_FOX), lambda b: (b, 0))
    cache = pl.BlockSpec((None, nseq, D_FOX, past), lambda b: (l, b, 0, 0))
    return pl.pallas_call(
        functools.partial(_fox_sample_kernel, past=past, ls=ls, nseq=nseq), grid=(nb // nseq,),
        in_specs=[new, new, new, cache, cache, pl.BlockSpec((nseq, 8, past + LANE), lambda b: (b, 0, 0))],
        out_specs=new, out_shape=jax.ShapeDtypeStruct((nb * ls, D_FOX), BF16),
        compiler_params=_cparams(("parallel",)), name="fox_sample",
    )(qb, kb, vb, kct, vct, lf_all)


def _pool_tile(u, halo, w_ref, sc_ref, pos_start, tm):
    a = jnp.concatenate([halo, u], axis=0)
    e1 = a + pltpu.roll(a, 1, 0)
    e2 = e1 + pltpu.roll(e1, 2, 0)
    e3 = e2 + pltpu.roll(e2, 4, 0)
    e4 = e3 + pltpu.roll(e3, 8, 0)
    lane = lax.broadcasted_iota(jnp.int32, (1, D_POOL), 1)
    win = jnp.where(lane < 64, e1, jnp.where(lane < 128, e2, jnp.where(lane < 192, e3, e4)))[16:]
    wsz = jnp.where(lane < 64, 2.0, jnp.where(lane < 128, 4.0, jnp.where(lane < 192, 8.0, 16.0)))
    pos = (pos_start + lax.broadcasted_iota(jnp.int32, (tm, 1), 0)).astype(F32)
    diff = win / jnp.minimum(pos + 1.0, wsz) - u
    return jnp.dot(diff.astype(BF16), w_ref[...], preferred_element_type=F32) * sc_ref[...]


def _pool_kernel(u_ref, pre_ref, w_ref, sc_ref, o_ref, halo_sc, *, tm, pos0):
    t = pl.program_id(1)

    @pl.when(t == 0)
    def _():
        halo_sc[...] = pre_ref[0]

    u = u_ref[...]
    o_ref[...] = _pool_tile(u, halo_sc[...], w_ref, sc_ref, pos0 + t * tm, tm).astype(o_ref.dtype)
    halo_sc[...] = u[tm - 16:]


def _pool(u, prefix, w, l, nb, nt, tm, pos0):
    return pl.pallas_call(
        functools.partial(_pool_kernel, tm=tm, pos0=pos0), grid=(nb, nt),
        in_specs=[pl.BlockSpec((tm, D_POOL), lambda b, t: (b * nt + t, 0)),
                  pl.BlockSpec((1, 16, D_POOL), lambda b, t: (b, 0, 0)),
                  _layer_spec(w["wbd"], l, 2), _layer_spec(w["pscale"], l, 2)],
        out_specs=pl.BlockSpec((tm, D_POOL), lambda b, t: (b * nt + t, 0)),
        out_shape=jax.ShapeDtypeStruct((nb * nt * tm, D_POOL), BF16),
        scratch_shapes=[pltpu.VMEM((16, D_POOL), F32)],
        compiler_params=_cparams(("parallel", "arbitrary")), name="pool_mixer",
    )(u, prefix, w["wbd"], w["pscale"])


def _ssd_chunk(xbc, z, sm, smt, ext_sc, st_sc, cw_ref, cb_ref, arow_ref, acol_ref, dsk_ref, gn_ref,
               row0, q, l_valid, live=None):
    ext_sc[SUBLANE:, :] = xbc
    w = cw_ref[...]
    conv = cb_ref[...]
    for j in range(CONV_W):
        conv = conv + w[j:j + 1] * ext_sc[pl.ds(SUBLANE - (CONV_W - 1) + j, q), :]
    ext_sc[:SUBLANE, :] = ext_sc[q:, :]
    act = _silu(conv)
    xs = act[:, :D_SSD]
    bbf = act[:, D_SSD:D_SSD + D_BC]
    bb = bbf.astype(BF16)
    cc = act[:, D_SSD + D_BC:].astype(BF16)
    bbt = bbf.T

    rvalid = (row0 + lax.broadcasted_iota(jnp.int32, (q, 1), 0)) < l_valid
    dtc = jnp.where(rvalid, sm, 0.0)
    acs_c = _scan_rows(dtc * (-jnp.exp(arow_ref[...])))
    cvalid = (row0 + lax.broadcasted_iota(jnp.int32, (1, LANE), 1)) < l_valid
    dtt = jnp.where(cvalid, smt, 0.0)
    acs_t = _scan_lanes(dtt * (-jnp.exp(acol_ref[...])))

    lane = lax.broadcasted_iota(jnp.int32, (1, LANE), 1)
    lo = lane < HD
    n_lo = lax.broadcasted_iota(jnp.int32, (LANE, 1), 0) < HD
    causal = (lax.broadcasted_iota(jnp.int32, (q, q), 1)
              <= lax.broadcasted_iota(jnp.int32, (q, q), 0))
    zc = jnp.zeros_like(cc)
    cb = [_nt(jnp.where(lo, cc, zc), bb), _nt(jnp.where(lo, zc, cc), bb)]
    ys = []
    for p in range(H // 2):
        cols = slice(p * LANE, (p + 1) * LANE)
        xp = xs[:, cols]
        xpb = xp.astype(BF16)
        st = st_sc[p]
        y_in = jnp.dot(cc, st.astype(BF16), preferred_element_type=F32)
        y_h, upd, dec = [], [], []
        for hh in range(2):
            h = 2 * p + hh
            g = h // (H // 2)
            arep = jnp.broadcast_to(acs_c[:, _DT_AT + h:_DT_AT + h + 1], (q, LANE))
            ak = acs_t[_DT_AT + h:_DT_AT + h + 1, :q]
            dtr = dtt[_DT_AT + h:_DT_AT + h + 1, :q]
            lm = jnp.exp(jnp.where(causal, arep[:, :q] - ak, NEG))
            gm = (cb[g] * lm * dtr).astype(BF16)
            y_h.append(jnp.dot(gm, xpb, preferred_element_type=F32) + y_in * jnp.exp(arep))
            alast = acs_t[_DT_AT + h:_DT_AT + h + 1, q - 1:q]
            bw = (bbt * (jnp.exp(alast - ak) * dtr)).astype(BF16)
            keep = (lo if hh == 0 else ~lo) & (n_lo if g == 0 else ~n_lo)
            upd.append(jnp.where(keep, jnp.dot(bw, xpb, preferred_element_type=F32), 0.0))
            dec.append(jnp.exp(alast))
        st_new = jnp.where(lo, dec[0], dec[1]) * st + upd[0] + upd[1]
        st_sc[p] = st_new if live is None else jnp.where(live, st_new, st)
        ys.append(jnp.where(lo, y_h[0], y_h[1]) + dsk_ref[:, cols] * xp)
    yc = jnp.concatenate(ys, axis=1) * _silu(z)
    return _rms(yc, gn_ref[...])


def _ssd_kernel(xbc_ref, z_ref, sm_ref, smt_ref, pre_ref, init_ref, cw_ref, cb_ref, arow_ref,
                acol_ref, dsk_ref, gn_ref, y_ref, fin_ref, ext_sc, st_sc, *, q, l_valid):
    c = pl.program_id(1)

    @pl.when(c == 0)
    def _():
        ext_sc[:SUBLANE, :] = pre_ref[0]
        st_sc[...] = init_ref[0]

    y = _ssd_chunk(xbc_ref[...], z_ref[...], sm_ref[...], smt_ref[...], ext_sc, st_sc, cw_ref, cb_ref,
                   arow_ref, acol_ref, dsk_ref, gn_ref, c * q, q, l_valid)
    y_ref[...] = y.astype(y_ref.dtype)

    @pl.when(c == pl.num_programs(1) - 1)
    def _():
        fin_ref[0] = st_sc[...]


def _ssd(xbc, z, sm, smt, prefix, init, w, l, nb, nc, q, l_valid):
    rows = lambda c: pl.BlockSpec((q, c), lambda b, i: (b * nc + i, 0))
    if smt.ndim == 2:
        smt_spec = pl.BlockSpec((16, LANE), lambda b, i: (0, b * nc + i))
    else:
        smt_spec = pl.BlockSpec((None, 16, LANE), lambda b, i: (b, 0, 0))
    names = ("cw", "cb", "arow", "acol", "dsk", "gn")
    state = pl.BlockSpec((1, H // 2, LANE, LANE), lambda b, i: (b, 0, 0, 0))
    return pl.pallas_call(
        functools.partial(_ssd_kernel, q=q, l_valid=l_valid), grid=(nb, nc),
        in_specs=[rows(D_CONV), rows(D_SSD), rows(LANE), smt_spec,
                  pl.BlockSpec((1, 8, D_CONV), lambda b, i: (b, 0, 0)),
                  state]
                 + [_layer_spec(w[n], l, 2) for n in names],
        out_specs=(rows(D_SSD), state),
        out_shape=(jax.ShapeDtypeStruct((nb * nc * q, D_SSD), BF16),
                   jax.ShapeDtypeStruct((nb, H // 2, LANE, LANE), F32)),
        scratch_shapes=[pltpu.VMEM((q + SUBLANE, D_CONV), F32), pltpu.VMEM((H // 2, LANE, LANE), F32)],
        compiler_params=_cparams(("parallel", "arbitrary")), name="conv_ssd",
    )(xbc, z, sm, smt, prefix, init, *[w[n] for n in names])


def _state_to_wide(s):
    def placed(h):
        g, hh = h // (H // 2), h % 2
        return jnp.pad(jnp.swapaxes(s[:, h].astype(F32), 1, 2),
                       ((0, 0), (g * HD, LANE - (g + 1) * HD), (hh * HD, LANE - (hh + 1) * HD)))
    return jnp.stack([placed(2 * p) + placed(2 * p + 1) for p in range(H // 2)], axis=1)


def _state_from_wide(wide):
    return jnp.stack([jnp.swapaxes(wide[:, h // 2, (h // (H // 2)) * HD:(h // (H // 2) + 1) * HD,
                                        (h % 2) * HD:(h % 2 + 1) * HD], 1, 2) for h in range(H)], axis=1)


def _ffn_tile(x, a, p, s, wo_ref, g1_ref, g2_ref, g3_ref, wg_ref, wu_ref, wd_ref):
    mp = (jnp.dot(a, wo_ref[:D_FOX, :], preferred_element_type=F32)
          + jnp.dot(p, wo_ref[D_FOX:D_FOX + D_POOL, :], preferred_element_type=F32)
          + jnp.dot(s, wo_ref[D_FOX + D_POOL:, :], preferred_element_type=F32))
    x1 = x + _rms(mp, g1_ref[...])
    hb = _rms(x1, g2_ref[...]).astype(BF16)
    gate = jnp.dot(hb, wg_ref[...], preferred_element_type=F32)
    up = jnp.dot(hb, wu_ref[...], preferred_element_type=F32)
    act = (_silu(gate) * up).astype(BF16)
    ff = jnp.dot(act, wd_ref[...], preferred_element_type=F32)
    return x1 + _rms(ff, g3_ref[...])


def _ffn_kernel(x_ref, a_ref, p_ref, s_ref, *rest):
    o_ref = rest[-1]
    o_ref[...] = _ffn_tile(x_ref[...], a_ref[...], p_ref[...], s_ref[...], *rest[:-1])


def _mix_ffn_kernel(x_ref, a_ref, u_ref, xbc_ref, z_ref, sm_ref, smt_ref, wbd_ref, psc_ref,
                    cw_ref, cb_ref, arow_ref, acol_ref, dsk_ref, gn_ref,
                    wo_ref, g1_ref, g2_ref, g3_ref, wg_ref, wu_ref, wd_ref,
                    o_ref, fin_ref, halo_sc, ext_sc, st_sc, pool_sc, ssd_sc, *, tm, q, per, nt, l_valid):
    i = pl.program_id(0)
    live = i < nt
    ia = jnp.minimum(i, nt - 1)
    first = (ia % per) == 0
    row0 = (ia % per) * tm

    @pl.when(i == 0)
    def _():
        pool_sc[...] = jnp.zeros(pool_sc.shape, pool_sc.dtype)
        ssd_sc[...] = jnp.zeros(ssd_sc.shape, ssd_sc.dtype)

    o_ref[...] = _ffn_tile(x_ref[...], a_ref[...], pool_sc[...], ssd_sc[...], wo_ref, g1_ref, g2_ref, g3_ref,
                           wg_ref, wu_ref, wd_ref)

    u = u_ref[...]
    halo = jnp.where(first, 0.0, halo_sc[...])
    pool_sc[...] = _pool_tile(u, halo, wbd_ref, psc_ref, row0, tm).astype(pool_sc.dtype)
    halo_sc[...] = u[tm - 16:]
    ext_sc[:SUBLANE, :] = jnp.where(first, 0.0, ext_sc[:SUBLANE, :])
    st_sc[...] = jnp.where(first, 0.0, st_sc[...])
    for c in range(tm // q):
        rows = slice(c * q, (c + 1) * q)
        y = _ssd_chunk(xbc_ref[rows, :], z_ref[rows, :], sm_ref[rows, :], smt_ref[:, c * q:(c + 1) * q],
                       ext_sc, st_sc, cw_ref, cb_ref, arow_ref, acol_ref, dsk_ref, gn_ref,
                       row0 + c * q, q, l_valid, live)
        ssd_sc[rows, :] = y.astype(ssd_sc.dtype)
    fin_ref[0] = st_sc[...]


def _mix_ffn(x, attn, u, xbc, z, sm, smt, w, l, nb, tp, tm, q, l_valid):
    assert q == LANE and tm % q == 0
    per = tp // tm
    nt = nb * per
    prev = lambda c: pl.BlockSpec((tm, c), lambda i: (jnp.maximum(i - 1, 0), 0))
    cur = lambda c: pl.BlockSpec((tm, c), lambda i: (jnp.minimum(i, nt - 1), 0))
    names = ("wbd", "pscale", "cw", "cb", "arow", "acol", "dsk", "gn")
    big = ("wo", "g1", "g2", "g3", "wg", "wu", "wd")
    return pl.pallas_call(
        functools.partial(_mix_ffn_kernel, tm=tm, q=q, per=per, nt=nt, l_valid=l_valid), grid=(nt + 1,),
        in_specs=[prev(D_MODEL), prev(D_FOX), cur(D_POOL), cur(D_CONV), cur(D_SSD), cur(LANE),
                  pl.BlockSpec((16, tm), lambda i: (0, jnp.minimum(i, nt - 1)))]
                 + [_layer_spec(w[n], l, 1) for n in names]
                 + [_layer_spec(w[n], l, 1, single_buffer=True) for n in big],
        out_specs=(prev(D_MODEL),
                   pl.BlockSpec((1, H // 2, LANE, LANE), lambda i: (jnp.minimum(i, nt - 1) // per, 0, 0, 0))),
        out_shape=(jax.ShapeDtypeStruct((nt * tm, D_MODEL), F32),
                   jax.ShapeDtypeStruct((nb, H // 2, LANE, LANE), F32)),
        scratch_shapes=[pltpu.VMEM((16, D_POOL), F32), pltpu.VMEM((q + SUBLANE, D_CONV), F32),
                        pltpu.VMEM((H // 2, LANE, LANE), F32),
                        pltpu.VMEM((tm, D_POOL), BF16), pltpu.VMEM((tm, D_SSD), BF16)],
        compiler_params=_cparams(("arbitrary",)), name="mix_ffn",
    )(x, attn, u, xbc, z, sm, smt, *[w[n] for n in names], *[w[n] for n in big])


def _ffn(x, attn, pool, ssd, w, l, tm):
    t = x.shape[0]
    row = lambda c: pl.BlockSpec((tm, c), lambda i: (i, 0))
    names = ("wo", "g1", "g2", "g3", "wg", "wu", "wd")
    return pl.pallas_call(
        _ffn_kernel, grid=(t // tm,),
        in_specs=[row(D_MODEL), row(D_FOX), row(D_POOL), row(D_SSD)]
                 + [_layer_spec(w[n], l, 1, single_buffer=True) for n in names],
        out_specs=row(D_MODEL), out_shape=jax.ShapeDtypeStruct((t, D_MODEL), F32),
        compiler_params=_cparams(("parallel",)), name="outproj_ffn",
    )(x, attn, pool, ssd, *[w[n] for n in names])


def _lane_pack(f_vals, dt_vals):
    pieces, at = [], 0
    for start, vals in sorted([(_DT_AT, dt_vals)] + [(c, f_vals) for c in _F_COPIES], key=lambda t: t[0]):
        pieces += [jnp.zeros(vals.shape[:-1] + (start - at,), F32), vals.astype(F32)]
        at = start + H
    pieces.append(jnp.zeros(f_vals.shape[:-1] + (LANE - at,), F32))
    return jnp.concatenate(pieces, axis=-1)


def _prep_weights(w_in, fox_f_bias, pool_w, pool_scale, conv_w, conv_b, dt_bias, a_log, d_skip, ssd_norm,
                  w_out, w_gate, w_up, w_down, ln_pre_mix, ln_post_mix, ln_pre_ffn, ln_post_ffn):
    depth = w_in.shape[0]
    f0 = 3 * D_FOX
    u0 = f0 + H
    dt0 = u0 + D_POOL + D_SSD + D_CONV
    ws = _lane_pack(w_in[:, :, f0:u0], w_in[:, :, dt0:dt0 + H])
    wm = jnp.concatenate([w_in[:, :, :D_FOX], ws, w_in[:, :, u0:dt0]], axis=2).astype(BF16)
    ws = ws.astype(BF16)
    bs = _lane_pack(fox_f_bias, dt_bias)
    wbd = jnp.concatenate([jnp.pad(pool_w[:, g].astype(F32), ((0, 0), (0, 0), (g * 64, D_POOL - (g + 1) * 64)))
                           for g in range(D_POOL // 64)], axis=1)
    alog = _lane_pack(jnp.zeros_like(a_log), a_log)
    row = lambda a: a.astype(F32).reshape(depth, 1, -1)
    tr = lambda a: jnp.transpose(a, (0, 2, 1))
    return dict(
        g_pre=row(ln_pre_mix), wm=wm, wkv=w_in[:, :, D_FOX:f0].astype(BF16),
        wkt=tr(w_in[:, :, D_FOX:2 * D_FOX]).astype(BF16), wvt=tr(w_in[:, :, 2 * D_FOX:f0]).astype(BF16),
        wst=tr(ws[:, :, :16]),
        bs=bs.reshape(depth, 1, LANE), bst=bs[:, :16].reshape(depth, 16, 1),
        wbd=wbd.astype(BF16), pscale=row(pool_scale),
        cw=jnp.pad(conv_w.astype(F32), ((0, 0), (0, SUBLANE - CONV_W), (0, 0))), cb=row(conv_b),
        arow=alog.reshape(depth, 1, LANE), acol=alog[:, :16].reshape(depth, 16, 1),
        dsk=row(jnp.repeat(d_skip, HD, axis=1)), gn=row(ssd_norm),
        wo=w_out.astype(BF16), g1=row(ln_post_mix), g2=row(ln_pre_ffn), g3=row(ln_post_ffn),
        wg=w_gate.astype(BF16), wu=w_up.astype(BF16), wd=w_down.astype(BF16))


def _mixers_and_ffn(x, w, l, attn_fn, prompt_dims, nb, nc, q, l_valid, pos0, pool_prefix, conv_prefix,
                    ssd_init, smt_fn, tm_tok, tm_pool):
    proj = _inproj(x, w, l, tm_tok, prompt_dims)
    u, z, xbc, sm, smt = proj[5:]
    attn = attn_fn(proj[:5], sm, smt)
    seq = nc * q
    if prompt_dims is not None:
        x, fin = _mix_ffn(x, attn, u, xbc, z, sm, smt, w, l, nb, seq, tm_tok, q, l_valid)
        return x, proj, fin
    pool = _pool(u, pool_prefix, w, l, nb, seq // tm_pool, tm_pool, pos0)
    ssd, fin = _ssd(xbc, z, sm, smt_fn(smt), conv_prefix, ssd_init, w, l, nb, nc, q, l_valid)
    x = _ffn(x, attn, pool, ssd, w, l, tm_tok)
    return x, proj, fin


def kernel(x_prompt, x_sample, cache_fox_k, cache_fox_v, cache_fox_logf, state_pool, state_conv, state_ssd,
           meta_tokens, ln_pre_mix, ln_post_mix, ln_pre_ffn, ln_post_ffn, w_in, fox_f_bias, pool_w, pool_scale,
           conv_w, conv_b, dt_bias, a_log, d_skip, ssd_norm, w_out, w_gate, w_up, w_down):
    nbp, seq, _ = x_prompt.shape
    nbs, ls, _ = x_sample.shape
    depth, _, past = cache_fox_logf.shape[:3]
    lp = N_META + seq
    tq = 256
    q_ssd = 128
    tp = -(-lp // tq) * tq
    pool_tile = tp // 4
    assert pool_tile % 16 == 0

    w = _prep_weights(w_in, fox_f_bias, pool_w, pool_scale, conv_w, conv_b, dt_bias, a_log, d_skip, ssd_norm,
                      w_out, w_gate, w_up, w_down, ln_pre_mix, ln_post_mix, ln_pre_ffn, ln_post_ffn)

    meta = jnp.broadcast_to(meta_tokens.astype(F32)[None], (nbp, N_META, D_MODEL))
    xp = jnp.concatenate([meta, x_prompt, jnp.zeros((nbp, tp - lp, D_MODEL), F32)], axis=1)
    xp = xp.reshape(nbp * tp, D_MODEL)
    xs = x_sample.reshape(nbs * ls, D_MODEL)

    zero_pool = jnp.zeros((nbp, 16, D_POOL), F32)
    zero_conv = jnp.zeros((nbp, 8, D_CONV), F32)
    zero_ssd = jnp.zeros((nbp, H // 2, LANE, LANE), F32)
    ssd_init = _state_to_wide(state_ssd.reshape(depth * nbs, H, HD, HD)).reshape(depth, nbs, H // 2, LANE, LANE)

    kct = jnp.transpose(cache_fox_k, (0, 1, 3, 4, 2)).reshape(depth, nbs, D_FOX, past)
    vct = jnp.transpose(cache_fox_v, (0, 1, 3, 4, 2)).reshape(depth, nbs, D_FOX, past)
    lfc = jnp.pad(jnp.transpose(cache_fox_logf.astype(F32), (0, 1, 3, 2)),
                  ((0, 0), (0, 0), (0, 8 - H), (0, 0)))
    pool_pre = jnp.pad(state_pool.astype(F32), ((0, 0), (0, 0), (16 - POOL_BUF, 0), (0, 0)))
    conv_pre = jnp.pad(state_conv.astype(F32), ((0, 0), (0, 0), (8 - (CONV_W - 1), 0), (0, 0)))

    def to_seq_major(smt):
        return jnp.transpose(smt.reshape(16, nbs, ls), (1, 0, 2))

    outs_p, outs_s = [], []
    for l in range(depth):
        def attn_p(qkv, sm, smt):
            qb, kb, _, _, vtb = qkv
            return _fox_prompt(qb, kb, _forget_bias(sm, nbp, tp), vtb, nbp, tp, tq)

        xp, proj, fin = _mixers_and_ffn(xp, w, l, attn_p, (nbp, tp, lp), nbp, tp // q_ssd, q_ssd, lp, 0,
                                        zero_pool, zero_conv, zero_ssd, lambda smt: smt, tq, pool_tile)
        outs_p.append((proj[2], proj[3], proj[9], proj[5], proj[7], fin))

        def attn_s(qkv, sm, smt):
            qb, _, kb, _, vb = qkv
            lf_all = jnp.concatenate([lfc[l], to_seq_major(smt)[:, :8], jnp.zeros((nbs, 8, LANE - ls), F32)],
                                     axis=2)
            return _fox_sample(qb, kb, vb, kct, vct, lf_all, l, nbs, ls, past)

        xs, proj, fin = _mixers_and_ffn(xs, w, l, attn_s, None, nbs, 1, ls, ls, past, pool_pre[l], conv_pre[l],
                                        ssd_init[l],
                                        lambda smt: jnp.pad(to_seq_major(smt), ((0, 0), (0, 0), (0, LANE - ls))),
                                        tq, ls)
        outs_s.append((proj[1], proj[3], proj[9], proj[5], proj[7], fin))

    def tails(outs, nb, rows, valid):
        seqv = lambda a: a.reshape(nb, rows, a.shape[-1])
        lf = jnp.stack([jnp.transpose(o[2][:H].reshape(H, nb, rows)[:, :, :valid], (1, 2, 0)) for o in outs])
        pn = jnp.stack([seqv(o[3])[:, valid - POOL_BUF:valid] for o in outs])
        cn = jnp.stack([seqv(o[4])[:, valid - (CONV_W - 1):valid] for o in outs])
        sn = jnp.stack([_state_from_wide(o[5]) for o in outs])
        return lf, pn, cn, sn

    def token_minor(outs, i):
        a = jnp.stack([o[i] for o in outs]).reshape(depth, nbp, H, HD, lp)
        return jnp.transpose(a, (0, 1, 4, 2, 3))

    def token_major(outs, i):
        return jnp.stack([o[i].reshape(nbs, ls, H, HD) for o in outs])

    y_prompt = xp.reshape(nbp, tp, D_MODEL)[:, N_META:lp]
    y_sample = xs.reshape(nbs, ls, D_MODEL)
    return ((y_prompt, y_sample, token_minor(outs_p, 0), token_minor(outs_p, 1)) + tails(outs_p, nbp, tp, lp)
            + (token_major(outs_s, 0), token_major(outs_s, 1)) + tails(outs_s, nbs, ls, ls))
```

```python
import functools
import math

import jax
import jax.numpy as jnp
from jax import lax
from jax.experimental import pallas as pl
from jax.experimental.pallas import tpu as pltpu

F32 = jnp.float32
BF16 = jnp.bfloat16

D_MODEL = 1024
N_META = 16
EPS = 1e-6
H = 6
HD = 64
D_FOX = H * HD
D_POOL = 256
POOL_BUF = 15
D_SSD = H * HD
D_BC = 128
D_CONV = D_SSD + 2 * D_BC
CONV_W = 4
D_FF = 2816
LANE = 128
SUBLANE = 8
NEG = -1e30
LOG2E = math.log2(math.e)

_Q0, _S0, _U0, _Z0, _X0, _MAIN = 0, 384, 512, 768, 1152, 1792
_MXU_N = 256
_GROUPS = ((_Q0, _U0), (_U0, _Z0), (_Z0, _X0 + LANE), (_X0 + LANE, _MAIN))
_F_AT, _DT_AT, _F_COPIES = 0, 8, (0, 16, 32)
_VROWS = HD + SUBLANE

_VMEM_LIMIT = 56 * 1024 * 1024


def _cparams(sem):
    return pltpu.CompilerParams(dimension_semantics=sem, vmem_limit_bytes=_VMEM_LIMIT)


def _layer_spec(a, l, ngrid, single_buffer=False):
    idx = (l,) + (0,) * (a.ndim - 1)
    kw = dict(pipeline_mode=pl.Buffered(1)) if single_buffer else {}
    return pl.BlockSpec((None,) + a.shape[1:], lambda *_: idx, **kw)


def _rms(x, g):
    ms = jnp.mean(x * x, axis=-1, keepdims=True)
    return x * lax.rsqrt(ms + EPS) * g


def _silu(x):
    return x * jax.nn.sigmoid(x)


def _softplus_tail(x):
    return jnp.log1p(jnp.exp(-jnp.abs(x)))


def _nt(a, b):
    return lax.dot_general(a, b, (((1,), (1,)), ((), ())), preferred_element_type=F32)


def _tn(a, b):
    return lax.dot_general(a, b, (((0,), (0,)), ((), ())), preferred_element_type=F32)


def _scan_lanes(x):
    lane = lax.broadcasted_iota(jnp.int32, x.shape, 1)
    s = 1
    while s < x.shape[1]:
        x = x + jnp.where(lane >= s, pltpu.roll(x, s, 1), 0.0)
        s *= 2
    return x


def _scan_rows(x):
    row = lax.broadcasted_iota(jnp.int32, x.shape, 0)
    s = 1
    while s < x.shape[0]:
        x = x + jnp.where(row >= s, pltpu.roll(x, s, 0), 0.0)
        s *= 2
    return x


def _inproj_kernel(x_ref, g_ref, wm_ref, wst_ref, bs_ref, bst_ref, *rest, prompt, qscale):
    hb = _rms(x_ref[...], g_ref[...]).astype(BF16)

    if prompt:
        wkt_ref, wvt_ref, _, _, qb_ref, kb_ref, kt_ref, vt_ref, vtb_ref, u_ref, z_ref, xbc_ref, sm_ref, smt_ref = rest
        kt = _nt(wkt_ref[...], hb)
        kt_ref[...] = kt
        kb_ref[...] = kt.T.astype(BF16)
        vt = _nt(wvt_ref[...], hb)
        vt_ref[...] = vt
        tm = vt.shape[1]
        ones_row = jnp.where(lax.broadcasted_iota(jnp.int32, (SUBLANE, tm), 0) == 0, 1.0, 0.0).astype(BF16)
        for h in range(H):
            vtb_ref[0, h * _VROWS:h * _VROWS + HD, :] = vt[h * HD:(h + 1) * HD].astype(BF16)
            vtb_ref[0, h * _VROWS + HD:(h + 1) * _VROWS, :] = ones_row
    else:
        wkv_ref, qb_ref, k_ref, kb_ref, v_ref, vb_ref, u_ref, z_ref, xbc_ref, sm_ref, smt_ref = rest
        kv = jnp.dot(hb, wkv_ref[...], preferred_element_type=F32)
        k_ref[...] = kv[:, :D_FOX]
        kb_ref[...] = kv[:, :D_FOX].astype(BF16)
        v_ref[...] = kv[:, D_FOX:]
        vb_ref[...] = kv[:, D_FOX:].astype(BF16)
    g0, g1, g2, g3 = (jnp.dot(hb, wm_ref[:, a:b], preferred_element_type=F32) for a, b in _GROUPS)
    qb_ref[...] = (g0[:, :_S0] * qscale).astype(BF16)
    u_ref[...] = g1
    z_ref[...] = g2[:, :_X0 - _Z0]
    xbc_ref[:, :LANE] = g2[:, _X0 - _Z0:]
    xbc_ref[:, LANE:] = g3

    sm = g0[:, _S0:] + bs_ref[...]
    lane = lax.broadcasted_iota(jnp.int32, sm.shape, 1)
    tail = _softplus_tail(sm)
    is_dt = (lane >= _DT_AT) & (lane < _DT_AT + 8)
    sm_ref[...] = jnp.where(is_dt, jnp.maximum(sm, 0.0) + tail, jnp.minimum(sm, 0.0) - tail)
    smt = _nt(wst_ref[...], hb) + bst_ref[...]
    row = lax.broadcasted_iota(jnp.int32, smt.shape, 0)
    tail = _softplus_tail(smt)
    smt_ref[...] = jnp.where(row < _DT_AT, jnp.minimum(smt, 0.0) - tail, jnp.maximum(smt, 0.0) + tail)


def _inproj(x, w, l, tm, prompt_dims=None, kv_stacks=()):
    t = x.shape[0]
    nt = t // tm
    row = lambda c: pl.BlockSpec((tm, c), lambda i: (i, 0))
    sds = jax.ShapeDtypeStruct
    tail_shapes = (sds((t, D_POOL), F32), sds((t, D_SSD), F32), sds((t, D_CONV), F32),
                   sds((t, LANE), F32), sds((16, t), F32))
    tail_specs = (row(D_POOL), row(D_SSD), row(D_CONV), row(LANE), pl.BlockSpec((16, tm), lambda i: (0, i)))
    if prompt_dims is not None:
        nb, tp, lp = prompt_dims
        per = tp // tm
        tmin = pl.BlockSpec((None, None, D_FOX, tm), lambda i: (l, i // per, 0, i % per))
        out_shape = (sds((t, D_FOX), BF16), sds((t, D_FOX), BF16), sds(kv_stacks[0].shape, F32),
                     sds(kv_stacks[1].shape, F32), sds((nt, H * _VROWS, tm), BF16)) + tail_shapes
        out_specs = (row(D_FOX), row(D_FOX), tmin, tmin,
                     pl.BlockSpec((1, H * _VROWS, tm), lambda i: (i, 0, 0))) + tail_specs
        qscale = (HD ** -0.5) * LOG2E
    else:
        out_shape = (sds((t, D_FOX), BF16), sds((t, D_FOX), F32), sds((t, D_FOX), BF16),
                     sds((t, D_FOX), F32), sds((t, D_FOX), BF16)) + tail_shapes
        out_specs = (row(D_FOX),) * 5 + tail_specs
        qscale = HD ** -0.5
    names = ("g_pre", "wm", "wst", "bs", "bst") + (("wkt", "wvt") if prompt_dims is not None else ("wkv",))
    n_in = 1 + len(names)
    return pl.pallas_call(
        functools.partial(_inproj_kernel, prompt=prompt_dims is not None, qscale=qscale), grid=(nt,),
        in_specs=[row(D_MODEL)] + [_layer_spec(w[n], l, 1) for n in names]
                 + [pl.BlockSpec(memory_space=pl.ANY)] * len(kv_stacks),
        out_specs=out_specs, out_shape=out_shape,
        input_output_aliases={n_in + s: 2 + s for s in range(len(kv_stacks))},
        compiler_params=_cparams(("parallel",)), name="inproj",
    )(x, *[w[n] for n in names], *kv_stacks)


def _forget_bias_kernel(sm_ref, e_ref, *, nblk, rows):
    lane = lax.broadcasted_iota(jnp.int32, (1, LANE), 1)
    keep = ((lane & 15) < H) & (lane < 48)
    carry = jnp.zeros((1, LANE), F32)
    for j in range(nblk):
        blk = slice(j * rows, (j + 1) * rows)
        c = _scan_rows(sm_ref[blk, :]) + carry
        carry = c[rows - 1:rows, :]
        c2 = c * LOG2E
        hi = c2.astype(BF16).astype(F32)
        mid = (c2 - hi).astype(BF16).astype(F32)
        low = (c2 - hi) - mid
        e = jnp.where(lane < 16, hi, jnp.where(lane < 32, mid, low))
        e_ref[blk, :] = jnp.where(keep, e, 0.0).astype(BF16)


def _forget_bias(sm, nb, tp):
    rows = LANE
    return pl.pallas_call(
        functools.partial(_forget_bias_kernel, nblk=tp // rows, rows=rows), grid=(nb,),
        in_specs=[pl.BlockSpec((tp, LANE), lambda b: (b, 0))],
        out_specs=pl.BlockSpec((tp, LANE), lambda b: (b, 0)),
        out_shape=jax.ShapeDtypeStruct((nb * tp, LANE), BF16),
        compiler_params=_cparams(("parallel",)), name="forget_bias",
    )(sm)


def _fox_prompt_kernel(q_ref, k_ref, e_ref, vt_ref, o_ref, qa_sc, m_sc, acc_sc, sa_sc, sb_sc, *, tq):
    i = pl.program_id(1)
    lane = lax.broadcasted_iota(jnp.int32, (1, LANE), 1)
    causal = (lax.broadcasted_iota(jnp.int32, (tq, tq), 0)
              <= lax.broadcasted_iota(jnp.int32, (tq, tq), 1))
    for h in range(H):
        q2 = q_ref[:, (h // 2) * LANE:(h // 2 + 1) * LANE]
        mine = (lane < HD) if h % 2 == 0 else (lane >= HD)
        qm = jnp.where(mine, q2, jnp.zeros_like(q2))
        pick = (lane == _F_COPIES[0] + h) | (lane == _F_COPIES[1] + h) | (lane == _F_COPIES[2] + h)
        qe = jnp.broadcast_to(jnp.where(pick, -1.0, 0.0), (tq, LANE)).astype(BF16)
        qa_sc[h] = jnp.concatenate([qm, qe], axis=1)
    m_sc[...] = jnp.full(m_sc.shape, NEG, F32)
    acc_sc[...] = jnp.zeros(acc_sc.shape, F32)

    def logits_h(j, s_sc, h):
        rows = pl.ds(pl.multiple_of(j * tq, tq), tq)
        ka = jnp.concatenate([k_ref[rows, (h // 2) * LANE:(h // 2 + 1) * LANE], e_ref[rows, :]], axis=1)
        s_sc[h] = _nt(ka, qa_sc[h])

    def softmax_pv_h(j, s_sc, masked, h):
        s = s_sc[h]
        if masked:
            s = jnp.where(causal, s, NEG)
        m_old = m_sc[h:h + 1, :]
        m_new = jnp.maximum(m_old, jnp.max(s, axis=0, keepdims=True))
        m_sc[h:h + 1, :] = m_new
        pm = jnp.exp2(s - m_new).astype(BF16)
        hs = slice(h * _VROWS, (h + 1) * _VROWS)
        pv = jnp.dot(vt_ref[j, hs, :], pm, preferred_element_type=F32)
        acc_sc[hs, :] = jnp.exp2(m_old - m_new) * acc_sc[hs, :] + pv

    def logits(j, s_sc):
        for h in range(H):
            logits_h(j, s_sc, h)

    def softmax_pv(j, s_sc, masked):
        for h in range(H):
            softmax_pv_h(j, s_sc, masked, h)

    def body(jj, carry):
        j = 2 * jj
        logits(j + 1, sb_sc)
        softmax_pv(j, sa_sc, False)
        logits(j + 2, sa_sc)
        softmax_pv(j + 1, sb_sc, False)
        return carry

    logits(0, sa_sc)
    lax.fori_loop(0, i // 2, body, 0)

    @pl.when(i % 2 == 0)
    def _():
        softmax_pv(i, sa_sc, True)

    @pl.when(i % 2 == 1)
    def _():
        logits(i, sb_sc)
        softmax_pv(i - 1, sa_sc, False)
        softmax_pv(i, sb_sc, True)

    out = jnp.concatenate([acc_sc[h * _VROWS:h * _VROWS + HD, :] / acc_sc[h * _VROWS + HD:h * _VROWS + HD + 1, :]
                           for h in range(H)], axis=0)
    o_ref[...] = out.T.astype(o_ref.dtype)


def _fox_prompt(qb, kb, eb, vt, nb, tp, tq):
    nq = tp // tq
    return pl.pallas_call(
        functools.partial(_fox_prompt_kernel, tq=tq), grid=(nb, nq),
        in_specs=[pl.BlockSpec((tq, D_FOX), lambda b, i: (b * nq + i, 0)),
                  pl.BlockSpec((tp, D_FOX), lambda b, i: (b, 0)),
                  pl.BlockSpec((tp, LANE), lambda b, i: (b, 0)),
                  pl.BlockSpec((nq, H * _VROWS, tq), lambda b, i: (b, 0, 0))],
        out_specs=pl.BlockSpec((tq, D_FOX), lambda b, i: (b * nq + i, 0)),
        out_shape=jax.ShapeDtypeStruct((nb * tp, D_FOX), BF16),
        scratch_shapes=[pltpu.VMEM((H, tq, 2 * LANE), BF16), pltpu.VMEM((8, tq), F32),
                        pltpu.VMEM((H * _VROWS, tq), F32),
                        pltpu.VMEM((H, tq, tq), F32), pltpu.VMEM((H, tq, tq), F32)],
        compiler_params=_cparams(("parallel", "arbitrary")), name="fox_prompt",
    )(qb, kb, eb, vt)


def _fox_sample_kernel(q_ref, kn_ref, vn_ref, kct_ref, vct_ref, lf_ref, o_ref, *, past, ls, nseq):
    for s in range(nseq):
        rows = slice(s * ls, (s + 1) * ls)
        _fox_sample_one(q_ref.at[rows], kn_ref.at[rows], vn_ref.at[rows], kct_ref.at[s], vct_ref.at[s],
                        lf_ref.at[s], o_ref.at[rows], past, ls)


def _fox_sample_one(q_ref, kn_ref, vn_ref, kct_ref, vct_ref, lf_ref, o_ref, past, ls):
    causal = (lax.broadcasted_iota(jnp.int32, (ls, ls), 1)
              <= lax.broadcasted_iota(jnp.int32, (ls, ls), 0))
    nblk = lf_ref.shape[1] // LANE
    loc = _scan_lanes(jnp.concatenate([lf_ref[:, j * LANE:(j + 1) * LANE] for j in range(nblk)], axis=0))
    carry = jnp.zeros((8, 1), F32)
    cs = []
    for j in range(nblk):
        blk = loc[j * 8:(j + 1) * 8]
        cs.append(blk + carry)
        carry = carry + blk[:, LANE - 1:LANE]
    c_all = jnp.concatenate(cs, axis=1)

    q = q_ref[...]
    lane = lax.broadcasted_iota(jnp.int32, (1, D_FOX), 1)
    own = [(lane >= h * HD) & (lane < (h + 1) * HD) for h in range(H)]
    q_all = jnp.concatenate([jnp.where(own[h], q, jnp.zeros_like(q)) for h in range(H)], axis=0)
    bias = [c_all[h:h + 1, past - 1:past] - c_all[h:h + 1, :] for h in range(H)]
    bias_c = jnp.concatenate([jnp.broadcast_to(b[:, :past], (ls, past)) for b in bias], axis=0)
    bias_n = jnp.concatenate([jnp.broadcast_to(b[:, past:past + ls], (ls, ls)) for b in bias], axis=0)
    s_c = jnp.dot(q_all, kct_ref[...].astype(BF16), preferred_element_type=F32) + bias_c
    s_n = jnp.where(jnp.concatenate([causal] * H, axis=0), _nt(q_all, kn_ref[...]) + bias_n, NEG)
    m = jnp.maximum(jnp.max(s_c, axis=-1, keepdims=True), jnp.max(s_n, axis=-1, keepdims=True))
    p_c = jnp.exp(s_c - m)
    p_n = jnp.exp(s_n - m)
    den = jnp.sum(p_c, axis=-1, keepdims=True) + jnp.sum(p_n, axis=-1, keepdims=True)
    o_all = (_nt(p_c.astype(BF16), vct_ref[...].astype(BF16))
             + jnp.dot(p_n.astype(BF16), vn_ref[...], preferred_element_type=F32)) / den
    o = jnp.zeros((ls, D_FOX), F32)
    for h in range(H):
        o = jnp.where(own[h], o_all[h * ls:(h + 1) * ls], o)
    o_ref[...] = o.astype(o_ref.dtype)


def _fox_sample(qb, kb, vb, kct, vct, lf_all, l, nb, ls, past):
    nseq = 4
    new = pl.BlockSpec((nseq * ls, D_FOX), lambda b: (b, 0))
    cache = pl.BlockSpec((None, nseq, D_FOX, past), lambda b: (l, b, 0, 0))
    return pl.pallas_call(
        functools.partial(_fox_sample_kernel, past=past, ls=ls, nseq=nseq), grid=(nb // nseq,),
        in_specs=[new, new, new, cache, cache, pl.BlockSpec((nseq, 8, past + LANE), lambda b: (b, 0, 0))],
        out_specs=new, out_shape=jax.ShapeDtypeStruct((nb * ls, D_FOX), BF16),
        compiler_params=_cparams(("parallel",)), name="fox_sample",
    )(qb, kb, vb, kct, vct, lf_all)


def _pool_tile(u, halo, w_ref, sc_ref, pos_start, tm):
    a = jnp.concatenate([halo, u], axis=0)
    e1 = a + pltpu.roll(a, 1, 0)
    e2 = e1 + pltpu.roll(e1, 2, 0)
    e3 = e2 + pltpu.roll(e2, 4, 0)
    e4 = e3 + pltpu.roll(e3, 8, 0)
    lane = lax.broadcasted_iota(jnp.int32, (1, D_POOL), 1)
    win = jnp.where(lane < 64, e1, jnp.where(lane < 128, e2, jnp.where(lane < 192, e3, e4)))[16:]
    wsz = jnp.where(lane < 64, 2.0, jnp.where(lane < 128, 4.0, jnp.where(lane < 192, 8.0, 16.0)))
    pos = (pos_start + lax.broadcasted_iota(jnp.int32, (tm, 1), 0)).astype(F32)
    diff = win / jnp.minimum(pos + 1.0, wsz) - u
    return jnp.dot(diff.astype(BF16), w_ref[...], preferred_element_type=F32) * sc_ref[...]


def _pool_kernel(u_ref, pre_ref, w_ref, sc_ref, o_ref, halo_sc, *, tm, pos0):
    t = pl.program_id(1)

    @pl.when(t == 0)
    def _():
        halo_sc[...] = pre_ref[0]

    u = u_ref[...]
    o_ref[...] = _pool_tile(u, halo_sc[...], w_ref, sc_ref, pos0 + t * tm, tm).astype(o_ref.dtype)
    halo_sc[...] = u[tm - 16:]


def _pool(u, prefix, w, l, nb, nt, tm, pos0):
    return pl.pallas_call(
        functools.partial(_pool_kernel, tm=tm, pos0=pos0), grid=(nb, nt),
        in_specs=[pl.BlockSpec((tm, D_POOL), lambda b, t: (b * nt + t, 0)),
                  pl.BlockSpec((1, 16, D_POOL), lambda b, t: (b, 0, 0)),
                  _layer_spec(w["wbd"], l, 2), _layer_spec(w["pscale"], l, 2)],
        out_specs=pl.BlockSpec((tm, D_POOL), lambda b, t: (b * nt + t, 0)),
        out_shape=jax.ShapeDtypeStruct((nb * nt * tm, D_POOL), BF16),
        scratch_shapes=[pltpu.VMEM((16, D_POOL), F32)],
        compiler_params=_cparams(("parallel", "arbitrary")), name="pool_mixer",
    )(u, prefix, w["wbd"], w["pscale"])


def _ssd_chunk(xbc, z, sm, smt, ext_sc, st_sc, cw_ref, cb_ref, arow_ref, acol_ref, dsk_ref, gn_ref,
               row0, q, l_valid, live=None):
    ext_sc[SUBLANE:, :] = xbc
    w = cw_ref[...]
    conv = cb_ref[...]
    for j in range(CONV_W):
        conv = conv + w[j:j + 1] * ext_sc[pl.ds(SUBLANE - (CONV_W - 1) + j, q), :]
    ext_sc[:SUBLANE, :] = ext_sc[q:, :]
    act = _silu(conv)
    xs = act[:, :D_SSD]
    bbf = act[:, D_SSD:D_SSD + D_BC]
    bb = bbf.astype(BF16)
    cc = act[:, D_SSD + D_BC:].astype(BF16)
    bbt = bbf.T

    rvalid = (row0 + lax.broadcasted_iota(jnp.int32, (q, 1), 0)) < l_valid
    dtc = jnp.where(rvalid, sm, 0.0)
    acs_c = _scan_rows(dtc * (-jnp.exp(arow_ref[...])))
    cvalid = (row0 + lax.broadcasted_iota(jnp.int32, (1, LANE), 1)) < l_valid
    dtt = jnp.where(cvalid, smt, 0.0)
    acs_t = _scan_lanes(dtt * (-jnp.exp(acol_ref[...])))

    lane = lax.broadcasted_iota(jnp.int32, (1, LANE), 1)
    lo = lane < HD
    n_lo = lax.broadcasted_iota(jnp.int32, (LANE, 1), 0) < HD
    causal = (lax.broadcasted_iota(jnp.int32, (q, q), 1)
              <= lax.broadcasted_iota(jnp.int32, (q, q), 0))
    zc = jnp.zeros_like(cc)
    cb = [_nt(jnp.where(lo, cc, zc), bb), _nt(jnp.where(lo, zc, cc), bb)]
    ys = []
    for p in range(H // 2):
        cols = slice(p * LANE, (p + 1) * LANE)
        xp = xs[:, cols]
        xpb = xp.astype(BF16)
        st = st_sc[p]
        y_in = jnp.dot(cc, st.astype(BF16), preferred_element_type=F32)
        y_h, upd, dec = [], [], []
        for hh in range(2):
            h = 2 * p + hh
            g = h // (H // 2)
            arep = jnp.broadcast_to(acs_c[:, _DT_AT + h:_DT_AT + h + 1], (q, LANE))
            ak = acs_t[_DT_AT + h:_DT_AT + h + 1, :q]
            dtr = dtt[_DT_AT + h:_DT_AT + h + 1, :q]
            lm = jnp.exp(jnp.where(causal, arep[:, :q] - ak, NEG))
            gm = (cb[g] * lm * dtr).astype(BF16)
            y_h.append(jnp.dot(gm, xpb, preferred_element_type=F32) + y_in * jnp.exp(arep))
            alast = acs_t[_DT_AT + h:_DT_AT + h + 1, q - 1:q]
            bw = (bbt * (jnp.exp(alast - ak) * dtr)).astype(BF16)
            keep = (lo if hh == 0 else ~lo) & (n_lo if g == 0 else ~n_lo)
            upd.append(jnp.where(keep, jnp.dot(bw, xpb, preferred_element_type=F32), 0.0))
            dec.append(jnp.exp(alast))
        st_new = jnp.where(lo, dec[0], dec[1]) * st + upd[0] + upd[1]
        st_sc[p] = st_new if live is None else jnp.where(live, st_new, st)
        ys.append(jnp.where(lo, y_h[0], y_h[1]) + dsk_ref[:, cols] * xp)
    yc = jnp.concatenate(ys, axis=1) * _silu(z)
    return _rms(yc, gn_ref[...])


def _ssd_kernel(xbc_ref, z_ref, sm_ref, smt_ref, pre_ref, init_ref, cw_ref, cb_ref, arow_ref,
                acol_ref, dsk_ref, gn_ref, y_ref, fin_ref, ext_sc, st_sc, *, q, l_valid):
    c = pl.program_id(1)

    @pl.when(c == 0)
    def _():
        ext_sc[:SUBLANE, :] = pre_ref[0]
        st_sc[...] = init_ref[0]

    y = _ssd_chunk(xbc_ref[...], z_ref[...], sm_ref[...], smt_ref[...], ext_sc, st_sc, cw_ref, cb_ref,
                   arow_ref, acol_ref, dsk_ref, gn_ref, c * q, q, l_valid)
    y_ref[...] = y.astype(y_ref.dtype)

    @pl.when(c == pl.num_programs(1) - 1)
    def _():
        fin_ref[0] = st_sc[...]


def _ssd(xbc, z, sm, smt, prefix, init, w, l, nb, nc, q, l_valid):
    rows = lambda c: pl.BlockSpec((q, c), lambda b, i: (b * nc + i, 0))
    if smt.ndim == 2:
        smt_spec = pl.BlockSpec((16, LANE), lambda b, i: (0, b * nc + i))
    else:
        smt_spec = pl.BlockSpec((None, 16, LANE), lambda b, i: (b, 0, 0))
    names = ("cw", "cb", "arow", "acol", "dsk", "gn")
    state = pl.BlockSpec((1, H // 2, LANE, LANE), lambda b, i: (b, 0, 0, 0))
    return pl.pallas_call(
        functools.partial(_ssd_kernel, q=q, l_valid=l_valid), grid=(nb, nc),
        in_specs=[rows(D_CONV), rows(D_SSD), rows(LANE), smt_spec,
                  pl.BlockSpec((1, 8, D_CONV), lambda b, i: (b, 0, 0)),
                  state]
                 + [_layer_spec(w[n], l, 2) for n in names],
        out_specs=(rows(D_SSD), state),
        out_shape=(jax.ShapeDtypeStruct((nb * nc * q, D_SSD), BF16),
                   jax.ShapeDtypeStruct((nb, H // 2, LANE, LANE), F32)),
        scratch_shapes=[pltpu.VMEM((q + SUBLANE, D_CONV), F32), pltpu.VMEM((H // 2, LANE, LANE), F32)],
        compiler_params=_cparams(("parallel", "arbitrary")), name="conv_ssd",
    )(xbc, z, sm, smt, prefix, init, *[w[n] for n in names])


def _state_to_wide(s):
    def placed(h):
        g, hh = h // (H // 2), h % 2
        return jnp.pad(jnp.swapaxes(s[:, h].astype(F32), 1, 2),
                       ((0, 0), (g * HD, LANE - (g + 1) * HD), (hh * HD, LANE - (hh + 1) * HD)))
    return jnp.stack([placed(2 * p) + placed(2 * p + 1) for p in range(H // 2)], axis=1)


def _state_from_wide(wide):
    return jnp.stack([jnp.swapaxes(wide[:, h // 2, (h // (H // 2)) * HD:(h // (H // 2) + 1) * HD,
                                        (h % 2) * HD:(h % 2 + 1) * HD], 1, 2) for h in range(H)], axis=1)


def _ffn_tile(x, a, p, s, wo_ref, g1_ref, g2_ref, g3_ref, wg_ref, wu_ref, wd_ref):
    mp = (jnp.dot(a, wo_ref[:D_FOX, :], preferred_element_type=F32)
          + jnp.dot(p, wo_ref[D_FOX:D_FOX + D_POOL, :], preferred_element_type=F32)
          + jnp.dot(s, wo_ref[D_FOX + D_POOL:, :], preferred_element_type=F32))
    x1 = x + _rms(mp, g1_ref[...])
    hb = _rms(x1, g2_ref[...]).astype(BF16)
    gate = jnp.dot(hb, wg_ref[...], preferred_element_type=F32)
    up = jnp.dot(hb, wu_ref[...], preferred_element_type=F32)
    act = (_silu(gate) * up).astype(BF16)
    ff = jnp.dot(act, wd_ref[...], preferred_element_type=F32)
    return x1 + _rms(ff, g3_ref[...])


def _ffn_kernel(x_ref, a_ref, p_ref, s_ref, *rest):
    o_ref = rest[-1]
    o_ref[...] = _ffn_tile(x_ref[...], a_ref[...], p_ref[...], s_ref[...], *rest[:-1])


def _mix_ffn_kernel(x_ref, a_ref, u_ref, xbc_ref, z_ref, sm_ref, smt_ref, wbd_ref, psc_ref,
                    cw_ref, cb_ref, arow_ref, acol_ref, dsk_ref, gn_ref,
                    wo_ref, g1_ref, g2_ref, g3_ref, wg_ref, wu_ref, wd_ref,
                    o_ref, fin_ref, halo_sc, ext_sc, st_sc, pool_sc, ssd_sc, *, tm, q, per, nt, l_valid):
    i = pl.program_id(0)
    live = i < nt
    ia = jnp.minimum(i, nt - 1)
    first = (ia % per) == 0
    row0 = (ia % per) * tm

    @pl.when(i == 0)
    def _():
        pool_sc[...] = jnp.zeros(pool_sc.shape, pool_sc.dtype)
        ssd_sc[...] = jnp.zeros(ssd_sc.shape, ssd_sc.dtype)

    o_ref[...] = _ffn_tile(x_ref[...], a_ref[...], pool_sc[...], ssd_sc[...], wo_ref, g1_ref, g2_ref, g3_ref,
                           wg_ref, wu_ref, wd_ref)

    u = u_ref[...]
    halo = jnp.where(first, 0.0, halo_sc[...])
    pool_sc[...] = _pool_tile(u, halo, wbd_ref, psc_ref, row0, tm).astype(pool_sc.dtype)
    halo_sc[...] = u[tm - 16:]
    ext_sc[:SUBLANE, :] = jnp.where(first, 0.0, ext_sc[:SUBLANE, :])
    st_sc[...] = jnp.where(first, 0.0, st_sc[...])
    for c in range(tm // q):
        rows = slice(c * q, (c + 1) * q)
        y = _ssd_chunk(xbc_ref[rows, :], z_ref[rows, :], sm_ref[rows, :], smt_ref[:, c * q:(c + 1) * q],
                       ext_sc, st_sc, cw_ref, cb_ref, arow_ref, acol_ref, dsk_ref, gn_ref,
                       row0 + c * q, q, l_valid, live)
        ssd_sc[rows, :] = y.astype(ssd_sc.dtype)
    fin_ref[0] = st_sc[...]


def _mix_ffn(x, attn, u, xbc, z, sm, smt, w, l, nb, tp, tm, q, l_valid):
    assert q == LANE and tm % q == 0
    per = tp // tm
    nt = nb * per
    prev = lambda c: pl.BlockSpec((tm, c), lambda i: (jnp.maximum(i - 1, 0), 0))
    cur = lambda c: pl.BlockSpec((tm, c), lambda i: (jnp.minimum(i, nt - 1), 0))
    names = ("wbd", "pscale", "cw", "cb", "arow", "acol", "dsk", "gn")
    big = ("wo", "g1", "g2", "g3", "wg", "wu", "wd")
    return pl.pallas_call(
        functools.partial(_mix_ffn_kernel, tm=tm, q=q, per=per, nt=nt, l_valid=l_valid), grid=(nt + 1,),
        in_specs=[prev(D_MODEL), prev(D_FOX), cur(D_POOL), cur(D_CONV), cur(D_SSD), cur(LANE),
                  pl.BlockSpec((16, tm), lambda i: (0, jnp.minimum(i, nt - 1)))]
                 + [_layer_spec(w[n], l, 1) for n in names]
                 + [_layer_spec(w[n], l, 1, single_buffer=True) for n in big],
        out_specs=(prev(D_MODEL),
                   pl.BlockSpec((1, H // 2, LANE, LANE), lambda i: (jnp.minimum(i, nt - 1) // per, 0, 0, 0))),
        out_shape=(jax.ShapeDtypeStruct((nt * tm, D_MODEL), F32),
                   jax.ShapeDtypeStruct((nb, H // 2, LANE, LANE), F32)),
        scratch_shapes=[pltpu.VMEM((16, D_POOL), F32), pltpu.VMEM((q + SUBLANE, D_CONV), F32),
                        pltpu.VMEM((H // 2, LANE, LANE), F32),
                        pltpu.VMEM((tm, D_POOL), BF16), pltpu.VMEM((tm, D_SSD), BF16)],
        compiler_params=_cparams(("arbitrary",)), name="mix_ffn",
    )(x, attn, u, xbc, z, sm, smt, *[w[n] for n in names], *[w[n] for n in big])


def _ffn(x, attn, pool, ssd, w, l, tm):
    t = x.shape[0]
    row = lambda c: pl.BlockSpec((tm, c), lambda i: (i, 0))
    names = ("wo", "g1", "g2", "g3", "wg", "wu", "wd")
    return pl.pallas_call(
        _ffn_kernel, grid=(t // tm,),
        in_specs=[row(D_MODEL), row(D_FOX), row(D_POOL), row(D_SSD)]
                 + [_layer_spec(w[n], l, 1, single_buffer=True) for n in names],
        out_specs=row(D_MODEL), out_shape=jax.ShapeDtypeStruct((t, D_MODEL), F32),
        compiler_params=_cparams(("parallel",)), name="outproj_ffn",
    )(x, attn, pool, ssd, *[w[n] for n in names])


def _lane_pack(f_vals, dt_vals):
    pieces, at = [], 0
    for start, vals in sorted([(_DT_AT, dt_vals)] + [(c, f_vals) for c in _F_COPIES], key=lambda t: t[0]):
        pieces += [jnp.zeros(vals.shape[:-1] + (start - at,), F32), vals.astype(F32)]
        at = start + H
    pieces.append(jnp.zeros(f_vals.shape[:-1] + (LANE - at,), F32))
    return jnp.concatenate(pieces, axis=-1)


def _prep_weights(w_in, fox_f_bias, pool_w, pool_scale, conv_w, conv_b, dt_bias, a_log, d_skip, ssd_norm,
                  w_out, w_gate, w_up, w_down, ln_pre_mix, ln_post_mix, ln_pre_ffn, ln_post_ffn):
    depth = w_in.shape[0]
    f0 = 3 * D_FOX
    u0 = f0 + H
    dt0 = u0 + D_POOL + D_SSD + D_CONV
    ws = _lane_pack(w_in[:, :, f0:u0], w_in[:, :, dt0:dt0 + H])
    wm = jnp.concatenate([w_in[:, :, :D_FOX], ws, w_in[:, :, u0:dt0]], axis=2).astype(BF16)
    ws = ws.astype(BF16)
    bs = _lane_pack(fox_f_bias, dt_bias)
    wbd = jnp.concatenate([jnp.pad(pool_w[:, g].astype(F32), ((0, 0), (0, 0), (g * 64, D_POOL - (g + 1) * 64)))
                           for g in range(D_POOL // 64)], axis=1)
    alog = _lane_pack(jnp.zeros_like(a_log), a_log)
    row = lambda a: a.astype(F32).reshape(depth, 1, -1)
    tr = lambda a: jnp.transpose(a, (0, 2, 1))
    return dict(
        g_pre=row(ln_pre_mix), wm=wm, wkv=w_in[:, :, D_FOX:f0].astype(BF16),
        wkt=tr(w_in[:, :, D_FOX:2 * D_FOX]).astype(BF16), wvt=tr(w_in[:, :, 2 * D_FOX:f0]).astype(BF16),
        wst=tr(ws[:, :, :16]),
        bs=bs.reshape(depth, 1, LANE), bst=bs[:, :16].reshape(depth, 16, 1),
        wbd=wbd.astype(BF16), pscale=row(pool_scale),
        cw=jnp.pad(conv_w.astype(F32), ((0, 0), (0, SUBLANE - CONV_W), (0, 0))), cb=row(conv_b),
        arow=alog.reshape(depth, 1, LANE), acol=alog[:, :16].reshape(depth, 16, 1),
        dsk=row(jnp.repeat(d_skip, HD, axis=1)), gn=row(ssd_norm),
        wo=w_out.astype(BF16), g1=row(ln_post_mix), g2=row(ln_pre_ffn), g3=row(ln_post_ffn),
        wg=w_gate.astype(BF16), wu=w_up.astype(BF16), wd=w_down.astype(BF16))


def _mixers_and_ffn(x, w, l, attn_fn, prompt_dims, nb, nc, q, l_valid, pos0, pool_prefix, conv_prefix,
                    ssd_init, smt_fn, tm_tok, tm_pool, kv_stacks=()):
    proj = _inproj(x, w, l, tm_tok, prompt_dims, kv_stacks)
    u, z, xbc, sm, smt = proj[5:]
    attn = attn_fn(proj[:5], sm, smt)
    seq = nc * q
    if prompt_dims is not None:
        x, fin = _mix_ffn(x, attn, u, xbc, z, sm, smt, w, l, nb, seq, tm_tok, q, l_valid)
        return x, proj, fin
    pool = _pool(u, pool_prefix, w, l, nb, seq // tm_pool, tm_pool, pos0)
    ssd, fin = _ssd(xbc, z, sm, smt_fn(smt), conv_prefix, ssd_init, w, l, nb, nc, q, l_valid)
    x = _ffn(x, attn, pool, ssd, w, l, tm_tok)
    return x, proj, fin


def kernel(x_prompt, x_sample, cache_fox_k, cache_fox_v, cache_fox_logf, state_pool, state_conv, state_ssd,
           meta_tokens, ln_pre_mix, ln_post_mix, ln_pre_ffn, ln_post_ffn, w_in, fox_f_bias, pool_w, pool_scale,
           conv_w, conv_b, dt_bias, a_log, d_skip, ssd_norm, w_out, w_gate, w_up, w_down):
    nbp, seq, _ = x_prompt.shape
    nbs, ls, _ = x_sample.shape
    depth, _, past = cache_fox_logf.shape[:3]
    lp = N_META + seq
    tq = 256
    q_ssd = 128
    tp = -(-lp // tq) * tq
    pool_tile = tp // 4
    assert pool_tile % 16 == 0

    w = _prep_weights(w_in, fox_f_bias, pool_w, pool_scale, conv_w, conv_b, dt_bias, a_log, d_skip, ssd_norm,
                      w_out, w_gate, w_up, w_down, ln_pre_mix, ln_post_mix, ln_pre_ffn, ln_post_ffn)

    meta = jnp.broadcast_to(meta_tokens.astype(F32)[None], (nbp, N_META, D_MODEL))
    xp = jnp.concatenate([meta, x_prompt, jnp.zeros((nbp, tp - lp, D_MODEL), F32)], axis=1)
    xp = xp.reshape(nbp * tp, D_MODEL)
    xs = x_sample.reshape(nbs * ls, D_MODEL)

    zero_pool = jnp.zeros((nbp, 16, D_POOL), F32)
    zero_conv = jnp.zeros((nbp, 8, D_CONV), F32)
    zero_ssd = jnp.zeros((nbp, H // 2, LANE, LANE), F32)
    ssd_init = _state_to_wide(state_ssd.reshape(depth * nbs, H, HD, HD)).reshape(depth, nbs, H // 2, LANE, LANE)

    kct = jnp.transpose(cache_fox_k, (0, 1, 3, 4, 2)).reshape(depth, nbs, D_FOX, past)
    vct = jnp.transpose(cache_fox_v, (0, 1, 3, 4, 2)).reshape(depth, nbs, D_FOX, past)
    lfc = jnp.pad(jnp.transpose(cache_fox_logf.astype(F32), (0, 1, 3, 2)),
                  ((0, 0), (0, 0), (0, 8 - H), (0, 0)))
    pool_pre = jnp.pad(state_pool.astype(F32), ((0, 0), (0, 0), (16 - POOL_BUF, 0), (0, 0)))
    conv_pre = jnp.pad(state_conv.astype(F32), ((0, 0), (0, 0), (8 - (CONV_W - 1), 0), (0, 0)))

    def to_seq_major(smt):
        return jnp.transpose(smt.reshape(16, nbs, ls), (1, 0, 2))

    outs_p, outs_s = [], []
    kv_stacks = tuple(jnp.zeros((depth, nbp, D_FOX, lp), F32) for _ in range(2))
    for l in range(depth):
        def attn_p(qkv, sm, smt):
            qb, kb, _, _, vtb = qkv
            return _fox_prompt(qb, kb, _forget_bias(sm, nbp, tp), vtb, nbp, tp, tq)

        xp, proj, fin = _mixers_and_ffn(xp, w, l, attn_p, (nbp, tp, lp), nbp, tp // q_ssd, q_ssd, lp, 0,
                                        zero_pool, zero_conv, zero_ssd, lambda smt: smt, tq, pool_tile, kv_stacks)
        kv_stacks = (proj[2], proj[3])
        outs_p.append((None, None, proj[9], proj[5], proj[7], fin))

        def attn_s(qkv, sm, smt):
            qb, _, kb, _, vb = qkv
            lf_all = jnp.concatenate([lfc[l], to_seq_major(smt)[:, :8], jnp.zeros((nbs, 8, LANE - ls), F32)],
                                     axis=2)
            return _fox_sample(qb, kb, vb, kct, vct, lf_all, l, nbs, ls, past)

        xs, proj, fin = _mixers_and_ffn(xs, w, l, attn_s, None, nbs, 1, ls, ls, past, pool_pre[l], conv_pre[l],
                                        ssd_init[l],
                                        lambda smt: jnp.pad(to_seq_major(smt), ((0, 0), (0, 0), (0, LANE - ls))),
                                        tq, ls)
        outs_s.append((proj[1], proj[3], proj[9], proj[5], proj[7], fin))

    def tails(outs, nb, rows, valid):
        seqv = lambda a: a.reshape(nb, rows, a.shape[-1])
        lf = jnp.stack([jnp.transpose(o[2][:H].reshape(H, nb, rows)[:, :, :valid], (1, 2, 0)) for o in outs])
        pn = jnp.stack([seqv(o[3])[:, valid - POOL_BUF:valid] for o in outs])
        cn = jnp.stack([seqv(o[4])[:, valid - (CONV_W - 1):valid] for o in outs])
        sn = jnp.stack([_state_from_wide(o[5]) for o in outs])
        return lf, pn, cn, sn

    def token_minor(a):
        return jnp.transpose(a.reshape(depth, nbp, H, HD, lp), (0, 1, 4, 2, 3))

    def token_major(outs, i):
        return jnp.stack([o[i].reshape(nbs, ls, H, HD) for o in outs])

    y_prompt = xp.reshape(nbp, tp, D_MODEL)[:, N_META:lp]
    y_sample = xs.reshape(nbs, ls, D_MODEL)
    return ((y_prompt, y_sample, token_minor(kv_stacks[0]), token_minor(kv_stacks[1])) + tails(outs_p, nbp, tp, lp)
            + (token_major(outs_s, 0), token_major(outs_s, 1)) + tails(outs_s, nbs, ls, ls))
```

```python
import functools
import math

import jax
import jax.numpy as jnp
from jax import lax
from jax.experimental import pallas as pl
from jax.experimental.pallas import tpu as pltpu

F32 = jnp.float32
BF16 = jnp.bfloat16

D_MODEL = 1024
N_META = 16
EPS = 1e-6
H = 6
HD = 64
D_FOX = H * HD
D_POOL = 256
POOL_BUF = 15
D_SSD = H * HD
D_BC = 128
D_CONV = D_SSD + 2 * D_BC
CONV_W = 4
D_FF = 2816
LANE = 128
SUBLANE = 8
NEG = -1e30
LOG2E = math.log2(math.e)

_Q0, _S0, _U0, _Z0, _X0, _MAIN = 0, 384, 512, 768, 1152, 1792
_MXU_N = 256
_GROUPS = ((_Q0, _U0), (_U0, _Z0), (_Z0, _X0 + LANE), (_X0 + LANE, _MAIN))
_F_AT, _DT_AT, _F_COPIES = 0, 8, (0, 16, 32)
_VROWS = HD + SUBLANE

_VMEM_LIMIT = 56 * 1024 * 1024


def _cparams(sem):
    return pltpu.CompilerParams(dimension_semantics=sem, vmem_limit_bytes=_VMEM_LIMIT)


def _layer_spec(a, l, ngrid, single_buffer=False):
    idx = (l,) + (0,) * (a.ndim - 1)
    kw = dict(pipeline_mode=pl.Buffered(1)) if single_buffer else {}
    return pl.BlockSpec((None,) + a.shape[1:], lambda *_: idx, **kw)


def _rms(x, g):
    ms = jnp.mean(x * x, axis=-1, keepdims=True)
    return x * lax.rsqrt(ms + EPS) * g


def _silu(x):
    return x * jax.nn.sigmoid(x)


def _softplus_tail(x):
    return jnp.log1p(jnp.exp(-jnp.abs(x)))


def _nt(a, b):
    return lax.dot_general(a, b, (((1,), (1,)), ((), ())), preferred_element_type=F32)


def _tn(a, b):
    return lax.dot_general(a, b, (((0,), (0,)), ((), ())), preferred_element_type=F32)


def _scan_lanes(x):
    lane = lax.broadcasted_iota(jnp.int32, x.shape, 1)
    s = 1
    while s < x.shape[1]:
        x = x + jnp.where(lane >= s, pltpu.roll(x, s, 1), 0.0)
        s *= 2
    return x


def _scan_rows(x):
    row = lax.broadcasted_iota(jnp.int32, x.shape, 0)
    s = 1
    while s < x.shape[0]:
        x = x + jnp.where(row >= s, pltpu.roll(x, s, 0), 0.0)
        s *= 2
    return x


def _inproj_kernel(x_ref, g_ref, wm_ref, wst_ref, bs_ref, bst_ref, *rest, prompt, qscale):
    hb = _rms(x_ref[...], g_ref[...]).astype(BF16)
    hbt = hb.T

    if prompt:
        wkt_ref, wvt_ref, _, _, qb_ref, kb_ref, kt_ref, vt_ref, vtb_ref, u_ref, z_ref, xbc_ref, sm_ref, smt_ref = rest
        kt = jnp.dot(wkt_ref[...], hbt, preferred_element_type=F32)
        kt_ref[...] = kt
        kb_ref[...] = kt.T.astype(BF16)
        vt = jnp.dot(wvt_ref[...], hbt, preferred_element_type=F32)
        vt_ref[...] = vt
        tm = vt.shape[1]
        ones_row = jnp.where(lax.broadcasted_iota(jnp.int32, (SUBLANE, tm), 0) == 0, 1.0, 0.0).astype(BF16)
        for h in range(H):
            vtb_ref[0, h * _VROWS:h * _VROWS + HD, :] = vt[h * HD:(h + 1) * HD].astype(BF16)
            vtb_ref[0, h * _VROWS + HD:(h + 1) * _VROWS, :] = ones_row
    else:
        wkv_ref, qb_ref, k_ref, kb_ref, v_ref, vb_ref, u_ref, z_ref, xbc_ref, sm_ref, smt_ref = rest
        kv = jnp.dot(hb, wkv_ref[...], preferred_element_type=F32)
        k_ref[...] = kv[:, :D_FOX]
        kb_ref[...] = kv[:, :D_FOX].astype(BF16)
        v_ref[...] = kv[:, D_FOX:]
        vb_ref[...] = kv[:, D_FOX:].astype(BF16)
    g0, g1, g2, g3 = (jnp.dot(hb, wm_ref[:, a:b], preferred_element_type=F32) for a, b in _GROUPS)
    qb_ref[...] = (g0[:, :_S0] * qscale).astype(BF16)
    u_ref[...] = g1
    z_ref[...] = g2[:, :_X0 - _Z0]
    xbc_ref[:, :LANE] = g2[:, _X0 - _Z0:]
    xbc_ref[:, LANE:] = g3

    sm = g0[:, _S0:] + bs_ref[...]
    lane = lax.broadcasted_iota(jnp.int32, sm.shape, 1)
    tail = _softplus_tail(sm)
    is_dt = (lane >= _DT_AT) & (lane < _DT_AT + 8)
    sm_ref[...] = jnp.where(is_dt, jnp.maximum(sm, 0.0) + tail, jnp.minimum(sm, 0.0) - tail)
    smt = jnp.dot(wst_ref[...], hbt, preferred_element_type=F32) + bst_ref[...]
    row = lax.broadcasted_iota(jnp.int32, smt.shape, 0)
    tail = _softplus_tail(smt)
    smt_ref[...] = jnp.where(row < _DT_AT, jnp.minimum(smt, 0.0) - tail, jnp.maximum(smt, 0.0) + tail)


def _inproj(x, w, l, tm, prompt_dims=None, kv_stacks=()):
    t = x.shape[0]
    nt = t // tm
    row = lambda c: pl.BlockSpec((tm, c), lambda i: (i, 0))
    sds = jax.ShapeDtypeStruct
    tail_shapes = (sds((t, D_POOL), F32), sds((t, D_SSD), F32), sds((t, D_CONV), F32),
                   sds((t, LANE), F32), sds((16, t), F32))
    tail_specs = (row(D_POOL), row(D_SSD), row(D_CONV), row(LANE), pl.BlockSpec((16, tm), lambda i: (0, i)))
    if prompt_dims is not None:
        nb, tp, lp = prompt_dims
        per = tp // tm
        tmin = pl.BlockSpec((None, None, D_FOX, tm), lambda i: (l, i // per, 0, i % per))
        out_shape = (sds((t, D_FOX), BF16), sds((t, D_FOX), BF16), sds(kv_stacks[0].shape, F32),
                     sds(kv_stacks[1].shape, F32), sds((nt, H * _VROWS, tm), BF16)) + tail_shapes
        out_specs = (row(D_FOX), row(D_FOX), tmin, tmin,
                     pl.BlockSpec((1, H * _VROWS, tm), lambda i: (i, 0, 0))) + tail_specs
        qscale = (HD ** -0.5) * LOG2E
    else:
        out_shape = (sds((t, D_FOX), BF16), sds((t, D_FOX), F32), sds((t, D_FOX), BF16),
                     sds((t, D_FOX), F32), sds((t, D_FOX), BF16)) + tail_shapes
        out_specs = (row(D_FOX),) * 5 + tail_specs
        qscale = HD ** -0.5
    names = ("g_pre", "wm", "wst", "bs", "bst") + (("wkt", "wvt") if prompt_dims is not None else ("wkv",))
    n_in = 1 + len(names)
    return pl.pallas_call(
        functools.partial(_inproj_kernel, prompt=prompt_dims is not None, qscale=qscale), grid=(nt,),
        in_specs=[row(D_MODEL)] + [_layer_spec(w[n], l, 1) for n in names]
                 + [pl.BlockSpec(memory_space=pl.ANY)] * len(kv_stacks),
        out_specs=out_specs, out_shape=out_shape,
        input_output_aliases={n_in + s: 2 + s for s in range(len(kv_stacks))},
        compiler_params=_cparams(("parallel",)), name="inproj",
    )(x, *[w[n] for n in names], *kv_stacks)


def _forget_bias_kernel(sm_ref, e_ref, *, nblk, rows):
    lane = lax.broadcasted_iota(jnp.int32, (1, LANE), 1)
    keep = ((lane & 15) < H) & (lane < 48)
    carry = jnp.zeros((1, LANE), F32)
    for j in range(nblk):
        blk = slice(j * rows, (j + 1) * rows)
        c = _scan_rows(sm_ref[blk, :]) + carry
        carry = c[rows - 1:rows, :]
        c2 = c * LOG2E
        hi = c2.astype(BF16).astype(F32)
        mid = (c2 - hi).astype(BF16).astype(F32)
        low = (c2 - hi) - mid
        e = jnp.where(lane < 16, hi, jnp.where(lane < 32, mid, low))
        e_ref[blk, :] = jnp.where(keep, e, 0.0).astype(BF16)


def _forget_bias(sm, nb, tp):
    rows = LANE
    return pl.pallas_call(
        functools.partial(_forget_bias_kernel, nblk=tp // rows, rows=rows), grid=(nb,),
        in_specs=[pl.BlockSpec((tp, LANE), lambda b: (b, 0))],
        out_specs=pl.BlockSpec((tp, LANE), lambda b: (b, 0)),
        out_shape=jax.ShapeDtypeStruct((nb * tp, LANE), BF16),
        compiler_params=_cparams(("parallel",)), name="forget_bias",
    )(sm)


def _fox_prompt_kernel(q_ref, k_ref, e_ref, vt_ref, o_ref, qa_sc, m_sc, acc_sc, sa_sc, sb_sc, *, tq):
    i = pl.program_id(1)
    causal = (lax.broadcasted_iota(jnp.int32, (tq, tq), 0)
              <= lax.broadcasted_iota(jnp.int32, (tq, tq), 1))
    feat = lax.broadcasted_iota(jnp.int32, (LANE, 1), 0)
    for p in range(H // 2):
        qt = q_ref[:, p * LANE:(p + 1) * LANE].astype(F32).T
        for h in (2 * p, 2 * p + 1):
            mine = (feat < HD) if h % 2 == 0 else (feat >= HD)
            pick = (feat == _F_COPIES[0] + h) | (feat == _F_COPIES[1] + h) | (feat == _F_COPIES[2] + h)
            qa_sc[h, :LANE, :] = jnp.where(mine, qt, 0.0).astype(BF16)
            qa_sc[h, LANE:, :] = jnp.broadcast_to(jnp.where(pick, -1.0, 0.0), (LANE, tq)).astype(BF16)
    m_sc[...] = jnp.full(m_sc.shape, NEG, F32)
    acc_sc[...] = jnp.zeros(acc_sc.shape, F32)

    def logits_h(j, s_sc, h):
        rows = pl.ds(pl.multiple_of(j * tq, tq), tq)
        ka = jnp.concatenate([k_ref[rows, (h // 2) * LANE:(h // 2 + 1) * LANE], e_ref[rows, :]], axis=1)
        s_sc[h] = jnp.dot(ka, qa_sc[h], preferred_element_type=F32)

    def softmax_pv_h(j, s_sc, masked, h):
        s = s_sc[h]
        if masked:
            s = jnp.where(causal, s, NEG)
        m_old = m_sc[h:h + 1, :]
        m_new = jnp.maximum(m_old, jnp.max(s, axis=0, keepdims=True))
        m_sc[h:h + 1, :] = m_new
        pm = jnp.exp2(s - m_new).astype(BF16)
        hs = slice(h * _VROWS, (h + 1) * _VROWS)
        pv = jnp.dot(vt_ref[j, hs, :], pm, preferred_element_type=F32)
        acc_sc[hs, :] = jnp.exp2(m_old - m_new) * acc_sc[hs, :] + pv

    def logits(j, s_sc):
        for h in range(H):
            logits_h(j, s_sc, h)

    def softmax_pv(j, s_sc, masked):
        for h in range(H):
            softmax_pv_h(j, s_sc, masked, h)

    def body(jj, carry):
        j = 2 * jj
        logits(j + 1, sb_sc)
        softmax_pv(j, sa_sc, False)
        logits(j + 2, sa_sc)
        softmax_pv(j + 1, sb_sc, False)
        return carry

    logits(0, sa_sc)
    lax.fori_loop(0, i // 2, body, 0)

    @pl.when(i % 2 == 0)
    def _():
        softmax_pv(i, sa_sc, True)

    @pl.when(i % 2 == 1)
    def _():
        logits(i, sb_sc)
        softmax_pv(i - 1, sa_sc, False)
        softmax_pv(i, sb_sc, True)

    out = jnp.concatenate([acc_sc[h * _VROWS:h * _VROWS + HD, :] / acc_sc[h * _VROWS + HD:h * _VROWS + HD + 1, :]
                           for h in range(H)], axis=0)
    o_ref[...] = out.T.astype(o_ref.dtype)


def _fox_prompt(qb, kb, eb, vt, nb, tp, tq):
    nq = tp // tq
    return pl.pallas_call(
        functools.partial(_fox_prompt_kernel, tq=tq), grid=(nb, nq),
        in_specs=[pl.BlockSpec((tq, D_FOX), lambda b, i: (b * nq + i, 0)),
                  pl.BlockSpec((tp, D_FOX), lambda b, i: (b, 0)),
                  pl.BlockSpec((tp, LANE), lambda b, i: (b, 0)),
                  pl.BlockSpec((nq, H * _VROWS, tq), lambda b, i: (b, 0, 0))],
        out_specs=pl.BlockSpec((tq, D_FOX), lambda b, i: (b * nq + i, 0)),
        out_shape=jax.ShapeDtypeStruct((nb * tp, D_FOX), BF16),
        scratch_shapes=[pltpu.VMEM((H, 2 * LANE, tq), BF16), pltpu.VMEM((8, tq), F32),
                        pltpu.VMEM((H * _VROWS, tq), F32),
                        pltpu.VMEM((H, tq, tq), F32), pltpu.VMEM((H, tq, tq), F32)],
        compiler_params=_cparams(("parallel", "arbitrary")), name="fox_prompt",
    )(qb, kb, eb, vt)


def _fox_sample_kernel(q_ref, kn_ref, vn_ref, kct_ref, vct_ref, lf_ref, o_ref, *, past, ls, nseq):
    for s in range(nseq):
        rows = slice(s * ls, (s + 1) * ls)
        _fox_sample_one(q_ref.at[rows], kn_ref.at[rows], vn_ref.at[rows], kct_ref.at[s], vct_ref.at[s],
                        lf_ref.at[s], o_ref.at[rows], past, ls)


def _fox_sample_one(q_ref, kn_ref, vn_ref, kct_ref, vct_ref, lf_ref, o_ref, past, ls):
    causal = (lax.broadcasted_iota(jnp.int32, (ls, ls), 1)
              <= lax.broadcasted_iota(jnp.int32, (ls, ls), 0))
    nblk = lf_ref.shape[1] // LANE
    loc = _scan_lanes(jnp.concatenate([lf_ref[:, j * LANE:(j + 1) * LANE] for j in range(nblk)], axis=0))
    carry = jnp.zeros((8, 1), F32)
    cs = []
    for j in range(nblk):
        blk = loc[j * 8:(j + 1) * 8]
        cs.append(blk + carry)
        carry = carry + blk[:, LANE - 1:LANE]
    c_all = jnp.concatenate(cs, axis=1)

    q = q_ref[...]
    lane = lax.broadcasted_iota(jnp.int32, (1, D_FOX), 1)
    own = [(lane >= h * HD) & (lane < (h + 1) * HD) for h in range(H)]
    q_all = jnp.concatenate([jnp.where(own[h], q, jnp.zeros_like(q)) for h in range(H)], axis=0)
    bias = [c_all[h:h + 1, past - 1:past] - c_all[h:h + 1, :] for h in range(H)]
    bias_c = jnp.concatenate([jnp.broadcast_to(b[:, :past], (ls, past)) for b in bias], axis=0)
    bias_n = jnp.concatenate([jnp.broadcast_to(b[:, past:past + ls], (ls, ls)) for b in bias], axis=0)
    s_c = jnp.dot(q_all, kct_ref[...].astype(BF16), preferred_element_type=F32) + bias_c
    s_n = jnp.where(jnp.concatenate([causal] * H, axis=0), _nt(q_all, kn_ref[...]) + bias_n, NEG)
    m = jnp.maximum(jnp.max(s_c, axis=-1, keepdims=True), jnp.max(s_n, axis=-1, keepdims=True))
    p_c = jnp.exp(s_c - m)
    p_n = jnp.exp(s_n - m)
    den = jnp.sum(p_c, axis=-1, keepdims=True) + jnp.sum(p_n, axis=-1, keepdims=True)
    o_all = (_nt(p_c.astype(BF16), vct_ref[...].astype(BF16))
             + jnp.dot(p_n.astype(BF16), vn_ref[...], preferred_element_type=F32)) / den
    o = jnp.zeros((ls, D_FOX), F32)
    for h in range(H):
        o = jnp.where(own[h], o_all[h * ls:(h + 1) * ls], o)
    o_ref[...] = o.astype(o_ref.dtype)


def _fox_sample(qb, kb, vb, kct, vct, lf_all, l, nb, ls, past):
    nseq = 4
    new = pl.BlockSpec((nseq * ls, D_FOX), lambda b: (b, 0))
    cache = pl.BlockSpec((None, nseq, D_FOX, past), lambda b: (l, b, 0, 0))
    return pl.pallas_call(
        functools.partial(_fox_sample_kernel, past=past, ls=ls, nseq=nseq), grid=(nb // nseq,),
        in_specs=[new, new, new, cache, cache, pl.BlockSpec((nseq, 8, past + LANE), lambda b: (b, 0, 0))],
        out_specs=new, out_shape=jax.ShapeDtypeStruct((nb * ls, D_FOX), BF16),
        compiler_params=_cparams(("parallel",)), name="fox_sample",
    )(qb, kb, vb, kct, vct, lf_all)


def _pool_tile(u, halo, w_ref, sc_ref, pos_start, tm):
    a = jnp.concatenate([halo, u], axis=0)
    e1 = a + pltpu.roll(a, 1, 0)
    e2 = e1 + pltpu.roll(e1, 2, 0)
    e3 = e2 + pltpu.roll(e2, 4, 0)
    e4 = e3 + pltpu.roll(e3, 8, 0)
    lane = lax.broadcasted_iota(jnp.int32, (1, D_POOL), 1)
    win = jnp.where(lane < 64, e1, jnp.where(lane < 128, e2, jnp.where(lane < 192, e3, e4)))[16:]
    wsz = jnp.where(lane < 64, 2.0, jnp.where(lane < 128, 4.0, jnp.where(lane < 192, 8.0, 16.0)))
    pos = (pos_start + lax.broadcasted_iota(jnp.int32, (tm, 1), 0)).astype(F32)
    diff = win / jnp.minimum(pos + 1.0, wsz) - u
    return jnp.dot(diff.astype(BF16), w_ref[...], preferred_element_type=F32) * sc_ref[...]


def _pool_kernel(u_ref, pre_ref, w_ref, sc_ref, o_ref, halo_sc, *, tm, pos0):
    t = pl.program_id(1)

    @pl.when(t == 0)
    def _():
        halo_sc[...] = pre_ref[0]

    u = u_ref[...]
    o_ref[...] = _pool_tile(u, halo_sc[...], w_ref, sc_ref, pos0 + t * tm, tm).astype(o_ref.dtype)
    halo_sc[...] = u[tm - 16:]


def _pool(u, prefix, w, l, nb, nt, tm, pos0):
    return pl.pallas_call(
        functools.partial(_pool_kernel, tm=tm, pos0=pos0), grid=(nb, nt),
        in_specs=[pl.BlockSpec((tm, D_POOL), lambda b, t: (b * nt + t, 0)),
                  pl.BlockSpec((1, 16, D_POOL), lambda b, t: (b, 0, 0)),
                  _layer_spec(w["wbd"], l, 2), _layer_spec(w["pscale"], l, 2)],
        out_specs=pl.BlockSpec((tm, D_POOL), lambda b, t: (b * nt + t, 0)),
        out_shape=jax.ShapeDtypeStruct((nb * nt * tm, D_POOL), BF16),
        scratch_shapes=[pltpu.VMEM((16, D_POOL), F32)],
        compiler_params=_cparams(("parallel", "arbitrary")), name="pool_mixer",
    )(u, prefix, w["wbd"], w["pscale"])


def _ssd_chunk(xbc, z, sm, smt, ext_sc, st_sc, cw_ref, cb_ref, arow_ref, acol_ref, dsk_ref, gn_ref,
               row0, q, l_valid, live=None):
    ext_sc[SUBLANE:, :] = xbc
    w = cw_ref[...]
    conv = cb_ref[...]
    for j in range(CONV_W):
        conv = conv + w[j:j + 1] * ext_sc[pl.ds(SUBLANE - (CONV_W - 1) + j, q), :]
    ext_sc[:SUBLANE, :] = ext_sc[q:, :]
    act = _silu(conv)
    xs = act[:, :D_SSD]
    bbf = act[:, D_SSD:D_SSD + D_BC]
    bb = bbf.astype(BF16)
    cc = act[:, D_SSD + D_BC:].astype(BF16)
    bbt = bbf.T

    rvalid = (row0 + lax.broadcasted_iota(jnp.int32, (q, 1), 0)) < l_valid
    dtc = jnp.where(rvalid, sm, 0.0)
    acs_c = _scan_rows(dtc * (-jnp.exp(arow_ref[...])))
    cvalid = (row0 + lax.broadcasted_iota(jnp.int32, (1, LANE), 1)) < l_valid
    dtt = jnp.where(cvalid, smt, 0.0)
    acs_t = _scan_lanes(dtt * (-jnp.exp(acol_ref[...])))

    lane = lax.broadcasted_iota(jnp.int32, (1, LANE), 1)
    lo = lane < HD
    n_lo = lax.broadcasted_iota(jnp.int32, (LANE, 1), 0) < HD
    causal = (lax.broadcasted_iota(jnp.int32, (q, q), 1)
              <= lax.broadcasted_iota(jnp.int32, (q, q), 0))
    zc = jnp.zeros_like(cc)
    cb = [_nt(jnp.where(lo, cc, zc), bb), _nt(jnp.where(lo, zc, cc), bb)]
    ys = []
    for p in range(H // 2):
        cols = slice(p * LANE, (p + 1) * LANE)
        xp = xs[:, cols]
        xpb = xp.astype(BF16)
        st = st_sc[p]
        y_in = jnp.dot(cc, st.astype(BF16), preferred_element_type=F32)
        y_h, upd, dec = [], [], []
        for hh in range(2):
            h = 2 * p + hh
            g = h // (H // 2)
            arep = jnp.broadcast_to(acs_c[:, _DT_AT + h:_DT_AT + h + 1], (q, LANE))
            ak = acs_t[_DT_AT + h:_DT_AT + h + 1, :q]
            dtr = dtt[_DT_AT + h:_DT_AT + h + 1, :q]
            lm = jnp.exp(jnp.where(causal, arep[:, :q] - ak, NEG))
            gm = (cb[g] * lm * dtr).astype(BF16)
            y_h.append(jnp.dot(gm, xpb, preferred_element_type=F32) + y_in * jnp.exp(arep))
            alast = acs_t[_DT_AT + h:_DT_AT + h + 1, q - 1:q]
            bw = (bbt * (jnp.exp(alast - ak) * dtr)).astype(BF16)
            keep = (lo if hh == 0 else ~lo) & (n_lo if g == 0 else ~n_lo)
            upd.append(jnp.where(keep, jnp.dot(bw, xpb, preferred_element_type=F32), 0.0))
            dec.append(jnp.exp(alast))
        st_new = jnp.where(lo, dec[0], dec[1]) * st + upd[0] + upd[1]
        st_sc[p] = st_new if live is None else jnp.where(live, st_new, st)
        ys.append(jnp.where(lo, y_h[0], y_h[1]) + dsk_ref[:, cols] * xp)
    yc = jnp.concatenate(ys, axis=1) * _silu(z)
    return _rms(yc, gn_ref[...])


def _ssd_kernel(xbc_ref, z_ref, sm_ref, smt_ref, pre_ref, init_ref, cw_ref, cb_ref, arow_ref,
                acol_ref, dsk_ref, gn_ref, y_ref, fin_ref, ext_sc, st_sc, *, q, l_valid):
    c = pl.program_id(1)

    @pl.when(c == 0)
    def _():
        ext_sc[:SUBLANE, :] = pre_ref[0]
        st_sc[...] = init_ref[0]

    y = _ssd_chunk(xbc_ref[...], z_ref[...], sm_ref[...], smt_ref[...], ext_sc, st_sc, cw_ref, cb_ref,
                   arow_ref, acol_ref, dsk_ref, gn_ref, c * q, q, l_valid)
    y_ref[...] = y.astype(y_ref.dtype)

    @pl.when(c == pl.num_programs(1) - 1)
    def _():
        fin_ref[0] = st_sc[...]


def _ssd(xbc, z, sm, smt, prefix, init, w, l, nb, nc, q, l_valid):
    rows = lambda c: pl.BlockSpec((q, c), lambda b, i: (b * nc + i, 0))
    if smt.ndim == 2:
        smt_spec = pl.BlockSpec((16, LANE), lambda b, i: (0, b * nc + i))
    else:
        smt_spec = pl.BlockSpec((None, 16, LANE), lambda b, i: (b, 0, 0))
    names = ("cw", "cb", "arow", "acol", "dsk", "gn")
    state = pl.BlockSpec((1, H // 2, LANE, LANE), lambda b, i: (b, 0, 0, 0))
    return pl.pallas_call(
        functools.partial(_ssd_kernel, q=q, l_valid=l_valid), grid=(nb, nc),
        in_specs=[rows(D_CONV), rows(D_SSD), rows(LANE), smt_spec,
                  pl.BlockSpec((1, 8, D_CONV), lambda b, i: (b, 0, 0)),
                  state]
                 + [_layer_spec(w[n], l, 2) for n in names],
        out_specs=(rows(D_SSD), state),
        out_shape=(jax.ShapeDtypeStruct((nb * nc * q, D_SSD), BF16),
                   jax.ShapeDtypeStruct((nb, H // 2, LANE, LANE), F32)),
        scratch_shapes=[pltpu.VMEM((q + SUBLANE, D_CONV), F32), pltpu.VMEM((H // 2, LANE, LANE), F32)],
        compiler_params=_cparams(("parallel", "arbitrary")), name="conv_ssd",
    )(xbc, z, sm, smt, prefix, init, *[w[n] for n in names])


def _state_to_wide(s):
    def placed(h):
        g, hh = h // (H // 2), h % 2
        return jnp.pad(jnp.swapaxes(s[:, h].astype(F32), 1, 2),
                       ((0, 0), (g * HD, LANE - (g + 1) * HD), (hh * HD, LANE - (hh + 1) * HD)))
    return jnp.stack([placed(2 * p) + placed(2 * p + 1) for p in range(H // 2)], axis=1)


def _state_from_wide(wide):
    return jnp.stack([jnp.swapaxes(wide[:, h // 2, (h // (H // 2)) * HD:(h // (H // 2) + 1) * HD,
                                        (h % 2) * HD:(h % 2 + 1) * HD], 1, 2) for h in range(H)], axis=1)


def _ffn_tile(x, a, p, s, wo_ref, g1_ref, g2_ref, g3_ref, wg_ref, wu_ref, wd_ref):
    mp = (jnp.dot(a, wo_ref[:D_FOX, :], preferred_element_type=F32)
          + jnp.dot(p, wo_ref[D_FOX:D_FOX + D_POOL, :], preferred_element_type=F32)
          + jnp.dot(s, wo_ref[D_FOX + D_POOL:, :], preferred_element_type=F32))
    x1 = x + _rms(mp, g1_ref[...])
    hb = _rms(x1, g2_ref[...]).astype(BF16)
    gate = jnp.dot(hb, wg_ref[...], preferred_element_type=F32)
    up = jnp.dot(hb, wu_ref[...], preferred_element_type=F32)
    act = (_silu(gate) * up).astype(BF16)
    ff = jnp.dot(act, wd_ref[...], preferred_element_type=F32)
    return x1 + _rms(ff, g3_ref[...])


def _ffn_kernel(x_ref, a_ref, p_ref, s_ref, *rest):
    o_ref = rest[-1]
    o_ref[...] = _ffn_tile(x_ref[...], a_ref[...], p_ref[...], s_ref[...], *rest[:-1])


def _mix_ffn_kernel(x_ref, a_ref, u_ref, xbc_ref, z_ref, sm_ref, smt_ref, wbd_ref, psc_ref,
                    cw_ref, cb_ref, arow_ref, acol_ref, dsk_ref, gn_ref,
                    wo_ref, g1_ref, g2_ref, g3_ref, wg_ref, wu_ref, wd_ref,
                    o_ref, fin_ref, halo_sc, ext_sc, st_sc, pool_sc, ssd_sc, *, tm, q, per, nt, l_valid):
    i = pl.program_id(0)
    live = i < nt
    ia = jnp.minimum(i, nt - 1)
    first = (ia % per) == 0
    row0 = (ia % per) * tm

    @pl.when(i == 0)
    def _():
        pool_sc[...] = jnp.zeros(pool_sc.shape, pool_sc.dtype)
        ssd_sc[...] = jnp.zeros(ssd_sc.shape, ssd_sc.dtype)

    o_ref[...] = _ffn_tile(x_ref[...], a_ref[...], pool_sc[...], ssd_sc[...], wo_ref, g1_ref, g2_ref, g3_ref,
                           wg_ref, wu_ref, wd_ref)

    u = u_ref[...]
    halo = jnp.where(first, 0.0, halo_sc[...])
    pool_sc[...] = _pool_tile(u, halo, wbd_ref, psc_ref, row0, tm).astype(pool_sc.dtype)
    halo_sc[...] = u[tm - 16:]
    ext_sc[:SUBLANE, :] = jnp.where(first, 0.0, ext_sc[:SUBLANE, :])
    st_sc[...] = jnp.where(first, 0.0, st_sc[...])
    for c in range(tm // q):
        rows = slice(c * q, (c + 1) * q)
        y = _ssd_chunk(xbc_ref[rows, :], z_ref[rows, :], sm_ref[rows, :], smt_ref[:, c * q:(c + 1) * q],
                       ext_sc, st_sc, cw_ref, cb_ref, arow_ref, acol_ref, dsk_ref, gn_ref,
                       row0 + c * q, q, l_valid, live)
        ssd_sc[rows, :] = y.astype(ssd_sc.dtype)
    fin_ref[0] = st_sc[...]


def _mix_ffn(x, attn, u, xbc, z, sm, smt, w, l, nb, tp, tm, q, l_valid):
    assert q == LANE and tm % q == 0
    per = tp // tm
    nt = nb * per
    prev = lambda c: pl.BlockSpec((tm, c), lambda i: (jnp.maximum(i - 1, 0), 0))
    cur = lambda c: pl.BlockSpec((tm, c), lambda i: (jnp.minimum(i, nt - 1), 0))
    names = ("wbd", "pscale", "cw", "cb", "arow", "acol", "dsk", "gn")
    big = ("wo", "g1", "g2", "g3", "wg", "wu", "wd")
    return pl.pallas_call(
        functools.partial(_mix_ffn_kernel, tm=tm, q=q, per=per, nt=nt, l_valid=l_valid), grid=(nt + 1,),
        in_specs=[prev(D_MODEL), prev(D_FOX), cur(D_POOL), cur(D_CONV), cur(D_SSD), cur(LANE),
                  pl.BlockSpec((16, tm), lambda i: (0, jnp.minimum(i, nt - 1)))]
                 + [_layer_spec(w[n], l, 1) for n in names]
                 + [_layer_spec(w[n], l, 1, single_buffer=True) for n in big],
        out_specs=(prev(D_MODEL),
                   pl.BlockSpec((1, H // 2, LANE, LANE), lambda i: (jnp.minimum(i, nt - 1) // per, 0, 0, 0))),
        out_shape=(jax.ShapeDtypeStruct((nt * tm, D_MODEL), F32),
                   jax.ShapeDtypeStruct((nb, H // 2, LANE, LANE), F32)),
        scratch_shapes=[pltpu.VMEM((16, D_POOL), F32), pltpu.VMEM((q + SUBLANE, D_CONV), F32),
                        pltpu.VMEM((H // 2, LANE, LANE), F32),
                        pltpu.VMEM((tm, D_POOL), BF16), pltpu.VMEM((tm, D_SSD), BF16)],
        compiler_params=_cparams(("arbitrary",)), name="mix_ffn",
    )(x, attn, u, xbc, z, sm, smt, *[w[n] for n in names], *[w[n] for n in big])


def _ffn(x, attn, pool, ssd, w, l, tm):
    t = x.shape[0]
    row = lambda c: pl.BlockSpec((tm, c), lambda i: (i, 0))
    names = ("wo", "g1", "g2", "g3", "wg", "wu", "wd")
    return pl.pallas_call(
        _ffn_kernel, grid=(t // tm,),
        in_specs=[row(D_MODEL), row(D_FOX), row(D_POOL), row(D_SSD)]
                 + [_layer_spec(w[n], l, 1, single_buffer=True) for n in names],
        out_specs=row(D_MODEL), out_shape=jax.ShapeDtypeStruct((t, D_MODEL), F32),
        compiler_params=_cparams(("parallel",)), name="outproj_ffn",
    )(x, attn, pool, ssd, *[w[n] for n in names])


def _lane_pack(f_vals, dt_vals):
    pieces, at = [], 0
    for start, vals in sorted([(_DT_AT, dt_vals)] + [(c, f_vals) for c in _F_COPIES], key=lambda t: t[0]):
        pieces += [jnp.zeros(vals.shape[:-1] + (start - at,), F32), vals.astype(F32)]
        at = start + H
    pieces.append(jnp.zeros(f_vals.shape[:-1] + (LANE - at,), F32))
    return jnp.concatenate(pieces, axis=-1)


def _prep_weights(w_in, fox_f_bias, pool_w, pool_scale, conv_w, conv_b, dt_bias, a_log, d_skip, ssd_norm,
                  w_out, w_gate, w_up, w_down, ln_pre_mix, ln_post_mix, ln_pre_ffn, ln_post_ffn):
    depth = w_in.shape[0]
    f0 = 3 * D_FOX
    u0 = f0 + H
    dt0 = u0 + D_POOL + D_SSD + D_CONV
    ws = _lane_pack(w_in[:, :, f0:u0], w_in[:, :, dt0:dt0 + H])
    wm = jnp.concatenate([w_in[:, :, :D_FOX], ws, w_in[:, :, u0:dt0]], axis=2).astype(BF16)
    ws = ws.astype(BF16)
    bs = _lane_pack(fox_f_bias, dt_bias)
    wbd = jnp.concatenate([jnp.pad(pool_w[:, g].astype(F32), ((0, 0), (0, 0), (g * 64, D_POOL - (g + 1) * 64)))
                           for g in range(D_POOL // 64)], axis=1)
    alog = _lane_pack(jnp.zeros_like(a_log), a_log)
    row = lambda a: a.astype(F32).reshape(depth, 1, -1)
    tr = lambda a: jnp.transpose(a, (0, 2, 1))
    return dict(
        g_pre=row(ln_pre_mix), wm=wm, wkv=w_in[:, :, D_FOX:f0].astype(BF16),
        wkt=tr(w_in[:, :, D_FOX:2 * D_FOX]).astype(BF16), wvt=tr(w_in[:, :, 2 * D_FOX:f0]).astype(BF16),
        wst=tr(ws[:, :, :16]),
        bs=bs.reshape(depth, 1, LANE), bst=bs[:, :16].reshape(depth, 16, 1),
        wbd=wbd.astype(BF16), pscale=row(pool_scale),
        cw=jnp.pad(conv_w.astype(F32), ((0, 0), (0, SUBLANE - CONV_W), (0, 0))), cb=row(conv_b),
        arow=alog.reshape(depth, 1, LANE), acol=alog[:, :16].reshape(depth, 16, 1),
        dsk=row(jnp.repeat(d_skip, HD, axis=1)), gn=row(ssd_norm),
        wo=w_out.astype(BF16), g1=row(ln_post_mix), g2=row(ln_pre_ffn), g3=row(ln_post_ffn),
        wg=w_gate.astype(BF16), wu=w_up.astype(BF16), wd=w_down.astype(BF16))


def _mixers_and_ffn(x, w, l, attn_fn, prompt_dims, nb, nc, q, l_valid, pos0, pool_prefix, conv_prefix,
                    ssd_init, smt_fn, tm_tok, tm_pool, kv_stacks=()):
    proj = _inproj(x, w, l, tm_tok, prompt_dims, kv_stacks)
    u, z, xbc, sm, smt = proj[5:]
    attn = attn_fn(proj[:5], sm, smt)
    seq = nc * q
    if prompt_dims is not None:
        x, fin = _mix_ffn(x, attn, u, xbc, z, sm, smt, w, l, nb, seq, tm_tok, q, l_valid)
        return x, proj, fin
    pool = _pool(u, pool_prefix, w, l, nb, seq // tm_pool, tm_pool, pos0)
    ssd, fin = _ssd(xbc, z, sm, smt_fn(smt), conv_prefix, ssd_init, w, l, nb, nc, q, l_valid)
    x = _ffn(x, attn, pool, ssd, w, l, tm_tok)
    return x, proj, fin


def kernel(x_prompt, x_sample, cache_fox_k, cache_fox_v, cache_fox_logf, state_pool, state_conv, state_ssd,
           meta_tokens, ln_pre_mix, ln_post_mix, ln_pre_ffn, ln_post_ffn, w_in, fox_f_bias, pool_w, pool_scale,
           conv_w, conv_b, dt_bias, a_log, d_skip, ssd_norm, w_out, w_gate, w_up, w_down):
    nbp, seq, _ = x_prompt.shape
    nbs, ls, _ = x_sample.shape
    depth, _, past = cache_fox_logf.shape[:3]
    lp = N_META + seq
    tq = 256
    q_ssd = 128
    tp = -(-lp // tq) * tq
    pool_tile = tp // 4
    assert pool_tile % 16 == 0

    w = _prep_weights(w_in, fox_f_bias, pool_w, pool_scale, conv_w, conv_b, dt_bias, a_log, d_skip, ssd_norm,
                      w_out, w_gate, w_up, w_down, ln_pre_mix, ln_post_mix, ln_pre_ffn, ln_post_ffn)

    meta = jnp.broadcast_to(meta_tokens.astype(F32)[None], (nbp, N_META, D_MODEL))
    xp = jnp.concatenate([meta, x_prompt, jnp.zeros((nbp, tp - lp, D_MODEL), F32)], axis=1)
    xp = xp.reshape(nbp * tp, D_MODEL)
    xs = x_sample.reshape(nbs * ls, D_MODEL)

    zero_pool = jnp.zeros((nbp, 16, D_POOL), F32)
    zero_conv = jnp.zeros((nbp, 8, D_CONV), F32)
    zero_ssd = jnp.zeros((nbp, H // 2, LANE, LANE), F32)
    ssd_init = _state_to_wide(state_ssd.reshape(depth * nbs, H, HD, HD)).reshape(depth, nbs, H // 2, LANE, LANE)

    kct = jnp.transpose(cache_fox_k, (0, 1, 3, 4, 2)).reshape(depth, nbs, D_FOX, past)
    vct = jnp.transpose(cache_fox_v, (0, 1, 3, 4, 2)).reshape(depth, nbs, D_FOX, past)
    lfc = jnp.pad(jnp.transpose(cache_fox_logf.astype(F32), (0, 1, 3, 2)),
                  ((0, 0), (0, 0), (0, 8 - H), (0, 0)))
    pool_pre = jnp.pad(state_pool.astype(F32), ((0, 0), (0, 0), (16 - POOL_BUF, 0), (0, 0)))
    conv_pre = jnp.pad(state_conv.astype(F32), ((0, 0), (0, 0), (8 - (CONV_W - 1), 0), (0, 0)))

    def to_seq_major(smt):
        return jnp.transpose(smt.reshape(16, nbs, ls), (1, 0, 2))

    outs_p, outs_s = [], []
    kv_stacks = tuple(jnp.zeros((depth, nbp, D_FOX, lp), F32) for _ in range(2))
    for l in range(depth):
        def attn_p(qkv, sm, smt):
            qb, kb, _, _, vtb = qkv
            return _fox_prompt(qb, kb, _forget_bias(sm, nbp, tp), vtb, nbp, tp, tq)

        xp, proj, fin = _mixers_and_ffn(xp, w, l, attn_p, (nbp, tp, lp), nbp, tp // q_ssd, q_ssd, lp, 0,
                                        zero_pool, zero_conv, zero_ssd, lambda smt: smt, tq, pool_tile, kv_stacks)
        kv_stacks = (proj[2], proj[3])
        outs_p.append((None, None, proj[9], proj[5], proj[7], fin))

        def attn_s(qkv, sm, smt):
            qb, _, kb, _, vb = qkv
            lf_all = jnp.concatenate([lfc[l], to_seq_major(smt)[:, :8], jnp.zeros((nbs, 8, LANE - ls), F32)],
                                     axis=2)
            return _fox_sample(qb, kb, vb, kct, vct, lf_all, l, nbs, ls, past)

        xs, proj, fin = _mixers_and_ffn(xs, w, l, attn_s, None, nbs, 1, ls, ls, past, pool_pre[l], conv_pre[l],
                                        ssd_init[l],
                                        lambda smt: jnp.pad(to_seq_major(smt), ((0, 0), (0, 0), (0, LANE - ls))),
                                        tq, ls)
        outs_s.append((proj[1], proj[3], proj[9], proj[5], proj[7], fin))

    def tails(outs, nb, rows, valid):
        seqv = lambda a: a.reshape(nb, rows, a.shape[-1])
        lf = jnp.stack([jnp.transpose(o[2][:H].reshape(H, nb, rows)[:, :, :valid], (1, 2, 0)) for o in outs])
        pn = jnp.stack([seqv(o[3])[:, valid - POOL_BUF:valid] for o in outs])
        cn = jnp.stack([seqv(o[4])[:, valid - (CONV_W - 1):valid] for o in outs])
        sn = jnp.stack([_state_from_wide(o[5]) for o in outs])
        return lf, pn, cn, sn

    def token_minor(a):
        return jnp.transpose(a.reshape(depth, nbp, H, HD, lp), (0, 1, 4, 2, 3))

    def token_major(outs, i):
        return jnp.stack([o[i].reshape(nbs, ls, H, HD) for o in outs])

    y_prompt = xp.reshape(nbp, tp, D_MODEL)[:, N_META:lp]
    y_sample = xs.reshape(nbs, ls, D_MODEL)
    return ((y_prompt, y_sample, token_minor(kv_stacks[0]), token_minor(kv_stacks[1])) + tails(outs_p, nbp, tp, lp)
            + (token_major(outs_s, 0), token_major(outs_s, 1)) + tails(outs_s, nbs, ls, ls))
```

```python
import functools
import math

import jax
import jax.numpy as jnp
from jax import lax
from jax.experimental import pallas as pl
from jax.experimental.pallas import tpu as pltpu

F32 = jnp.float32
BF16 = jnp.bfloat16

D_MODEL = 1024
N_META = 16
EPS = 1e-6
H = 6
HD = 64
D_FOX = H * HD
D_POOL = 256
POOL_BUF = 15
D_SSD = H * HD
D_BC = 128
D_CONV = D_SSD + 2 * D_BC
CONV_W = 4
D_FF = 2816
LANE = 128
SUBLANE = 8
NEG = -1e30
LOG2E = math.log2(math.e)

_Q0, _S0, _U0, _Z0, _X0, _MAIN = 0, 384, 512, 768, 1152, 1792
_MXU_N = 256
_GROUPS = ((_Q0, _U0), (_U0, _Z0), (_Z0, _X0 + LANE), (_X0 + LANE, _MAIN))
_F_AT, _DT_AT, _F_COPIES = 0, 8, (0, 16, 32)
_VROWS = HD + SUBLANE

_VMEM_LIMIT = 56 * 1024 * 1024


def _cparams(sem):
    return pltpu.CompilerParams(dimension_semantics=sem, vmem_limit_bytes=_VMEM_LIMIT)


def _layer_spec(a, l, ngrid, single_buffer=False):
    idx = (l,) + (0,) * (a.ndim - 1)
    kw = dict(pipeline_mode=pl.Buffered(1)) if single_buffer else {}
    return pl.BlockSpec((None,) + a.shape[1:], lambda *_: idx, **kw)


def _rms(x, g):
    ms = jnp.mean(x * x, axis=-1, keepdims=True)
    return x * lax.rsqrt(ms + EPS) * g


def _silu(x):
    return x * jax.nn.sigmoid(x)


def _softplus_tail(x):
    return jnp.log1p(jnp.exp(-jnp.abs(x)))


def _nt(a, b):
    return lax.dot_general(a, b, (((1,), (1,)), ((), ())), preferred_element_type=F32)


def _tn(a, b):
    return lax.dot_general(a, b, (((0,), (0,)), ((), ())), preferred_element_type=F32)


def _scan_lanes(x):
    lane = lax.broadcasted_iota(jnp.int32, x.shape, 1)
    s = 1
    while s < x.shape[1]:
        x = x + jnp.where(lane >= s, pltpu.roll(x, s, 1), 0.0)
        s *= 2
    return x


def _scan_rows(x):
    row = lax.broadcasted_iota(jnp.int32, x.shape, 0)
    s = 1
    while s < x.shape[0]:
        x = x + jnp.where(row >= s, pltpu.roll(x, s, 0), 0.0)
        s *= 2
    return x


def _inproj_kernel(x_ref, g_ref, wm_ref, wst_ref, bs_ref, bst_ref, *rest, prompt, qscale, per):
    hb = _rms(x_ref[...], g_ref[...]).astype(BF16)
    hbt = hb.T

    if prompt:
        (wkt_ref, wvt_ref, _, _, qb_ref, kb_ref, kt_ref, vt_ref, vtb_ref, e_ref, u_ref, z_ref, xbc_ref, sm_ref,
         smt_ref, carry_sc) = rest
        kt = jnp.dot(wkt_ref[...], hbt, preferred_element_type=F32)
        kt_ref[...] = kt
        kb_ref[...] = kt.T.astype(BF16)
        vt = jnp.dot(wvt_ref[...], hbt, preferred_element_type=F32)
        vt_ref[...] = vt
        tm = vt.shape[1]
        ones_row = jnp.where(lax.broadcasted_iota(jnp.int32, (SUBLANE, tm), 0) == 0, 1.0, 0.0).astype(BF16)
        for h in range(H):
            vtb_ref[0, h * _VROWS:h * _VROWS + HD, :] = vt[h * HD:(h + 1) * HD].astype(BF16)
            vtb_ref[0, h * _VROWS + HD:(h + 1) * _VROWS, :] = ones_row
    else:
        wkv_ref, qb_ref, k_ref, kb_ref, v_ref, vb_ref, u_ref, z_ref, xbc_ref, sm_ref, smt_ref = rest
        kv = jnp.dot(hb, wkv_ref[...], preferred_element_type=F32)
        k_ref[...] = kv[:, :D_FOX]
        kb_ref[...] = kv[:, :D_FOX].astype(BF16)
        v_ref[...] = kv[:, D_FOX:]
        vb_ref[...] = kv[:, D_FOX:].astype(BF16)
    g0, g1, g2, g3 = (jnp.dot(hb, wm_ref[:, a:b], preferred_element_type=F32) for a, b in _GROUPS)
    qb_ref[...] = (g0[:, :_S0] * qscale).astype(BF16)
    u_ref[...] = g1
    z_ref[...] = g2[:, :_X0 - _Z0]
    xbc_ref[:, :LANE] = g2[:, _X0 - _Z0:]
    xbc_ref[:, LANE:] = g3

    sm = g0[:, _S0:] + bs_ref[...]
    lane = lax.broadcasted_iota(jnp.int32, sm.shape, 1)
    tail = _softplus_tail(sm)
    is_dt = (lane >= _DT_AT) & (lane < _DT_AT + 8)
    sm = jnp.where(is_dt, jnp.maximum(sm, 0.0) + tail, jnp.minimum(sm, 0.0) - tail)
    sm_ref[...] = sm
    if prompt:
        first = (pl.program_id(0) % per) == 0
        c = _scan_rows(sm) + jnp.where(first, 0.0, carry_sc[...])
        carry_sc[...] = c[-1:, :]
        lane1 = lane[:1]
        c2 = c * LOG2E
        hi = c2.astype(BF16).astype(F32)
        mid = (c2 - hi).astype(BF16).astype(F32)
        e = jnp.where(lane1 < 16, hi, jnp.where(lane1 < 32, mid, (c2 - hi) - mid))
        e_ref[...] = jnp.where(((lane1 & 15) < H) & (lane1 < 48), e, 0.0).astype(BF16)
    smt = jnp.dot(wst_ref[...], hbt, preferred_element_type=F32) + bst_ref[...]
    row = lax.broadcasted_iota(jnp.int32, smt.shape, 0)
    tail = _softplus_tail(smt)
    smt_ref[...] = jnp.where(row < _DT_AT, jnp.minimum(smt, 0.0) - tail, jnp.maximum(smt, 0.0) + tail)


def _inproj(x, w, l, tm, prompt_dims=None, kv_stacks=()):
    t = x.shape[0]
    nt = t // tm
    row = lambda c: pl.BlockSpec((tm, c), lambda i: (i, 0))
    sds = jax.ShapeDtypeStruct
    tail_shapes = (sds((t, D_POOL), F32), sds((t, D_SSD), F32), sds((t, D_CONV), F32),
                   sds((t, LANE), F32), sds((16, t), F32))
    tail_specs = (row(D_POOL), row(D_SSD), row(D_CONV), row(LANE), pl.BlockSpec((16, tm), lambda i: (0, i)))
    if prompt_dims is not None:
        nb, tp, lp = prompt_dims
        per = tp // tm
        tmin = pl.BlockSpec((None, None, D_FOX, tm), lambda i: (l, i // per, 0, i % per))
        out_shape = (sds((t, D_FOX), BF16), sds((t, D_FOX), BF16), sds(kv_stacks[0].shape, F32),
                     sds(kv_stacks[1].shape, F32), sds((nt, H * _VROWS, tm), BF16),
                     sds((t, LANE), BF16)) + tail_shapes
        out_specs = (row(D_FOX), row(D_FOX), tmin, tmin,
                     pl.BlockSpec((1, H * _VROWS, tm), lambda i: (i, 0, 0)), row(LANE)) + tail_specs
        qscale = (HD ** -0.5) * LOG2E
        scratch = [pltpu.VMEM((1, LANE), F32)]
    else:
        per = None
        out_shape = (sds((t, D_FOX), BF16), sds((t, D_FOX), F32), sds((t, D_FOX), BF16),
                     sds((t, D_FOX), F32), sds((t, D_FOX), BF16)) + tail_shapes
        out_specs = (row(D_FOX),) * 5 + tail_specs
        qscale = HD ** -0.5
        scratch = []
    names = ("g_pre", "wm", "wst", "bs", "bst") + (("wkt", "wvt") if prompt_dims is not None else ("wkv",))
    n_in = 1 + len(names)
    return pl.pallas_call(
        functools.partial(_inproj_kernel, prompt=prompt_dims is not None, qscale=qscale, per=per), grid=(nt,),
        in_specs=[row(D_MODEL)] + [_layer_spec(w[n], l, 1) for n in names]
                 + [pl.BlockSpec(memory_space=pl.ANY)] * len(kv_stacks),
        out_specs=out_specs, out_shape=out_shape, scratch_shapes=scratch,
        input_output_aliases={n_in + s: 2 + s for s in range(len(kv_stacks))},
        compiler_params=_cparams(("arbitrary",)), name="inproj",
    )(x, *[w[n] for n in names], *kv_stacks)


def _fox_prompt_kernel(q_ref, k_ref, e_ref, vt_ref, o_ref, qa_sc, m_sc, acc_sc, sa_sc, sb_sc, *, tq):
    i = pl.program_id(1)
    causal = (lax.broadcasted_iota(jnp.int32, (tq, tq), 0)
              <= lax.broadcasted_iota(jnp.int32, (tq, tq), 1))
    feat = lax.broadcasted_iota(jnp.int32, (LANE, 1), 0)
    for p in range(H // 2):
        qt = q_ref[:, p * LANE:(p + 1) * LANE].astype(F32).T
        for h in (2 * p, 2 * p + 1):
            mine = (feat < HD) if h % 2 == 0 else (feat >= HD)
            pick = (feat == _F_COPIES[0] + h) | (feat == _F_COPIES[1] + h) | (feat == _F_COPIES[2] + h)
            qa_sc[h, :LANE, :] = jnp.where(mine, qt, 0.0).astype(BF16)
            qa_sc[h, LANE:, :] = jnp.broadcast_to(jnp.where(pick, -1.0, 0.0), (LANE, tq)).astype(BF16)
    m_sc[...] = jnp.full(m_sc.shape, NEG, F32)
    acc_sc[...] = jnp.zeros(acc_sc.shape, F32)

    def logits_h(j, s_sc, h):
        rows = pl.ds(pl.multiple_of(j * tq, tq), tq)
        ka = jnp.concatenate([k_ref[rows, (h // 2) * LANE:(h // 2 + 1) * LANE], e_ref[rows, :]], axis=1)
        s_sc[h] = jnp.dot(ka, qa_sc[h], preferred_element_type=F32)

    def softmax_pv_h(j, s_sc, masked, h):
        s = s_sc[h]
        if masked:
            s = jnp.where(causal, s, NEG)
        m_old = m_sc[h:h + 1, :]
        m_new = jnp.maximum(m_old, jnp.max(s, axis=0, keepdims=True))
        m_sc[h:h + 1, :] = m_new
        pm = jnp.exp2(s - m_new).astype(BF16)
        hs = slice(h * _VROWS, (h + 1) * _VROWS)
        pv = jnp.dot(vt_ref[j, hs, :], pm, preferred_element_type=F32)
        acc_sc[hs, :] = jnp.exp2(m_old - m_new) * acc_sc[hs, :] + pv

    def logits(j, s_sc):
        for h in range(H):
            logits_h(j, s_sc, h)

    def softmax_pv(j, s_sc, masked):
        for h in range(H):
            softmax_pv_h(j, s_sc, masked, h)

    def body(jj, carry):
        j = 2 * jj
        logits(j + 1, sb_sc)
        softmax_pv(j, sa_sc, False)
        logits(j + 2, sa_sc)
        softmax_pv(j + 1, sb_sc, False)
        return carry

    logits(0, sa_sc)
    lax.fori_loop(0, i // 2, body, 0)

    @pl.when(i % 2 == 0)
    def _():
        softmax_pv(i, sa_sc, True)

    @pl.when(i % 2 == 1)
    def _():
        logits(i, sb_sc)
        softmax_pv(i - 1, sa_sc, False)
        softmax_pv(i, sb_sc, True)

    out = jnp.concatenate([acc_sc[h * _VROWS:h * _VROWS + HD, :] / acc_sc[h * _VROWS + HD:h * _VROWS + HD + 1, :]
                           for h in range(H)], axis=0)
    o_ref[...] = out.T.astype(o_ref.dtype)


def _fox_prompt(qb, kb, eb, vt, nb, tp, tq):
    nq = tp // tq
    return pl.pallas_call(
        functools.partial(_fox_prompt_kernel, tq=tq), grid=(nb, nq),
        in_specs=[pl.BlockSpec((tq, D_FOX), lambda b, i: (b * nq + i, 0)),
                  pl.BlockSpec((tp, D_FOX), lambda b, i: (b, 0)),
                  pl.BlockSpec((tp, LANE), lambda b, i: (b, 0)),
                  pl.BlockSpec((nq, H * _VROWS, tq), lambda b, i: (b, 0, 0))],
        out_specs=pl.BlockSpec((tq, D_FOX), lambda b, i: (b * nq + i, 0)),
        out_shape=jax.ShapeDtypeStruct((nb * tp, D_FOX), BF16),
        scratch_shapes=[pltpu.VMEM((H, 2 * LANE, tq), BF16), pltpu.VMEM((8, tq), F32),
                        pltpu.VMEM((H * _VROWS, tq), F32),
                        pltpu.VMEM((H, tq, tq), F32), pltpu.VMEM((H, tq, tq), F32)],
        compiler_params=_cparams(("parallel", "arbitrary")), name="fox_prompt",
    )(qb, kb, eb, vt)


def _fox_sample_kernel(q_ref, kn_ref, vn_ref, kct_ref, vct_ref, lf_ref, o_ref, *, past, ls, nseq):
    for s in range(nseq):
        rows = slice(s * ls, (s + 1) * ls)
        _fox_sample_one(q_ref.at[rows], kn_ref.at[rows], vn_ref.at[rows], kct_ref.at[s], vct_ref.at[s],
                        lf_ref.at[s], o_ref.at[rows], past, ls)


def _fox_sample_one(q_ref, kn_ref, vn_ref, kct_ref, vct_ref, lf_ref, o_ref, past, ls):
    causal = (lax.broadcasted_iota(jnp.int32, (ls, ls), 1)
              <= lax.broadcasted_iota(jnp.int32, (ls, ls), 0))
    nblk = lf_ref.shape[1] // LANE
    loc = _scan_lanes(jnp.concatenate([lf_ref[:, j * LANE:(j + 1) * LANE] for j in range(nblk)], axis=0))
    carry = jnp.zeros((8, 1), F32)
    cs = []
    for j in range(nblk):
        blk = loc[j * 8:(j + 1) * 8]
        cs.append(blk + carry)
        carry = carry + blk[:, LANE - 1:LANE]
    c_all = jnp.concatenate(cs, axis=1)

    q = q_ref[...]
    lane = lax.broadcasted_iota(jnp.int32, (1, D_FOX), 1)
    own = [(lane >= h * HD) & (lane < (h + 1) * HD) for h in range(H)]
    q_all = jnp.concatenate([jnp.where(own[h], q, jnp.zeros_like(q)) for h in range(H)], axis=0)
    bias = [c_all[h:h + 1, past - 1:past] - c_all[h:h + 1, :] for h in range(H)]
    bias_c = jnp.concatenate([jnp.broadcast_to(b[:, :past], (ls, past)) for b in bias], axis=0)
    bias_n = jnp.concatenate([jnp.broadcast_to(b[:, past:past + ls], (ls, ls)) for b in bias], axis=0)
    s_c = jnp.dot(q_all, kct_ref[...].astype(BF16), preferred_element_type=F32) + bias_c
    s_n = jnp.where(jnp.concatenate([causal] * H, axis=0), _nt(q_all, kn_ref[...]) + bias_n, NEG)
    m = jnp.maximum(jnp.max(s_c, axis=-1, keepdims=True), jnp.max(s_n, axis=-1, keepdims=True))
    p_c = jnp.exp(s_c - m)
    p_n = jnp.exp(s_n - m)
    den = jnp.sum(p_c, axis=-1, keepdims=True) + jnp.sum(p_n, axis=-1, keepdims=True)
    o_all = (_nt(p_c.astype(BF16), vct_ref[...].astype(BF16))
             + jnp.dot(p_n.astype(BF16), vn_ref[...], preferred_element_type=F32)) / den
    o = jnp.zeros((ls, D_FOX), F32)
    for h in range(H):
        o = jnp.where(own[h], o_all[h * ls:(h + 1) * ls], o)
    o_ref[...] = o.astype(o_ref.dtype)


def _fox_sample(qb, kb, vb, kct, vct, lf_all, l, nb, ls, past):
    nseq = 4
    new = pl.BlockSpec((nseq * ls, D_FOX), lambda b: (b, 0))
    cache = pl.BlockSpec((None, nseq, D_FOX, past), lambda b: (l, b, 0, 0))
    return pl.pallas_call(
        functools.partial(_fox_sample_kernel, past=past, ls=ls, nseq=nseq), grid=(nb // nseq,),
        in_specs=[new, new, new, cache, cache, pl.BlockSpec((nseq, 8, past + LANE), lambda b: (b, 0, 0))],
        out_specs=new, out_shape=jax.ShapeDtypeStruct((nb * ls, D_FOX), BF16),
        compiler_params=_cparams(("parallel",)), name="fox_sample",
    )(qb, kb, vb, kct, vct, lf_all)


def _pool_tile(u, halo, w_ref, sc_ref, pos_start, tm):
    a = jnp.concatenate([halo, u], axis=0)
    e1 = a + pltpu.roll(a, 1, 0)
    e2 = e1 + pltpu.roll(e1, 2, 0)
    e3 = e2 + pltpu.roll(e2, 4, 0)
    e4 = e3 + pltpu.roll(e3, 8, 0)
    lane = lax.broadcasted_iota(jnp.int32, (1, D_POOL), 1)
    win = jnp.where(lane < 64, e1, jnp.where(lane < 128, e2, jnp.where(lane < 192, e3, e4)))[16:]
    wsz = jnp.where(lane < 64, 2.0, jnp.where(lane < 128, 4.0, jnp.where(lane < 192, 8.0, 16.0)))
    pos = (pos_start + lax.broadcasted_iota(jnp.int32, (tm, 1), 0)).astype(F32)
    diff = win / jnp.minimum(pos + 1.0, wsz) - u
    return jnp.dot(diff.astype(BF16), w_ref[...], preferred_element_type=F32) * sc_ref[...]


def _pool_kernel(u_ref, pre_ref, w_ref, sc_ref, o_ref, *, ls, nseq, pos0):
    for s in range(nseq):
        rows = slice(s * ls, (s + 1) * ls)
        o_ref[rows, :] = _pool_tile(u_ref[rows, :], pre_ref[s], w_ref, sc_ref, pos0, ls).astype(o_ref.dtype)


def _pool(u, prefix, w, l, nb, ls, pos0):
    nseq = 8
    tile = pl.BlockSpec((nseq * ls, D_POOL), lambda b: (b, 0))
    return pl.pallas_call(
        functools.partial(_pool_kernel, ls=ls, nseq=nseq, pos0=pos0), grid=(nb // nseq,),
        in_specs=[tile, pl.BlockSpec((nseq, 16, D_POOL), lambda b: (b, 0, 0)),
                  _layer_spec(w["wbd"], l, 1), _layer_spec(w["pscale"], l, 1)],
        out_specs=tile, out_shape=jax.ShapeDtypeStruct((nb * ls, D_POOL), BF16),
        compiler_params=_cparams(("parallel",)), name="pool_mixer",
    )(u, prefix, w["wbd"], w["pscale"])


def _ssd_chunk(xbc, z, sm, smt, ext_sc, st_sc, cw_ref, cb_ref, arow_ref, acol_ref, dsk_ref, gn_ref,
               row0, q, l_valid, live=None):
    ext_sc[SUBLANE:, :] = xbc
    w = cw_ref[...]
    conv = cb_ref[...]
    for j in range(CONV_W):
        conv = conv + w[j:j + 1] * ext_sc[pl.ds(SUBLANE - (CONV_W - 1) + j, q), :]
    ext_sc[:SUBLANE, :] = ext_sc[q:, :]
    act = _silu(conv)
    xs = act[:, :D_SSD]
    bbf = act[:, D_SSD:D_SSD + D_BC]
    bb = bbf.astype(BF16)
    cc = act[:, D_SSD + D_BC:].astype(BF16)
    bbt = bbf.T

    rvalid = (row0 + lax.broadcasted_iota(jnp.int32, (q, 1), 0)) < l_valid
    dtc = jnp.where(rvalid, sm, 0.0)
    acs_c = _scan_rows(dtc * (-jnp.exp(arow_ref[...])))
    cvalid = (row0 + lax.broadcasted_iota(jnp.int32, (1, LANE), 1)) < l_valid
    dtt = jnp.where(cvalid, smt, 0.0)
    acs_t = _scan_lanes(dtt * (-jnp.exp(acol_ref[...])))

    lane = lax.broadcasted_iota(jnp.int32, (1, LANE), 1)
    lo = lane < HD
    n_lo = lax.broadcasted_iota(jnp.int32, (LANE, 1), 0) < HD
    causal = (lax.broadcasted_iota(jnp.int32, (q, q), 1)
              <= lax.broadcasted_iota(jnp.int32, (q, q), 0))
    zc = jnp.zeros_like(cc)
    cb = [_nt(jnp.where(lo, cc, zc), bb), _nt(jnp.where(lo, zc, cc), bb)]
    ys = []
    for p in range(H // 2):
        cols = slice(p * LANE, (p + 1) * LANE)
        xp = xs[:, cols]
        xpb = xp.astype(BF16)
        st = st_sc[p]
        y_in = jnp.dot(cc, st.astype(BF16), preferred_element_type=F32)
        y_h, upd, dec = [], [], []
        for hh in range(2):
            h = 2 * p + hh
            g = h // (H // 2)
            arep = jnp.broadcast_to(acs_c[:, _DT_AT + h:_DT_AT + h + 1], (q, LANE))
            ak = acs_t[_DT_AT + h:_DT_AT + h + 1, :q]
            dtr = dtt[_DT_AT + h:_DT_AT + h + 1, :q]
            lm = jnp.exp(jnp.where(causal, arep[:, :q] - ak, NEG))
            gm = (cb[g] * lm * dtr).astype(BF16)
            y_h.append(jnp.dot(gm, xpb, preferred_element_type=F32) + y_in * jnp.exp(arep))
            alast = acs_t[_DT_AT + h:_DT_AT + h + 1, q - 1:q]
            bw = (bbt * (jnp.exp(alast - ak) * dtr)).astype(BF16)
            keep = (lo if hh == 0 else ~lo) & (n_lo if g == 0 else ~n_lo)
            upd.append(jnp.where(keep, jnp.dot(bw, xpb, preferred_element_type=F32), 0.0))
            dec.append(jnp.exp(alast))
        st_new = jnp.where(lo, dec[0], dec[1]) * st + upd[0] + upd[1]
        st_sc[p] = st_new if live is None else jnp.where(live, st_new, st)
        ys.append(jnp.where(lo, y_h[0], y_h[1]) + dsk_ref[:, cols] * xp)
    yc = jnp.concatenate(ys, axis=1) * _silu(z)
    return _rms(yc, gn_ref[...])


def _ssd_kernel(xbc_ref, z_ref, sm_ref, smt_ref, pre_ref, init_ref, cw_ref, cb_ref, arow_ref,
                acol_ref, dsk_ref, gn_ref, y_ref, fin_ref, ext_sc, st_sc, *, q, nseq):
    for s in range(nseq):
        rows = slice(s * q, (s + 1) * q)
        ext_sc[:SUBLANE, :] = pre_ref[s]
        st_sc[...] = init_ref[s]
        y = _ssd_chunk(xbc_ref[rows, :], z_ref[rows, :], sm_ref[rows, :], smt_ref[s], ext_sc, st_sc, cw_ref,
                       cb_ref, arow_ref, acol_ref, dsk_ref, gn_ref, 0, q, q)
        y_ref[rows, :] = y.astype(y_ref.dtype)
        fin_ref[s] = st_sc[...]


def _ssd(xbc, z, sm, smt, prefix, init, w, l, nb, q):
    nseq = 4
    rows = lambda c: pl.BlockSpec((nseq * q, c), lambda b: (b, 0))
    names = ("cw", "cb", "arow", "acol", "dsk", "gn")
    state = pl.BlockSpec((nseq, H // 2, LANE, LANE), lambda b: (b, 0, 0, 0))
    return pl.pallas_call(
        functools.partial(_ssd_kernel, q=q, nseq=nseq), grid=(nb // nseq,),
        in_specs=[rows(D_CONV), rows(D_SSD), rows(LANE), pl.BlockSpec((nseq, 16, LANE), lambda b: (b, 0, 0)),
                  pl.BlockSpec((nseq, 8, D_CONV), lambda b: (b, 0, 0)), state]
                 + [_layer_spec(w[n], l, 1) for n in names],
        out_specs=(rows(D_SSD), state),
        out_shape=(jax.ShapeDtypeStruct((nb * q, D_SSD), BF16),
                   jax.ShapeDtypeStruct((nb, H // 2, LANE, LANE), F32)),
        scratch_shapes=[pltpu.VMEM((q + SUBLANE, D_CONV), F32), pltpu.VMEM((H // 2, LANE, LANE), F32)],
        compiler_params=_cparams(("parallel",)), name="conv_ssd",
    )(xbc, z, sm, smt, prefix, init, *[w[n] for n in names])


def _state_to_wide(s):
    def placed(h):
        g, hh = h // (H // 2), h % 2
        return jnp.pad(jnp.swapaxes(s[:, h].astype(F32), 1, 2),
                       ((0, 0), (g * HD, LANE - (g + 1) * HD), (hh * HD, LANE - (hh + 1) * HD)))
    return jnp.stack([placed(2 * p) + placed(2 * p + 1) for p in range(H // 2)], axis=1)


def _state_from_wide(wide):
    return jnp.stack([jnp.swapaxes(wide[:, h // 2, (h // (H // 2)) * HD:(h // (H // 2) + 1) * HD,
                                        (h % 2) * HD:(h % 2 + 1) * HD], 1, 2) for h in range(H)], axis=1)


def _ffn_tile(x, a, p, s, wo_ref, g1_ref, g2_ref, g3_ref, wg_ref, wu_ref, wd_ref):
    mp = (jnp.dot(a, wo_ref[:D_FOX, :], preferred_element_type=F32)
          + jnp.dot(p, wo_ref[D_FOX:D_FOX + D_POOL, :], preferred_element_type=F32)
          + jnp.dot(s, wo_ref[D_FOX + D_POOL:, :], preferred_element_type=F32))
    x1 = x + _rms(mp, g1_ref[...])
    hb = _rms(x1, g2_ref[...]).astype(BF16)
    gate = jnp.dot(hb, wg_ref[...], preferred_element_type=F32)
    up = jnp.dot(hb, wu_ref[...], preferred_element_type=F32)
    act = (_silu(gate) * up).astype(BF16)
    ff = jnp.dot(act, wd_ref[...], preferred_element_type=F32)
    return x1 + _rms(ff, g3_ref[...])


def _ffn_kernel(x_ref, a_ref, p_ref, s_ref, *rest):
    o_ref = rest[-1]
    o_ref[...] = _ffn_tile(x_ref[...], a_ref[...], p_ref[...], s_ref[...], *rest[:-1])


def _mix_ffn_kernel(x_ref, a_ref, u_ref, xbc_ref, z_ref, sm_ref, smt_ref, wbd_ref, psc_ref,
                    cw_ref, cb_ref, arow_ref, acol_ref, dsk_ref, gn_ref,
                    wo_ref, g1_ref, g2_ref, g3_ref, wg_ref, wu_ref, wd_ref,
                    o_ref, fin_ref, halo_sc, ext_sc, st_sc, pool_sc, ssd_sc, *, tm, q, per, nt, l_valid):
    i = pl.program_id(0)
    live = i < nt
    ia = jnp.minimum(i, nt - 1)
    first = (ia % per) == 0
    row0 = (ia % per) * tm

    @pl.when(i == 0)
    def _():
        pool_sc[...] = jnp.zeros(pool_sc.shape, pool_sc.dtype)
        ssd_sc[...] = jnp.zeros(ssd_sc.shape, ssd_sc.dtype)

    o_ref[...] = _ffn_tile(x_ref[...], a_ref[...], pool_sc[...], ssd_sc[...], wo_ref, g1_ref, g2_ref, g3_ref,
                           wg_ref, wu_ref, wd_ref)

    u = u_ref[...]
    halo = jnp.where(first, 0.0, halo_sc[...])
    pool_sc[...] = _pool_tile(u, halo, wbd_ref, psc_ref, row0, tm).astype(pool_sc.dtype)
    halo_sc[...] = u[tm - 16:]
    ext_sc[:SUBLANE, :] = jnp.where(first, 0.0, ext_sc[:SUBLANE, :])
    st_sc[...] = jnp.where(first, 0.0, st_sc[...])
    for c in range(tm // q):
        rows = slice(c * q, (c + 1) * q)
        y = _ssd_chunk(xbc_ref[rows, :], z_ref[rows, :], sm_ref[rows, :], smt_ref[:, c * q:(c + 1) * q],
                       ext_sc, st_sc, cw_ref, cb_ref, arow_ref, acol_ref, dsk_ref, gn_ref,
                       row0 + c * q, q, l_valid, live)
        ssd_sc[rows, :] = y.astype(ssd_sc.dtype)
    fin_ref[0] = st_sc[...]


def _mix_ffn(x, attn, u, xbc, z, sm, smt, w, l, nb, tp, tm, q, l_valid):
    assert q == LANE and tm % q == 0
    per = tp // tm
    nt = nb * per
    prev = lambda c: pl.BlockSpec((tm, c), lambda i: (jnp.maximum(i - 1, 0), 0))
    cur = lambda c: pl.BlockSpec((tm, c), lambda i: (jnp.minimum(i, nt - 1), 0))
    names = ("wbd", "pscale", "cw", "cb", "arow", "acol", "dsk", "gn")
    big = ("wo", "g1", "g2", "g3", "wg", "wu", "wd")
    return pl.pallas_call(
        functools.partial(_mix_ffn_kernel, tm=tm, q=q, per=per, nt=nt, l_valid=l_valid), grid=(nt + 1,),
        in_specs=[prev(D_MODEL), prev(D_FOX), cur(D_POOL), cur(D_CONV), cur(D_SSD), cur(LANE),
                  pl.BlockSpec((16, tm), lambda i: (0, jnp.minimum(i, nt - 1)))]
                 + [_layer_spec(w[n], l, 1) for n in names]
                 + [_layer_spec(w[n], l, 1, single_buffer=True) for n in big],
        out_specs=(prev(D_MODEL),
                   pl.BlockSpec((1, H // 2, LANE, LANE), lambda i: (jnp.minimum(i, nt - 1) // per, 0, 0, 0))),
        out_shape=(jax.ShapeDtypeStruct((nt * tm, D_MODEL), F32),
                   jax.ShapeDtypeStruct((nb, H // 2, LANE, LANE), F32)),
        scratch_shapes=[pltpu.VMEM((16, D_POOL), F32), pltpu.VMEM((q + SUBLANE, D_CONV), F32),
                        pltpu.VMEM((H // 2, LANE, LANE), F32),
                        pltpu.VMEM((tm, D_POOL), BF16), pltpu.VMEM((tm, D_SSD), BF16)],
        compiler_params=_cparams(("arbitrary",)), name="mix_ffn",
    )(x, attn, u, xbc, z, sm, smt, *[w[n] for n in names], *[w[n] for n in big])


def _ffn(x, attn, pool, ssd, w, l, tm):
    t = x.shape[0]
    row = lambda c: pl.BlockSpec((tm, c), lambda i: (i, 0))
    names = ("wo", "g1", "g2", "g3", "wg", "wu", "wd")
    return pl.pallas_call(
        _ffn_kernel, grid=(t // tm,),
        in_specs=[row(D_MODEL), row(D_FOX), row(D_POOL), row(D_SSD)]
                 + [_layer_spec(w[n], l, 1, single_buffer=True) for n in names],
        out_specs=row(D_MODEL), out_shape=jax.ShapeDtypeStruct((t, D_MODEL), F32),
        compiler_params=_cparams(("parallel",)), name="outproj_ffn",
    )(x, attn, pool, ssd, *[w[n] for n in names])


def _lane_pack(f_vals, dt_vals):
    pieces, at = [], 0
    for start, vals in sorted([(_DT_AT, dt_vals)] + [(c, f_vals) for c in _F_COPIES], key=lambda t: t[0]):
        pieces += [jnp.zeros(vals.shape[:-1] + (start - at,), F32), vals.astype(F32)]
        at = start + H
    pieces.append(jnp.zeros(f_vals.shape[:-1] + (LANE - at,), F32))
    return jnp.concatenate(pieces, axis=-1)


def _prep_weights(w_in, fox_f_bias, pool_w, pool_scale, conv_w, conv_b, dt_bias, a_log, d_skip, ssd_norm,
                  w_out, w_gate, w_up, w_down, ln_pre_mix, ln_post_mix, ln_pre_ffn, ln_post_ffn):
    depth = w_in.shape[0]
    f0 = 3 * D_FOX
    u0 = f0 + H
    dt0 = u0 + D_POOL + D_SSD + D_CONV
    ws = _lane_pack(w_in[:, :, f0:u0], w_in[:, :, dt0:dt0 + H])
    wm = jnp.concatenate([w_in[:, :, :D_FOX], ws, w_in[:, :, u0:dt0]], axis=2).astype(BF16)
    ws = ws.astype(BF16)
    bs = _lane_pack(fox_f_bias, dt_bias)
    wbd = jnp.concatenate([jnp.pad(pool_w[:, g].astype(F32), ((0, 0), (0, 0), (g * 64, D_POOL - (g + 1) * 64)))
                           for g in range(D_POOL // 64)], axis=1)
    alog = _lane_pack(jnp.zeros_like(a_log), a_log)
    row = lambda a: a.astype(F32).reshape(depth, 1, -1)
    tr = lambda a: jnp.transpose(a, (0, 2, 1))
    return dict(
        g_pre=row(ln_pre_mix), wm=wm, wkv=w_in[:, :, D_FOX:f0].astype(BF16),
        wkt=tr(w_in[:, :, D_FOX:2 * D_FOX]).astype(BF16), wvt=tr(w_in[:, :, 2 * D_FOX:f0]).astype(BF16),
        wst=tr(ws[:, :, :16]),
        bs=bs.reshape(depth, 1, LANE), bst=bs[:, :16].reshape(depth, 16, 1),
        wbd=wbd.astype(BF16), pscale=row(pool_scale),
        cw=jnp.pad(conv_w.astype(F32), ((0, 0), (0, SUBLANE - CONV_W), (0, 0))), cb=row(conv_b),
        arow=alog.reshape(depth, 1, LANE), acol=alog[:, :16].reshape(depth, 16, 1),
        dsk=row(jnp.repeat(d_skip, HD, axis=1)), gn=row(ssd_norm),
        wo=w_out.astype(BF16), g1=row(ln_post_mix), g2=row(ln_pre_ffn), g3=row(ln_post_ffn),
        wg=w_gate.astype(BF16), wu=w_up.astype(BF16), wd=w_down.astype(BF16))


def _prompt_layer(x, w, l, nb, tp, lp, tm, q, kv_stacks):
    qb, kb, kt, vt, vtb, eb, u, z, xbc, sm, smt = _inproj(x, w, l, tm, (nb, tp, lp), kv_stacks)
    attn = _fox_prompt(qb, kb, eb, vtb, nb, tp, tm)
    x, fin = _mix_ffn(x, attn, u, xbc, z, sm, smt, w, l, nb, tp, tm, q, lp)
    return x, (kt, vt), (smt, u, xbc, fin)


def _sample_layer(x, w, l, nb, ls, past, tm, kct, vct, lfc, pool_pre, conv_pre, ssd_init):
    qb, k, kb, v, vb, u, z, xbc, sm, smt = _inproj(x, w, l, tm)
    smt_seq = jnp.transpose(smt.reshape(16, nb, ls), (1, 0, 2))
    lf_all = jnp.concatenate([lfc, smt_seq[:, :8], jnp.zeros((nb, 8, LANE - ls), F32)], axis=2)
    attn = _fox_sample(qb, kb, vb, kct, vct, lf_all, l, nb, ls, past)
    pool = _pool(u, pool_pre, w, l, nb, ls, past)
    ssd, fin = _ssd(xbc, z, sm, jnp.pad(smt_seq, ((0, 0), (0, 0), (0, LANE - ls))), conv_pre, ssd_init, w, l,
                    nb, ls)
    x = _ffn(x, attn, pool, ssd, w, l, tm)
    return x, (k, v), (smt, u, xbc, fin)


def kernel(x_prompt, x_sample, cache_fox_k, cache_fox_v, cache_fox_logf, state_pool, state_conv, state_ssd,
           meta_tokens, ln_pre_mix, ln_post_mix, ln_pre_ffn, ln_post_ffn, w_in, fox_f_bias, pool_w, pool_scale,
           conv_w, conv_b, dt_bias, a_log, d_skip, ssd_norm, w_out, w_gate, w_up, w_down):
    nbp, seq, _ = x_prompt.shape
    nbs, ls, _ = x_sample.shape
    depth, _, past = cache_fox_logf.shape[:3]
    lp = N_META + seq
    tq = 256
    q_ssd = 128
    tp = -(-lp // tq) * tq

    w = _prep_weights(w_in, fox_f_bias, pool_w, pool_scale, conv_w, conv_b, dt_bias, a_log, d_skip, ssd_norm,
                      w_out, w_gate, w_up, w_down, ln_pre_mix, ln_post_mix, ln_pre_ffn, ln_post_ffn)

    meta = jnp.broadcast_to(meta_tokens.astype(F32)[None], (nbp, N_META, D_MODEL))
    xp = jnp.concatenate([meta, x_prompt, jnp.zeros((nbp, tp - lp, D_MODEL), F32)], axis=1)
    xp = xp.reshape(nbp * tp, D_MODEL)
    xs = x_sample.reshape(nbs * ls, D_MODEL)

    ssd_init =_state_to_wide(state_ssd.reshape(depth * nbs, H, HD, HD)).reshape(depth, nbs, H // 2, LANE, LANE)

    kct = jnp.transpose(cache_fox_k, (0, 1, 3, 4, 2)).reshape(depth, nbs, D_FOX, past)
    vct = jnp.transpose(cache_fox_v, (0, 1, 3, 4, 2)).reshape(depth, nbs, D_FOX, past)
    lfc = jnp.pad(jnp.transpose(cache_fox_logf.astype(F32), (0, 1, 3, 2)),
                  ((0, 0), (0, 0), (0, 8 - H), (0, 0)))
    pool_pre = jnp.pad(state_pool.astype(F32), ((0, 0), (0, 0), (16 - POOL_BUF, 0), (0, 0)))
    conv_pre = jnp.pad(state_conv.astype(F32), ((0, 0), (0, 0), (8 - (CONV_W - 1), 0), (0, 0)))

    outs_p, outs_s, kv_s = [], [], []
    kv_stacks = tuple(jnp.zeros((depth, nbp, D_FOX, lp), F32) for _ in range(2))
    for l in range(depth):
        xp, kv_stacks, st = _prompt_layer(xp, w, l, nbp, tp, lp, tq, q_ssd, kv_stacks)
        outs_p.append(st)
        xs, kv, st = _sample_layer(xs, w, l, nbs, ls, past, tq, kct, vct, lfc[l], pool_pre[l], conv_pre[l],
                                   ssd_init[l])
        outs_s.append(st)
        kv_s.append(kv)

    def tails(outs, nb, rows, valid):
        seqv = lambda a: a.reshape(nb, rows, a.shape[-1])
        lf = jnp.stack([jnp.transpose(o[0][:H].reshape(H, nb, rows)[:, :, :valid], (1, 2, 0)) for o in outs])
        pn = jnp.stack([seqv(o[1])[:, valid - POOL_BUF:valid] for o in outs])
        cn = jnp.stack([seqv(o[2])[:, valid - (CONV_W - 1):valid] for o in outs])
        sn = jnp.stack([_state_from_wide(o[3]) for o in outs])
        return lf, pn, cn, sn

    def token_minor(a):
        return jnp.transpose(a.reshape(depth, nbp, H, HD, lp), (0, 1, 4, 2, 3))

    def token_major(i):
        return jnp.stack([kv[i].reshape(nbs, ls, H, HD) for kv in kv_s])

    y_prompt = xp.reshape(nbp, tp, D_MODEL)[:, N_META:lp]
    y_sample = xs.reshape(nbs, ls, D_MODEL)
    return ((y_prompt, y_sample, token_minor(kv_stacks[0]), token_minor(kv_stacks[1])) + tails(outs_p, nbp, tp, lp)
            + (token_major(0), token_major(1)) + tails(outs_s, nbs, ls, ls))
```

```python
import functools
import math

import jax
import jax.numpy as jnp
from jax import lax
from jax.experimental import pallas as pl
from jax.experimental.pallas import tpu as pltpu

F32 = jnp.float32
BF16 = jnp.bfloat16

D_MODEL = 1024
N_META = 16
EPS = 1e-6
H = 6
HD = 64
D_FOX = H * HD
D_POOL = 256
POOL_BUF = 15
D_SSD = H * HD
D_BC = 128
D_CONV = D_SSD + 2 * D_BC
CONV_W = 4
D_FF = 2816
LANE = 128
SUBLANE = 8
NEG = -1e30
LOG2E = math.log2(math.e)

_Q0, _S0, _U0, _Z0, _X0, _MAIN = 0, 384, 512, 768, 1152, 1792
_MXU_N = 256
_GROUPS = ((_Q0, _U0), (_U0, _Z0), (_Z0, _X0 + LANE), (_X0 + LANE, _MAIN))
_F_AT, _DT_AT, _F_COPIES = 0, 8, (0, 16, 32)
_VROWS = HD + SUBLANE

_VMEM_LIMIT = 56 * 1024 * 1024


def _cparams(sem):
    return pltpu.CompilerParams(dimension_semantics=sem, vmem_limit_bytes=_VMEM_LIMIT)


def _layer_spec(a, l, ngrid, single_buffer=False):
    idx = (l,) + (0,) * (a.ndim - 1)
    kw = dict(pipeline_mode=pl.Buffered(1)) if single_buffer else {}
    return pl.BlockSpec((None,) + a.shape[1:], lambda *_: idx, **kw)


def _rms(x, g):
    ms = jnp.mean(x * x, axis=-1, keepdims=True)
    return x * lax.rsqrt(ms + EPS) * g


def _silu(x):
    return x * jax.nn.sigmoid(x)


def _softplus_tail(x):
    return jnp.log1p(jnp.exp(-jnp.abs(x)))


def _nt(a, b):
    return lax.dot_general(a, b, (((1,), (1,)), ((), ())), preferred_element_type=F32)


def _tn(a, b):
    return lax.dot_general(a, b, (((0,), (0,)), ((), ())), preferred_element_type=F32)


def _scan_lanes(x):
    lane = lax.broadcasted_iota(jnp.int32, x.shape, 1)
    s = 1
    while s < x.shape[1]:
        x = x + jnp.where(lane >= s, pltpu.roll(x, s, 1), 0.0)
        s *= 2
    return x


def _scan_rows(x):
    row = lax.broadcasted_iota(jnp.int32, x.shape, 0)
    s = 1
    while s < x.shape[0]:
        x = x + jnp.where(row >= s, pltpu.roll(x, s, 0), 0.0)
        s *= 2
    return x


def _inproj_kernel(x_ref, g_ref, wm_ref, wst_ref, bs_ref, bst_ref, *rest, prompt, qscale, per):
    hb = _rms(x_ref[...], g_ref[...]).astype(BF16)
    hbt = hb.T

    if prompt:
        (wkt_ref, wvt_ref, _, _, qb_ref, kb_ref, kt_ref, vt_ref, vtb_ref, e_ref, u_ref, z_ref, xbc_ref, sm_ref,
         smt_ref, carry_sc) = rest
        kt = jnp.dot(wkt_ref[...], hbt, preferred_element_type=F32)
        kt_ref[...] = kt
        kb_ref[...] = kt.T.astype(BF16)
        vt = jnp.dot(wvt_ref[...], hbt, preferred_element_type=F32)
        vt_ref[...] = vt
        tm = vt.shape[1]
        ones_row = jnp.where(lax.broadcasted_iota(jnp.int32, (SUBLANE, tm), 0) == 0, 1.0, 0.0).astype(BF16)
        for h in range(H):
            vtb_ref[0, h * _VROWS:h * _VROWS + HD, :] = vt[h * HD:(h + 1) * HD].astype(BF16)
            vtb_ref[0, h * _VROWS + HD:(h + 1) * _VROWS, :] = ones_row
    else:
        wkv_ref, qb_ref, k_ref, kb_ref, v_ref, vb_ref, u_ref, z_ref, xbc_ref, sm_ref, smt_ref = rest
        kv = jnp.dot(hb, wkv_ref[...], preferred_element_type=F32)
        k_ref[...] = kv[:, :D_FOX]
        kb_ref[...] = kv[:, :D_FOX].astype(BF16)
        v_ref[...] = kv[:, D_FOX:]
        vb_ref[...] = kv[:, D_FOX:].astype(BF16)
    g0, g1, g2, g3 = (jnp.dot(hb, wm_ref[:, a:b], preferred_element_type=F32) for a, b in _GROUPS)
    qb_ref[...] = (g0[:, :_S0] * qscale).astype(BF16)
    u_ref[...] = g1
    z_ref[...] = g2[:, :_X0 - _Z0]
    xbc_ref[:, :LANE] = g2[:, _X0 - _Z0:]
    xbc_ref[:, LANE:] = g3

    sm = g0[:, _S0:] + bs_ref[...]
    lane = lax.broadcasted_iota(jnp.int32, sm.shape, 1)
    tail = _softplus_tail(sm)
    is_dt = (lane >= _DT_AT) & (lane < _DT_AT + 8)
    sm = jnp.where(is_dt, jnp.maximum(sm, 0.0) + tail, jnp.minimum(sm, 0.0) - tail)
    sm_ref[...] = sm
    if prompt:
        first = (pl.program_id(0) % per) == 0
        c = _scan_rows(sm) + jnp.where(first, 0.0, carry_sc[...])
        carry_sc[...] = c[-1:, :]
        lane1 = lane[:1]
        c2 = c * LOG2E
        hi = c2.astype(BF16).astype(F32)
        mid = (c2 - hi).astype(BF16).astype(F32)
        e = jnp.where(lane1 < 16, hi, jnp.where(lane1 < 32, mid, (c2 - hi) - mid))
        e_ref[...] = jnp.where(((lane1 & 15) < H) & (lane1 < 48), e, 0.0).astype(BF16)
    smt = jnp.dot(wst_ref[...], hbt, preferred_element_type=F32) + bst_ref[...]
    row = lax.broadcasted_iota(jnp.int32, smt.shape, 0)
    tail = _softplus_tail(smt)
    smt_ref[...] = jnp.where(row < _DT_AT, jnp.minimum(smt, 0.0) - tail, jnp.maximum(smt, 0.0) + tail)


def _inproj(x, w, l, tm, prompt_dims=None, kv_stacks=()):
    t = x.shape[0]
    nt = t // tm
    row = lambda c: pl.BlockSpec((tm, c), lambda i: (i, 0))
    sds = jax.ShapeDtypeStruct
    tail_shapes = (sds((t, D_POOL), F32), sds((t, D_SSD), F32), sds((t, D_CONV), F32),
                   sds((t, LANE), F32), sds((16, t), F32))
    tail_specs = (row(D_POOL), row(D_SSD), row(D_CONV), row(LANE), pl.BlockSpec((16, tm), lambda i: (0, i)))
    if prompt_dims is not None:
        nb, tp, lp = prompt_dims
        per = tp // tm
        tmin = pl.BlockSpec((None, None, D_FOX, tm), lambda i: (l, i // per, 0, i % per))
        out_shape = (sds((t, D_FOX), BF16), sds((t, D_FOX), BF16), sds(kv_stacks[0].shape, F32),
                     sds(kv_stacks[1].shape, F32), sds((nt, H * _VROWS, tm), BF16),
                     sds((t, LANE), BF16)) + tail_shapes
        out_specs = (row(D_FOX), row(D_FOX), tmin, tmin,
                     pl.BlockSpec((1, H * _VROWS, tm), lambda i: (i, 0, 0)), row(LANE)) + tail_specs
        qscale = (HD ** -0.5) * LOG2E
        scratch = [pltpu.VMEM((1, LANE), F32)]
    else:
        per = None
        out_shape = (sds((t, D_FOX), BF16), sds((t, D_FOX), F32), sds((t, D_FOX), BF16),
                     sds((t, D_FOX), F32), sds((t, D_FOX), BF16)) + tail_shapes
        out_specs = (row(D_FOX),) * 5 + tail_specs
        qscale = HD ** -0.5
        scratch = []
    names = ("g_pre", "wm", "wst", "bs", "bst") + (("wkt", "wvt") if prompt_dims is not None else ("wkv",))
    n_in = 1 + len(names)
    return pl.pallas_call(
        functools.partial(_inproj_kernel, prompt=prompt_dims is not None, qscale=qscale, per=per), grid=(nt,),
        in_specs=[row(D_MODEL)] + [_layer_spec(w[n], l, 1) for n in names]
                 + [pl.BlockSpec(memory_space=pl.ANY)] * len(kv_stacks),
        out_specs=out_specs, out_shape=out_shape, scratch_shapes=scratch,
        input_output_aliases={n_in + s: 2 + s for s in range(len(kv_stacks))},
        compiler_params=_cparams(("arbitrary",)), name="inproj",
    )(x, *[w[n] for n in names], *kv_stacks)


def _fox_prompt_kernel(q_ref, k_ref, e_ref, vt_ref, o_ref, qa_sc, m_sc, acc_sc, sa_sc, sb_sc, *, tq):
    i = pl.program_id(1)
    causal = (lax.broadcasted_iota(jnp.int32, (tq, tq), 0)
              <= lax.broadcasted_iota(jnp.int32, (tq, tq), 1))
    feat = lax.broadcasted_iota(jnp.int32, (LANE, 1), 0)
    for p in range(H // 2):
        qt = q_ref[:, p * LANE:(p + 1) * LANE].astype(F32).T
        for h in (2 * p, 2 * p + 1):
            mine = (feat < HD) if h % 2 == 0 else (feat >= HD)
            pick = (feat == _F_COPIES[0] + h) | (feat == _F_COPIES[1] + h) | (feat == _F_COPIES[2] + h)
            qa_sc[h, :LANE, :] = jnp.where(mine, qt, 0.0).astype(BF16)
            qa_sc[h, LANE:, :] = jnp.broadcast_to(jnp.where(pick, -1.0, 0.0), (LANE, tq)).astype(BF16)
    m_sc[...] = jnp.full(m_sc.shape, NEG, F32)
    acc_sc[...] = jnp.zeros(acc_sc.shape, F32)

    def logits_h(j, s_sc, h):
        rows = pl.ds(pl.multiple_of(j * tq, tq), tq)
        ka = jnp.concatenate([k_ref[rows, (h // 2) * LANE:(h // 2 + 1) * LANE], e_ref[rows, :]], axis=1)
        s_sc[h] = jnp.dot(ka, qa_sc[h], preferred_element_type=F32)

    def softmax_pv_h(j, s_sc, masked, h):
        s = s_sc[h]
        if masked:
            s = jnp.where(causal, s, NEG)
        m_old = m_sc[h:h + 1, :]
        m_new = jnp.maximum(m_old, jnp.max(s, axis=0, keepdims=True))
        m_sc[h:h + 1, :] = m_new
        pm = jnp.exp2(s - m_new).astype(BF16)
        hs = slice(h * _VROWS, (h + 1) * _VROWS)
        pv = jnp.dot(vt_ref[j, hs, :], pm, preferred_element_type=F32)
        acc_sc[hs, :] = jnp.exp2(m_old - m_new) * acc_sc[hs, :] + pv

    def logits(j, s_sc):
        for h in range(H):
            logits_h(j, s_sc, h)

    def softmax_pv(j, s_sc, masked):
        for h in range(H):
            softmax_pv_h(j, s_sc, masked, h)

    def two_steps(j):
        logits(j + 1, sb_sc)
        softmax_pv(j, sa_sc, False)
        logits(j + 2, sa_sc)
        softmax_pv(j + 1, sb_sc, False)

    def body4(jj, carry):
        two_steps(4 * jj)
        two_steps(4 * jj + 2)
        return carry

    def body2(jj, carry):
        two_steps(4 * (i // 4) + 2 * jj)
        return carry

    logits(0, sa_sc)
    lax.fori_loop(0, i // 4, body4, 0)
    lax.fori_loop(0, (i % 4) // 2, body2, 0)

    @pl.when(i % 2 == 0)
    def _():
        softmax_pv(i, sa_sc, True)

    @pl.when(i % 2 == 1)
    def _():
        logits(i, sb_sc)
        softmax_pv(i - 1, sa_sc, False)
        softmax_pv(i, sb_sc, True)

    out = jnp.concatenate([acc_sc[h * _VROWS:h * _VROWS + HD, :] / acc_sc[h * _VROWS + HD:h * _VROWS + HD + 1, :]
                           for h in range(H)], axis=0)
    o_ref[...] = out.T.astype(o_ref.dtype)


def _fox_prompt(qb, kb, eb, vt, nb, tp, tq):
    nq = tp // tq
    return pl.pallas_call(
        functools.partial(_fox_prompt_kernel, tq=tq), grid=(nb, nq),
        in_specs=[pl.BlockSpec((tq, D_FOX), lambda b, i: (b * nq + i, 0)),
                  pl.BlockSpec((tp, D_FOX), lambda b, i: (b, 0)),
                  pl.BlockSpec((tp, LANE), lambda b, i: (b, 0)),
                  pl.BlockSpec((nq, H * _VROWS, tq), lambda b, i: (b, 0, 0))],
        out_specs=pl.BlockSpec((tq, D_FOX), lambda b, i: (b * nq + i, 0)),
        out_shape=jax.ShapeDtypeStruct((nb * tp, D_FOX), BF16),
        scratch_shapes=[pltpu.VMEM((H, 2 * LANE, tq), BF16), pltpu.VMEM((8, tq), F32),
                        pltpu.VMEM((H * _VROWS, tq), F32),
                        pltpu.VMEM((H, tq, tq), F32), pltpu.VMEM((H, tq, tq), F32)],
        compiler_params=_cparams(("parallel", "arbitrary")), name="fox_prompt",
    )(qb, kb, eb, vt)


def _fox_sample_kernel(q_ref, kn_ref, vn_ref, kct_ref, vct_ref, lf_ref, o_ref, *, past, ls, nseq):
    for s in range(nseq):
        rows = slice(s * ls, (s + 1) * ls)
        _fox_sample_one(q_ref.at[rows], kn_ref.at[rows], vn_ref.at[rows], kct_ref.at[s], vct_ref.at[s],
                        lf_ref.at[s], o_ref.at[rows], past, ls)


def _fox_sample_one(q_ref, kn_ref, vn_ref, kct_ref, vct_ref, lf_ref, o_ref, past, ls):
    causal = (lax.broadcasted_iota(jnp.int32, (ls, ls), 1)
              <= lax.broadcasted_iota(jnp.int32, (ls, ls), 0))
    nblk = lf_ref.shape[1] // LANE
    loc = _scan_lanes(jnp.concatenate([lf_ref[:, j * LANE:(j + 1) * LANE] for j in range(nblk)], axis=0))
    carry = jnp.zeros((8, 1), F32)
    cs = []
    for j in range(nblk):
        blk = loc[j * 8:(j + 1) * 8]
        cs.append(blk + carry)
        carry = carry + blk[:, LANE - 1:LANE]
    c_all = jnp.concatenate(cs, axis=1)

    q = q_ref[...]
    lane = lax.broadcasted_iota(jnp.int32, (1, D_FOX), 1)
    own = [(lane >= h * HD) & (lane < (h + 1) * HD) for h in range(H)]
    q_all = jnp.concatenate([jnp.where(own[h], q, jnp.zeros_like(q)) for h in range(H)], axis=0)
    bias = [c_all[h:h + 1, past - 1:past] - c_all[h:h + 1, :] for h in range(H)]
    bias_c = jnp.concatenate([jnp.broadcast_to(b[:, :past], (ls, past)) for b in bias], axis=0)
    bias_n = jnp.concatenate([jnp.broadcast_to(b[:, past:past + ls], (ls, ls)) for b in bias], axis=0)
    s_c = jnp.dot(q_all, kct_ref[...].astype(BF16), preferred_element_type=F32) + bias_c
    s_n = jnp.where(jnp.concatenate([causal] * H, axis=0), _nt(q_all, kn_ref[...]) + bias_n, NEG)
    m = jnp.maximum(jnp.max(s_c, axis=-1, keepdims=True), jnp.max(s_n, axis=-1, keepdims=True))
    p_c = jnp.exp(s_c - m)
    p_n = jnp.exp(s_n - m)
    den = jnp.sum(p_c, axis=-1, keepdims=True) + jnp.sum(p_n, axis=-1, keepdims=True)
    o_all = (_nt(p_c.astype(BF16), vct_ref[...].astype(BF16))
             + jnp.dot(p_n.astype(BF16), vn_ref[...], preferred_element_type=F32)) / den
    o = jnp.zeros((ls, D_FOX), F32)
    for h in range(H):
        o = jnp.where(own[h], o_all[h * ls:(h + 1) * ls], o)
    o_ref[...] = o.astype(o_ref.dtype)


def _fox_sample(qb, kb, vb, kct, vct, lf_all, l, nb, ls, past):
    nseq = 4
    new = pl.BlockSpec((nseq * ls, D_FOX), lambda b: (b, 0))
    cache = pl.BlockSpec((None, nseq, D_FOX, past), lambda b: (l, b, 0, 0))
    return pl.pallas_call(
        functools.partial(_fox_sample_kernel, past=past, ls=ls, nseq=nseq), grid=(nb // nseq,),
        in_specs=[new, new, new, cache, cache, pl.BlockSpec((nseq, 8, past + LANE), lambda b: (b, 0, 0))],
        out_specs=new, out_shape=jax.ShapeDtypeStruct((nb * ls, D_FOX), BF16),
        compiler_params=_cparams(("parallel",)), name="fox_sample",
    )(qb, kb, vb, kct, vct, lf_all)


def _pool_tile(u, halo, w_ref, sc_ref, pos_start, tm):
    a = jnp.concatenate([halo, u], axis=0)
    e1 = a + pltpu.roll(a, 1, 0)
    e2 = e1 + pltpu.roll(e1, 2, 0)
    e3 = e2 + pltpu.roll(e2, 4, 0)
    e4 = e3 + pltpu.roll(e3, 8, 0)
    lane = lax.broadcasted_iota(jnp.int32, (1, D_POOL), 1)
    win = jnp.where(lane < 64, e1, jnp.where(lane < 128, e2, jnp.where(lane < 192, e3, e4)))[16:]
    wsz = jnp.where(lane < 64, 2.0, jnp.where(lane < 128, 4.0, jnp.where(lane < 192, 8.0, 16.0)))
    pos = (pos_start + lax.broadcasted_iota(jnp.int32, (tm, 1), 0)).astype(F32)
    diff = win / jnp.minimum(pos + 1.0, wsz) - u
    return jnp.dot(diff.astype(BF16), w_ref[...], preferred_element_type=F32) * sc_ref[...]


def _pool_kernel(u_ref, pre_ref, w_ref, sc_ref, o_ref, *, ls, nseq, pos0):
    for s in range(nseq):
        rows = slice(s * ls, (s + 1) * ls)
        o_ref[rows, :] = _pool_tile(u_ref[rows, :], pre_ref[s], w_ref, sc_ref, pos0, ls).astype(o_ref.dtype)


def _pool(u, prefix, w, l, nb, ls, pos0):
    nseq = 8
    tile = pl.BlockSpec((nseq * ls, D_POOL), lambda b: (b, 0))
    return pl.pallas_call(
        functools.partial(_pool_kernel, ls=ls, nseq=nseq, pos0=pos0), grid=(nb // nseq,),
        in_specs=[tile, pl.BlockSpec((nseq, 16, D_POOL), lambda b: (b, 0, 0)),
                  _layer_spec(w["wbd"], l, 1), _layer_spec(w["pscale"], l, 1)],
        out_specs=tile, out_shape=jax.ShapeDtypeStruct((nb * ls, D_POOL), BF16),
        compiler_params=_cparams(("parallel",)), name="pool_mixer",
    )(u, prefix, w["wbd"], w["pscale"])


def _ssd_chunk(xbc, z, sm, smt, ext_sc, st_sc, cw_ref, cb_ref, arow_ref, acol_ref, dsk_ref, gn_ref,
               row0, q, l_valid, live=None):
    ext_sc[SUBLANE:, :] = xbc
    w = cw_ref[...]
    conv = cb_ref[...]
    for j in range(CONV_W):
        conv = conv + w[j:j + 1] * ext_sc[pl.ds(SUBLANE - (CONV_W - 1) + j, q), :]
    ext_sc[:SUBLANE, :] = ext_sc[q:, :]
    act = _silu(conv)
    xs = act[:, :D_SSD]
    bbf = act[:, D_SSD:D_SSD + D_BC]
    bb = bbf.astype(BF16)
    cc = act[:, D_SSD + D_BC:].astype(BF16)
    bbt = bbf.T

    rvalid = (row0 + lax.broadcasted_iota(jnp.int32, (q, 1), 0)) < l_valid
    dtc = jnp.where(rvalid, sm, 0.0)
    acs_c = _scan_rows(dtc * (-jnp.exp(arow_ref[...])))
    cvalid = (row0 + lax.broadcasted_iota(jnp.int32, (1, LANE), 1)) < l_valid
    dtt = jnp.where(cvalid, smt, 0.0)
    acs_t = _scan_lanes(dtt * (-jnp.exp(acol_ref[...])))

    lane = lax.broadcasted_iota(jnp.int32, (1, LANE), 1)
    lo = lane < HD
    n_lo = lax.broadcasted_iota(jnp.int32, (LANE, 1), 0) < HD
    causal = (lax.broadcasted_iota(jnp.int32, (q, q), 1)
              <= lax.broadcasted_iota(jnp.int32, (q, q), 0))
    zc = jnp.zeros_like(cc)
    cb = [_nt(jnp.where(lo, cc, zc), bb), _nt(jnp.where(lo, zc, cc), bb)]
    ys = []
    for p in range(H // 2):
        cols = slice(p * LANE, (p + 1) * LANE)
        xp = xs[:, cols]
        xpb = xp.astype(BF16)
        st = st_sc[p]
        y_in = jnp.dot(cc, st.astype(BF16), preferred_element_type=F32)
        y_h, upd, dec = [], [], []
        for hh in range(2):
            h = 2 * p + hh
            g = h // (H // 2)
            arep = jnp.broadcast_to(acs_c[:, _DT_AT + h:_DT_AT + h + 1], (q, LANE))
            ak = acs_t[_DT_AT + h:_DT_AT + h + 1, :q]
            dtr = dtt[_DT_AT + h:_DT_AT + h + 1, :q]
            lm = jnp.exp(jnp.where(causal, arep[:, :q] - ak, NEG))
            gm = (cb[g] * lm * dtr).astype(BF16)
            y_h.append(jnp.dot(gm, xpb, preferred_element_type=F32) + y_in * jnp.exp(arep))
            alast = acs_t[_DT_AT + h:_DT_AT + h + 1, q - 1:q]
            bw = (bbt * (jnp.exp(alast - ak) * dtr)).astype(BF16)
            keep = (lo if hh == 0 else ~lo) & (n_lo if g == 0 else ~n_lo)
            upd.append(jnp.where(keep, jnp.dot(bw, xpb, preferred_element_type=F32), 0.0))
            dec.append(jnp.exp(alast))
        st_new = jnp.where(lo, dec[0], dec[1]) * st + upd[0] + upd[1]
        st_sc[p] = st_new if live is None else jnp.where(live, st_new, st)
        ys.append(jnp.where(lo, y_h[0], y_h[1]) + dsk_ref[:, cols] * xp)
    yc = jnp.concatenate(ys, axis=1) * _silu(z)
    return _rms(yc, gn_ref[...])


def _ssd_kernel(xbc_ref, z_ref, sm_ref, smt_ref, pre_ref, init_ref, cw_ref, cb_ref, arow_ref,
                acol_ref, dsk_ref, gn_ref, y_ref, fin_ref, ext_sc, st_sc, *, q, nseq):
    for s in range(nseq):
        rows = slice(s * q, (s + 1) * q)
        ext_sc[:SUBLANE, :] = pre_ref[s]
        st_sc[...] = init_ref[s]
        y = _ssd_chunk(xbc_ref[rows, :], z_ref[rows, :], sm_ref[rows, :], smt_ref[s], ext_sc, st_sc, cw_ref,
                       cb_ref, arow_ref, acol_ref, dsk_ref, gn_ref, 0, q, q)
        y_ref[rows, :] = y.astype(y_ref.dtype)
        fin_ref[s] = st_sc[...]


def _ssd(xbc, z, sm, smt, prefix, init, w, l, nb, q):
    nseq = 4
    rows = lambda c: pl.BlockSpec((nseq * q, c), lambda b: (b, 0))
    names = ("cw", "cb", "arow", "acol", "dsk", "gn")
    state = pl.BlockSpec((nseq, H // 2, LANE, LANE), lambda b: (b, 0, 0, 0))
    return pl.pallas_call(
        functools.partial(_ssd_kernel, q=q, nseq=nseq), grid=(nb // nseq,),
        in_specs=[rows(D_CONV), rows(D_SSD), rows(LANE), pl.BlockSpec((nseq, 16, LANE), lambda b: (b, 0, 0)),
                  pl.BlockSpec((nseq, 8, D_CONV), lambda b: (b, 0, 0)), state]
                 + [_layer_spec(w[n], l, 1) for n in names],
        out_specs=(rows(D_SSD), state),
        out_shape=(jax.ShapeDtypeStruct((nb * q, D_SSD), BF16),
                   jax.ShapeDtypeStruct((nb, H // 2, LANE, LANE), F32)),
        scratch_shapes=[pltpu.VMEM((q + SUBLANE, D_CONV), F32), pltpu.VMEM((H // 2, LANE, LANE), F32)],
        compiler_params=_cparams(("parallel",)), name="conv_ssd",
    )(xbc, z, sm, smt, prefix, init, *[w[n] for n in names])


def _state_to_wide(s):
    def placed(h):
        g, hh = h // (H // 2), h % 2
        return jnp.pad(jnp.swapaxes(s[:, h].astype(F32), 1, 2),
                       ((0, 0), (g * HD, LANE - (g + 1) * HD), (hh * HD, LANE - (hh + 1) * HD)))
    return jnp.stack([placed(2 * p) + placed(2 * p + 1) for p in range(H // 2)], axis=1)


def _state_from_wide(wide):
    return jnp.stack([jnp.swapaxes(wide[:, h // 2, (h // (H // 2)) * HD:(h // (H // 2) + 1) * HD,
                                        (h % 2) * HD:(h % 2 + 1) * HD], 1, 2) for h in range(H)], axis=1)


def _ffn_tile(x, a, p, s, wo_ref, g1_ref, g2_ref, g3_ref, wg_ref, wu_ref, wd_ref):
    mp = (jnp.dot(a, wo_ref[:D_FOX, :], preferred_element_type=F32)
          + jnp.dot(p, wo_ref[D_FOX:D_FOX + D_POOL, :], preferred_element_type=F32)
          + jnp.dot(s, wo_ref[D_FOX + D_POOL:, :], preferred_element_type=F32))
    x1 = x + _rms(mp, g1_ref[...])
    hb = _rms(x1, g2_ref[...]).astype(BF16)
    gate = jnp.dot(hb, wg_ref[...], preferred_element_type=F32)
    up = jnp.dot(hb, wu_ref[...], preferred_element_type=F32)
    act = (_silu(gate) * up).astype(BF16)
    ff = jnp.dot(act, wd_ref[...], preferred_element_type=F32)
    return x1 + _rms(ff, g3_ref[...])


def _ffn_kernel(x_ref, a_ref, p_ref, s_ref, *rest):
    o_ref = rest[-1]
    o_ref[...] = _ffn_tile(x_ref[...], a_ref[...], p_ref[...], s_ref[...], *rest[:-1])


def _mix_ffn_kernel(x_ref, a_ref, u_ref, xbc_ref, z_ref, sm_ref, smt_ref, wbd_ref, psc_ref,
                    cw_ref, cb_ref, arow_ref, acol_ref, dsk_ref, gn_ref,
                    wo_ref, g1_ref, g2_ref, g3_ref, wg_ref, wu_ref, wd_ref,
                    o_ref, fin_ref, halo_sc, ext_sc, st_sc, pool_sc, ssd_sc, *, tm, nsub, q, per, ng, l_valid):
    i = pl.program_id(0)
    live = i < ng
    ic = jnp.minimum(i, ng - 1)

    @pl.when(i == 0)
    def _():
        pool_sc[...] = jnp.zeros(pool_sc.shape, pool_sc.dtype)
        ssd_sc[...] = jnp.zeros(ssd_sc.shape, ssd_sc.dtype)

    o_ref[...] = _ffn_tile(x_ref[...], a_ref[...], pool_sc[...], ssd_sc[...], wo_ref, g1_ref, g2_ref, g3_ref,
                           wg_ref, wu_ref, wd_ref)

    for s in range(nsub):
        t = ic * nsub + s
        first = (t % per) == 0
        row0 = (t % per) * tm
        tile = slice(s * tm, (s + 1) * tm)
        u = u_ref[tile, :]
        halo = jnp.where(first, 0.0, halo_sc[...])
        pool_sc[tile, :] = _pool_tile(u, halo, wbd_ref, psc_ref, row0, tm).astype(pool_sc.dtype)
        halo_sc[...] = u[tm - 16:]
        ext_sc[:SUBLANE, :] = jnp.where(first, 0.0, ext_sc[:SUBLANE, :])
        st_sc[...] = jnp.where(first, 0.0, st_sc[...])
        for c in range(tm // q):
            rows = slice(s * tm + c * q, s * tm + (c + 1) * q)
            y = _ssd_chunk(xbc_ref[rows, :], z_ref[rows, :], sm_ref[rows, :], smt_ref[:, rows],
                           ext_sc, st_sc, cw_ref, cb_ref, arow_ref, acol_ref, dsk_ref, gn_ref,
                           row0 + c * q, q, l_valid, live)
            ssd_sc[rows, :] = y.astype(ssd_sc.dtype)
        fin_ref[t // per] = st_sc[...]


def _mix_ffn(x, attn, u, xbc, z, sm, smt, w, l, nb, tp, tm, q, l_valid):
    nsub = 2
    per = tp // tm
    assert q == LANE and tm % q == 0 and (nb * per) % nsub == 0
    ng = nb * per // nsub
    rows = nsub * tm
    prev = lambda c: pl.BlockSpec((rows, c), lambda i: (jnp.maximum(i - 1, 0), 0))
    cur = lambda c: pl.BlockSpec((rows, c), lambda i: (jnp.minimum(i, ng - 1), 0))
    names = ("wbd", "pscale", "cw", "cb", "arow", "acol", "dsk", "gn")
    big = ("wo", "g1", "g2", "g3", "wg", "wu", "wd")
    state_shape = (nb, H // 2, LANE, LANE)
    return pl.pallas_call(
        functools.partial(_mix_ffn_kernel, tm=tm, nsub=nsub, q=q, per=per, ng=ng, l_valid=l_valid),
        grid=(ng + 1,),
        in_specs=[prev(D_MODEL), prev(D_FOX), cur(D_POOL), cur(D_CONV), cur(D_SSD), cur(LANE),
                  pl.BlockSpec((16, rows), lambda i: (0, jnp.minimum(i, ng - 1)))]
                 + [_layer_spec(w[n], l, 1) for n in names]
                 + [_layer_spec(w[n], l, 1, single_buffer=True) for n in big],
        out_specs=(prev(D_MODEL), pl.BlockSpec(state_shape, lambda i: (0, 0, 0, 0))),
        out_shape=(jax.ShapeDtypeStruct((ng * rows, D_MODEL), F32), jax.ShapeDtypeStruct(state_shape, F32)),
        scratch_shapes=[pltpu.VMEM((16, D_POOL), F32), pltpu.VMEM((q + SUBLANE, D_CONV), F32),
                        pltpu.VMEM((H // 2, LANE, LANE), F32),
                        pltpu.VMEM((rows, D_POOL), BF16), pltpu.VMEM((rows, D_SSD), BF16)],
        compiler_params=_cparams(("arbitrary",)), name="mix_ffn",
    )(x, attn, u, xbc, z, sm, smt, *[w[n] for n in names], *[w[n] for n in big])


def _ffn(x, attn, pool, ssd, w, l, tm):
    t = x.shape[0]
    row = lambda c: pl.BlockSpec((tm, c), lambda i: (i, 0))
    names = ("wo", "g1", "g2", "g3", "wg", "wu", "wd")
    return pl.pallas_call(
        _ffn_kernel, grid=(t // tm,),
        in_specs=[row(D_MODEL), row(D_FOX), row(D_POOL), row(D_SSD)]
                 + [_layer_spec(w[n], l, 1, single_buffer=True) for n in names],
        out_specs=row(D_MODEL), out_shape=jax.ShapeDtypeStruct((t, D_MODEL), F32),
        compiler_params=_cparams(("parallel",)), name="outproj_ffn",
    )(x, attn, pool, ssd, *[w[n] for n in names])


def _lane_pack(f_vals, dt_vals):
    pieces, at = [], 0
    for start, vals in sorted([(_DT_AT, dt_vals)] + [(c, f_vals) for c in _F_COPIES], key=lambda t: t[0]):
        pieces += [jnp.zeros(vals.shape[:-1] + (start - at,), F32), vals.astype(F32)]
        at = start + H
    pieces.append(jnp.zeros(f_vals.shape[:-1] + (LANE - at,), F32))
    return jnp.concatenate(pieces, axis=-1)


def _prep_weights(w_in, fox_f_bias, pool_w, pool_scale, conv_w, conv_b, dt_bias, a_log, d_skip, ssd_norm,
                  w_out, w_gate, w_up, w_down, ln_pre_mix, ln_post_mix, ln_pre_ffn, ln_post_ffn):
    depth = w_in.shape[0]
    f0 = 3 * D_FOX
    u0 = f0 + H
    dt0 = u0 + D_POOL + D_SSD + D_CONV
    ws = _lane_pack(w_in[:, :, f0:u0], w_in[:, :, dt0:dt0 + H])
    wm = jnp.concatenate([w_in[:, :, :D_FOX], ws, w_in[:, :, u0:dt0]], axis=2).astype(BF16)
    ws = ws.astype(BF16)
    bs = _lane_pack(fox_f_bias, dt_bias)
    wbd = jnp.concatenate([jnp.pad(pool_w[:, g].astype(F32), ((0, 0), (0, 0), (g * 64, D_POOL - (g + 1) * 64)))
                           for g in range(D_POOL // 64)], axis=1)
    alog = _lane_pack(jnp.zeros_like(a_log), a_log)
    row = lambda a: a.astype(F32).reshape(depth, 1, -1)
    tr = lambda a: jnp.transpose(a, (0, 2, 1))
    return dict(
        g_pre=row(ln_pre_mix), wm=wm, wkv=w_in[:, :, D_FOX:f0].astype(BF16),
        wkt=tr(w_in[:, :, D_FOX:2 * D_FOX]).astype(BF16), wvt=tr(w_in[:, :, 2 * D_FOX:f0]).astype(BF16),
        wst=tr(ws[:, :, :16]),
        bs=bs.reshape(depth, 1, LANE), bst=bs[:, :16].reshape(depth, 16, 1),
        wbd=wbd.astype(BF16), pscale=row(pool_scale),
        cw=jnp.pad(conv_w.astype(F32), ((0, 0), (0, SUBLANE - CONV_W), (0, 0))), cb=row(conv_b),
        arow=alog.reshape(depth, 1, LANE), acol=alog[:, :16].reshape(depth, 16, 1),
        dsk=row(jnp.repeat(d_skip, HD, axis=1)), gn=row(ssd_norm),
        wo=w_out.astype(BF16), g1=row(ln_post_mix), g2=row(ln_pre_ffn), g3=row(ln_post_ffn),
        wg=w_gate.astype(BF16), wu=w_up.astype(BF16), wd=w_down.astype(BF16))


def _prompt_layer(x, w, l, nb, tp, lp, tm, q, kv_stacks):
    qb, kb, kt, vt, vtb, eb, u, z, xbc, sm, smt = _inproj(x, w, l, tm, (nb, tp, lp), kv_stacks)
    attn = _fox_prompt(qb, kb, eb, vtb, nb, tp, tm)
    x, fin = _mix_ffn(x, attn, u, xbc, z, sm, smt, w, l, nb, tp, tm, q, lp)
    return x, (kt, vt), (smt, u, xbc, fin)


def _sample_layer(x, w, l, nb, ls, past, tm, kct, vct, lfc, pool_pre, conv_pre, ssd_init):
    qb, k, kb, v, vb, u, z, xbc, sm, smt = _inproj(x, w, l, tm)
    smt_seq = jnp.transpose(smt.reshape(16, nb, ls), (1, 0, 2))
    lf_all = jnp.concatenate([lfc, smt_seq[:, :8], jnp.zeros((nb, 8, LANE - ls), F32)], axis=2)
    attn = _fox_sample(qb, kb, vb, kct, vct, lf_all, l, nb, ls, past)
    pool = _pool(u, pool_pre, w, l, nb, ls, past)
    ssd, fin = _ssd(xbc, z, sm, jnp.pad(smt_seq, ((0, 0), (0, 0), (0, LANE - ls))), conv_pre, ssd_init, w, l,
                    nb, ls)
    x = _ffn(x, attn, pool, ssd, w, l, tm)
    return x, (k, v), (smt, u, xbc, fin)


def kernel(x_prompt, x_sample, cache_fox_k, cache_fox_v, cache_fox_logf, state_pool, state_conv, state_ssd,
           meta_tokens, ln_pre_mix, ln_post_mix, ln_pre_ffn, ln_post_ffn, w_in, fox_f_bias, pool_w, pool_scale,
           conv_w, conv_b, dt_bias, a_log, d_skip, ssd_norm, w_out, w_gate, w_up, w_down):
    nbp, seq, _ = x_prompt.shape
    nbs, ls, _ = x_sample.shape
    depth, _, past = cache_fox_logf.shape[:3]
    lp = N_META + seq
    tq = 256
    q_ssd = 128
    tp = -(-lp // tq) * tq

    w = _prep_weights(w_in, fox_f_bias, pool_w, pool_scale, conv_w, conv_b, dt_bias, a_log, d_skip, ssd_norm,
                      w_out, w_gate, w_up, w_down, ln_pre_mix, ln_post_mix, ln_pre_ffn, ln_post_ffn)

    meta = jnp.broadcast_to(meta_tokens.astype(F32)[None], (nbp, N_META, D_MODEL))
    xp = jnp.concatenate([meta, x_prompt, jnp.zeros((nbp, tp - lp, D_MODEL), F32)], axis=1)
    xp = xp.reshape(nbp * tp, D_MODEL)
    xs = x_sample.reshape(nbs * ls, D_MODEL)

    ssd_init =_state_to_wide(state_ssd.reshape(depth * nbs, H, HD, HD)).reshape(depth, nbs, H // 2, LANE, LANE)

    kct = jnp.transpose(cache_fox_k, (0, 1, 3, 4, 2)).reshape(depth, nbs, D_FOX, past)
    vct = jnp.transpose(cache_fox_v, (0, 1, 3, 4, 2)).reshape(depth, nbs, D_FOX, past)
    lfc = jnp.pad(jnp.transpose(cache_fox_logf.astype(F32), (0, 1, 3, 2)),
                  ((0, 0), (0, 0), (0, 8 - H), (0, 0)))
    pool_pre = jnp.pad(state_pool.astype(F32), ((0, 0), (0, 0), (16 - POOL_BUF, 0), (0, 0)))
    conv_pre = jnp.pad(state_conv.astype(F32), ((0, 0), (0, 0), (8 - (CONV_W - 1), 0), (0, 0)))

    outs_p, outs_s, kv_s = [], [], []
    kv_stacks = tuple(jnp.zeros((depth, nbp, D_FOX, lp), F32) for _ in range(2))
    for l in range(depth):
        xp, kv_stacks, st = _prompt_layer(xp, w, l, nbp, tp, lp, tq, q_ssd, kv_stacks)
        outs_p.append(st)
        xs, kv, st = _sample_layer(xs, w, l, nbs, ls, past, tq, kct, vct, lfc[l], pool_pre[l], conv_pre[l],
                                   ssd_init[l])
        outs_s.append(st)
        kv_s.append(kv)

    def tails(outs, nb, rows, valid):
        seqv = lambda a: a.reshape(nb, rows, a.shape[-1])
        lf = jnp.stack([jnp.transpose(o[0][:H].reshape(H, nb, rows)[:, :, :valid], (1, 2, 0)) for o in outs])
        pn = jnp.stack([seqv(o[1])[:, valid - POOL_BUF:valid] for o in outs])
        cn = jnp.stack([seqv(o[2])[:, valid - (CONV_W - 1):valid] for o in outs])
        sn = jnp.stack([_state_from_wide(o[3]) for o in outs])
        return lf, pn, cn, sn

    def token_minor(a):
        return jnp.transpose(a.reshape(depth, nbp, H, HD, lp), (0, 1, 4, 2, 3))

    def token_major(i):
        return jnp.stack([kv[i].reshape(nbs, ls, H, HD) for kv in kv_s])

    y_prompt = xp.reshape(nbp, tp, D_MODEL)[:, N_META:lp]
    y_sample = xs.reshape(nbs, ls, D_MODEL)
    return ((y_prompt, y_sample, token_minor(kv_stacks[0]), token_minor(kv_stacks[1])) + tails(outs_p, nbp, tp, lp)
            + (token_major(0), token_major(1)) + tails(outs_s, nbs, ls, ls))
```

```python
import functools
import math

import jax
import jax.numpy as jnp
from jax import lax
from jax.experimental import pallas as pl
from jax.experimental.pallas import tpu as pltpu

F32 = jnp.float32
BF16 = jnp.bfloat16

D_MODEL = 1024
N_META = 16
EPS = 1e-6
H = 6
HD = 64
D_FOX = H * HD
D_POOL = 256
POOL_BUF = 15
D_SSD = H * HD
D_BC = 128
D_CONV = D_SSD + 2 * D_BC
CONV_W = 4
LANE = 128
SUBLANE = 8
NEG = -1e30
LOG2E = math.log2(math.e)

_Q0, _S0, _U0, _Z0, _X0, _MAIN = 0, 384, 512, 768, 1152, 1792
_MXU_N = 256
_GROUPS = ((_Q0, _U0), (_U0, _Z0), (_Z0, _X0 + LANE), (_X0 + LANE, _MAIN))
assert all((b - a) % _MXU_N == 0 for a, b in _GROUPS)
_DT_AT, _F_COPIES = 8, (0, 16, 32)
_VROWS = HD + SUBLANE

_VMEM_LIMIT = 56 * 1024 * 1024


def _cparams(sem):
    return pltpu.CompilerParams(dimension_semantics=sem, vmem_limit_bytes=_VMEM_LIMIT)


def _layer_spec(a, l, ngrid, single_buffer=False):
    idx = (l,) + (0,) * (a.ndim - 1)
    kw = dict(pipeline_mode=pl.Buffered(1)) if single_buffer else {}
    return pl.BlockSpec((None,) + a.shape[1:], lambda *_: idx, **kw)


def _rms(x, g):
    ms = jnp.mean(x * x, axis=-1, keepdims=True)
    return x * lax.rsqrt(ms + EPS) * g


def _silu(x):
    return x * jax.nn.sigmoid(x)


def _softplus_tail(x):
    return jnp.log1p(jnp.exp(-jnp.abs(x)))


def _nt(a, b):
    return lax.dot_general(a, b, (((1,), (1,)), ((), ())), preferred_element_type=F32)


def _scan_lanes(x):
    lane = lax.broadcasted_iota(jnp.int32, x.shape, 1)
    s = 1
    while s < x.shape[1]:
        x = x + jnp.where(lane >= s, pltpu.roll(x, s, 1), 0.0)
        s *= 2
    return x


def _scan_rows(x):
    row = lax.broadcasted_iota(jnp.int32, x.shape, 0)
    s = 1
    while s < x.shape[0]:
        x = x + jnp.where(row >= s, pltpu.roll(x, s, 0), 0.0)
        s *= 2
    return x


def _inproj_kernel(x_ref, g_ref, wm_ref, wst_ref, bs_ref, bst_ref, *rest, prompt, qscale, per):
    hb = _rms(x_ref[...], g_ref[...]).astype(BF16)
    hbt = hb.T

    if prompt:
        (wkt_ref, wvt_ref, _, _, qb_ref, kb_ref, kt_ref, vt_ref, vtb_ref, e_ref, u_ref, z_ref, xbc_ref, sm_ref,
         smt_ref, carry_sc) = rest
        kt = jnp.dot(wkt_ref[...], hbt, preferred_element_type=F32)
        kt_ref[...] = kt
        kb_ref[...] = kt.T.astype(BF16)
        vt = jnp.dot(wvt_ref[...], hbt, preferred_element_type=F32)
        vt_ref[...] = vt
        tm = vt.shape[1]
        ones_row = jnp.where(lax.broadcasted_iota(jnp.int32, (SUBLANE, tm), 0) == 0, 1.0, 0.0).astype(BF16)
        for h in range(H):
            vtb_ref[0, h * _VROWS:h * _VROWS + HD, :] = vt[h * HD:(h + 1) * HD].astype(BF16)
            vtb_ref[0, h * _VROWS + HD:(h + 1) * _VROWS, :] = ones_row
    else:
        wkv_ref, qb_ref, k_ref, kb_ref, v_ref, vb_ref, u_ref, z_ref, xbc_ref, sm_ref, smt_ref = rest
        kv = jnp.dot(hb, wkv_ref[...], preferred_element_type=F32)
        k_ref[...] = kv[:, :D_FOX]
        kb_ref[...] = kv[:, :D_FOX].astype(BF16)
        v_ref[...] = kv[:, D_FOX:]
        vb_ref[...] = kv[:, D_FOX:].astype(BF16)
    g0, g1, g2, g3 = (jnp.dot(hb, wm_ref[:, a:b], preferred_element_type=F32) for a, b in _GROUPS)
    qb_ref[...] = (g0[:, :_S0] * qscale).astype(BF16)
    u_ref[...] = g1
    z_ref[...] = g2[:, :_X0 - _Z0]
    xbc_ref[:, :LANE] = g2[:, _X0 - _Z0:]
    xbc_ref[:, LANE:] = g3

    sm = g0[:, _S0:] + bs_ref[...]
    lane = lax.broadcasted_iota(jnp.int32, sm.shape, 1)
    tail = _softplus_tail(sm)
    is_dt = (lane >= _DT_AT) & (lane < _DT_AT + 8)
    sm = jnp.where(is_dt, jnp.maximum(sm, 0.0) + tail, jnp.minimum(sm, 0.0) - tail)
    sm_ref[...] = sm
    if prompt:
        first = (pl.program_id(0) % per) == 0
        c = _scan_rows(sm) + jnp.where(first, 0.0, carry_sc[...])
        carry_sc[...] = c[-1:, :]
        lane1 = lane[:1]
        c2 = c * LOG2E
        hi = c2.astype(BF16).astype(F32)
        mid = (c2 - hi).astype(BF16).astype(F32)
        e = jnp.where(lane1 < 16, hi, jnp.where(lane1 < 32, mid, (c2 - hi) - mid))
        e_ref[...] = jnp.where(((lane1 & 15) < H) & (lane1 < 48), e, 0.0).astype(BF16)
    smt = jnp.dot(wst_ref[...], hbt, preferred_element_type=F32) + bst_ref[...]
    row = lax.broadcasted_iota(jnp.int32, smt.shape, 0)
    tail = _softplus_tail(smt)
    smt_ref[...] = jnp.where(row < _DT_AT, jnp.minimum(smt, 0.0) - tail, jnp.maximum(smt, 0.0) + tail)


def _inproj(x, w, l, tm, prompt_dims=None, kv_stacks=()):
    t = x.shape[0]
    nt = t // tm
    row = lambda c: pl.BlockSpec((tm, c), lambda i: (i, 0))
    sds = jax.ShapeDtypeStruct
    tail_shapes = (sds((t, D_POOL), F32), sds((t, D_SSD), F32), sds((t, D_CONV), F32),
                   sds((t, LANE), F32), sds((16, t), F32))
    tail_specs = (row(D_POOL), row(D_SSD), row(D_CONV), row(LANE), pl.BlockSpec((16, tm), lambda i: (0, i)))
    if prompt_dims is not None:
        nb, tp, lp = prompt_dims
        per = tp // tm
        tmin = pl.BlockSpec((None, None, D_FOX, tm), lambda i: (l, i // per, 0, i % per))
        out_shape = (sds((t, D_FOX), BF16), sds((t, D_FOX), BF16), sds(kv_stacks[0].shape, F32),
                     sds(kv_stacks[1].shape, F32), sds((nt, H * _VROWS, tm), BF16),
                     sds((t, LANE), BF16)) + tail_shapes
        out_specs = (row(D_FOX), row(D_FOX), tmin, tmin,
                     pl.BlockSpec((1, H * _VROWS, tm), lambda i: (i, 0, 0)), row(LANE)) + tail_specs
        qscale = (HD ** -0.5) * LOG2E
        scratch = [pltpu.VMEM((1, LANE), F32)]
    else:
        per = None
        out_shape = (sds((t, D_FOX), BF16), sds((t, D_FOX), F32), sds((t, D_FOX), BF16),
                     sds((t, D_FOX), F32), sds((t, D_FOX), BF16)) + tail_shapes
        out_specs = (row(D_FOX),) * 5 + tail_specs
        qscale = HD ** -0.5
        scratch = []
    names = ("g_pre", "wm", "wst", "bs", "bst") + (("wkt", "wvt") if prompt_dims is not None else ("wkv",))
    n_in = 1 + len(names)
    return pl.pallas_call(
        functools.partial(_inproj_kernel, prompt=prompt_dims is not None, qscale=qscale, per=per), grid=(nt,),
        in_specs=[row(D_MODEL)] + [_layer_spec(w[n], l, 1) for n in names]
                 + [pl.BlockSpec(memory_space=pl.ANY)] * len(kv_stacks),
        out_specs=out_specs, out_shape=out_shape, scratch_shapes=scratch,
        input_output_aliases={n_in + s: 2 + s for s in range(len(kv_stacks))},
        compiler_params=_cparams(("arbitrary",)), name="inproj",
    )(x, *[w[n] for n in names], *kv_stacks)


def _fox_prompt_kernel(q_ref, k_ref, e_ref, vt_ref, o_ref, qa_sc, m_sc, acc_sc, sa_sc, sb_sc, *, tq):
    i = pl.program_id(1)
    causal = (lax.broadcasted_iota(jnp.int32, (tq, tq), 0)
              <= lax.broadcasted_iota(jnp.int32, (tq, tq), 1))
    feat = lax.broadcasted_iota(jnp.int32, (LANE, 1), 0)
    for p in range(H // 2):
        qt = q_ref[:, p * LANE:(p + 1) * LANE].astype(F32).T
        for h in (2 * p, 2 * p + 1):
            mine = (feat < HD) if h % 2 == 0 else (feat >= HD)
            pick = (feat == _F_COPIES[0] + h) | (feat == _F_COPIES[1] + h) | (feat == _F_COPIES[2] + h)
            qa_sc[h, :LANE, :] = jnp.where(mine, qt, 0.0).astype(BF16)
            qa_sc[h, LANE:, :] = jnp.broadcast_to(jnp.where(pick, -1.0, 0.0), (LANE, tq)).astype(BF16)
    m_sc[...] = jnp.full(m_sc.shape, NEG, F32)
    acc_sc[...] = jnp.zeros(acc_sc.shape, F32)

    def logits_h(j, s_sc, h):
        rows = pl.ds(pl.multiple_of(j * tq, tq), tq)
        ka = jnp.concatenate([k_ref[rows, (h // 2) * LANE:(h // 2 + 1) * LANE], e_ref[rows, :]], axis=1)
        s_sc[h] = jnp.dot(ka, qa_sc[h], preferred_element_type=F32)

    def softmax_pv_h(j, s_sc, masked, h):
        s = s_sc[h]
        if masked:
            s = jnp.where(causal, s, NEG)
        m_old = m_sc[h:h + 1, :]
        m_new = jnp.maximum(m_old, jnp.max(s, axis=0, keepdims=True))
        m_sc[h:h + 1, :] = m_new
        pm = jnp.exp2(s - m_new).astype(BF16)
        hs = slice(h * _VROWS, (h + 1) * _VROWS)
        pv = jnp.dot(vt_ref[j, hs, :], pm, preferred_element_type=F32)
        acc_sc[hs, :] = jnp.exp2(m_old - m_new) * acc_sc[hs, :] + pv

    def logits(j, s_sc):
        for h in range(H):
            logits_h(j, s_sc, h)

    def softmax_pv(j, s_sc, masked):
        for h in range(H):
            softmax_pv_h(j, s_sc, masked, h)

    def two_steps(j):
        logits(j + 1, sb_sc)
        softmax_pv(j, sa_sc, False)
        logits(j + 2, sa_sc)
        softmax_pv(j + 1, sb_sc, False)

    def body4(jj, carry):
        two_steps(4 * jj)
        two_steps(4 * jj + 2)
        return carry

    def body2(jj, carry):
        two_steps(4 * (i // 4) + 2 * jj)
        return carry

    logits(0, sa_sc)
    lax.fori_loop(0, i // 4, body4, 0)
    lax.fori_loop(0, (i % 4) // 2, body2, 0)

    @pl.when(i % 2 == 0)
    def _():
        softmax_pv(i, sa_sc, True)

    @pl.when(i % 2 == 1)
    def _():
        logits(i, sb_sc)
        softmax_pv(i - 1, sa_sc, False)
        softmax_pv(i, sb_sc, True)

    out = jnp.concatenate([acc_sc[h * _VROWS:h * _VROWS + HD, :] / acc_sc[h * _VROWS + HD:h * _VROWS + HD + 1, :]
                           for h in range(H)], axis=0)
    o_ref[...] = out.T.astype(o_ref.dtype)


def _fox_prompt(qb, kb, eb, vt, nb, tp, tq):
    nq = tp // tq
    return pl.pallas_call(
        functools.partial(_fox_prompt_kernel, tq=tq), grid=(nb, nq),
        in_specs=[pl.BlockSpec((tq, D_FOX), lambda b, i: (b * nq + i, 0)),
                  pl.BlockSpec((tp, D_FOX), lambda b, i: (b, 0)),
                  pl.BlockSpec((tp, LANE), lambda b, i: (b, 0)),
                  pl.BlockSpec((nq, H * _VROWS, tq), lambda b, i: (b, 0, 0))],
        out_specs=pl.BlockSpec((tq, D_FOX), lambda b, i: (b * nq + i, 0)),
        out_shape=jax.ShapeDtypeStruct((nb * tp, D_FOX), BF16),
        scratch_shapes=[pltpu.VMEM((H, 2 * LANE, tq), BF16), pltpu.VMEM((8, tq), F32),
                        pltpu.VMEM((H * _VROWS, tq), F32),
                        pltpu.VMEM((H, tq, tq), F32), pltpu.VMEM((H, tq, tq), F32)],
        compiler_params=_cparams(("parallel", "arbitrary")), name="fox_prompt",
    )(qb, kb, eb, vt)


def _fox_sample_kernel(q_ref, kn_ref, vn_ref, kct_ref, vct_ref, lf_ref, o_ref, *, past, ls, nseq):
    for s in range(nseq):
        rows = slice(s * ls, (s + 1) * ls)
        _fox_sample_one(q_ref.at[rows], kn_ref.at[rows], vn_ref.at[rows], kct_ref.at[s], vct_ref.at[s],
                        lf_ref.at[s], o_ref.at[rows], past, ls)


def _fox_sample_one(q_ref, kn_ref, vn_ref, kct_ref, vct_ref, lf_ref, o_ref, past, ls):
    causal = (lax.broadcasted_iota(jnp.int32, (ls, ls), 1)
              <= lax.broadcasted_iota(jnp.int32, (ls, ls), 0))
    nblk = lf_ref.shape[1] // LANE
    loc = _scan_lanes(jnp.concatenate([lf_ref[:, j * LANE:(j + 1) * LANE] for j in range(nblk)], axis=0))
    carry = jnp.zeros((8, 1), F32)
    cs = []
    for j in range(nblk):
        blk = loc[j * 8:(j + 1) * 8]
        cs.append(blk + carry)
        carry = carry + blk[:, LANE - 1:LANE]
    c_all = jnp.concatenate(cs, axis=1)

    q = q_ref[...]
    lane = lax.broadcasted_iota(jnp.int32, (1, D_FOX), 1)
    own = [(lane >= h * HD) & (lane < (h + 1) * HD) for h in range(H)]
    q_all = jnp.concatenate([jnp.where(own[h], q, jnp.zeros_like(q)) for h in range(H)], axis=0)
    bias = [c_all[h:h + 1, past - 1:past] - c_all[h:h + 1, :] for h in range(H)]
    bias_c = jnp.concatenate([jnp.broadcast_to(b[:, :past], (ls, past)) for b in bias], axis=0)
    bias_n = jnp.concatenate([jnp.broadcast_to(b[:, past:past + ls], (ls, ls)) for b in bias], axis=0)
    s_c = jnp.dot(q_all, kct_ref[...].astype(BF16), preferred_element_type=F32) + bias_c
    s_n = jnp.where(jnp.concatenate([causal] * H, axis=0), _nt(q_all, kn_ref[...]) + bias_n, NEG)
    m = jnp.maximum(jnp.max(s_c, axis=-1, keepdims=True), jnp.max(s_n, axis=-1, keepdims=True))
    p_c = jnp.exp(s_c - m)
    p_n = jnp.exp(s_n - m)
    den = jnp.sum(p_c, axis=-1, keepdims=True) + jnp.sum(p_n, axis=-1, keepdims=True)
    o_all = (_nt(p_c.astype(BF16), vct_ref[...].astype(BF16))
             + jnp.dot(p_n.astype(BF16), vn_ref[...], preferred_element_type=F32)) / den
    o = jnp.zeros((ls, D_FOX), F32)
    for h in range(H):
        o = jnp.where(own[h], o_all[h * ls:(h + 1) * ls], o)
    o_ref[...] = o.astype(o_ref.dtype)


def _fox_sample(qb, kb, vb, kct, vct, lf_all, l, nb, ls, past):
    nseq = 4
    new = pl.BlockSpec((nseq * ls, D_FOX), lambda b: (b, 0))
    cache = pl.BlockSpec((None, nseq, D_FOX, past), lambda b: (l, b, 0, 0))
    return pl.pallas_call(
        functools.partial(_fox_sample_kernel, past=past, ls=ls, nseq=nseq), grid=(nb // nseq,),
        in_specs=[new, new, new, cache, cache, pl.BlockSpec((nseq, 8, past + LANE), lambda b: (b, 0, 0))],
        out_specs=new, out_shape=jax.ShapeDtypeStruct((nb * ls, D_FOX), BF16),
        compiler_params=_cparams(("parallel",)), name="fox_sample",
    )(qb, kb, vb, kct, vct, lf_all)


def _pool_tile(u, halo, w_ref, sc_ref, pos_start, tm):
    a = jnp.concatenate([halo, u], axis=0)
    e1 = a + pltpu.roll(a, 1, 0)
    e2 = e1 + pltpu.roll(e1, 2, 0)
    e3 = e2 + pltpu.roll(e2, 4, 0)
    e4 = e3 + pltpu.roll(e3, 8, 0)
    lane = lax.broadcasted_iota(jnp.int32, (1, D_POOL), 1)
    win = jnp.where(lane < 64, e1, jnp.where(lane < 128, e2, jnp.where(lane < 192, e3, e4)))[16:]
    wsz = jnp.where(lane < 64, 2.0, jnp.where(lane < 128, 4.0, jnp.where(lane < 192, 8.0, 16.0)))
    pos = (pos_start + lax.broadcasted_iota(jnp.int32, (tm, 1), 0)).astype(F32)
    diff = win / jnp.minimum(pos + 1.0, wsz) - u
    return jnp.dot(diff.astype(BF16), w_ref[...], preferred_element_type=F32) * sc_ref[...]


def _pool_kernel(u_ref, pre_ref, w_ref, sc_ref, o_ref, *, ls, nseq, pos0):
    for s in range(nseq):
        rows = slice(s * ls, (s + 1) * ls)
        o_ref[rows, :] = _pool_tile(u_ref[rows, :], pre_ref[s], w_ref, sc_ref, pos0, ls).astype(o_ref.dtype)


def _pool(u, prefix, w, l, nb, ls, pos0):
    nseq = 8
    tile = pl.BlockSpec((nseq * ls, D_POOL), lambda b: (b, 0))
    return pl.pallas_call(
        functools.partial(_pool_kernel, ls=ls, nseq=nseq, pos0=pos0), grid=(nb // nseq,),
        in_specs=[tile, pl.BlockSpec((nseq, 16, D_POOL), lambda b: (b, 0, 0)),
                  _layer_spec(w["wbd"], l, 1), _layer_spec(w["pscale"], l, 1)],
        out_specs=tile, out_shape=jax.ShapeDtypeStruct((nb * ls, D_POOL), BF16),
        compiler_params=_cparams(("parallel",)), name="pool_mixer",
    )(u, prefix, w["wbd"], w["pscale"])


def _ssd_chunk(xbc, z, sm, smt, ext_sc, st_sc, cw_ref, cb_ref, arow_ref, acol_ref, dsk_ref, gn_ref,
               row0, q, l_valid, live=None):
    ext_sc[SUBLANE:, :] = xbc
    w = cw_ref[...]
    conv = cb_ref[...]
    for j in range(CONV_W):
        conv = conv + w[j:j + 1] * ext_sc[pl.ds(SUBLANE - (CONV_W - 1) + j, q), :]
    ext_sc[:SUBLANE, :] = ext_sc[q:, :]
    act = _silu(conv)
    xs = act[:, :D_SSD]
    bbf = act[:, D_SSD:D_SSD + D_BC]
    bb = bbf.astype(BF16)
    cc = act[:, D_SSD + D_BC:].astype(BF16)
    bbt = bbf.T

    rvalid = (row0 + lax.broadcasted_iota(jnp.int32, (q, 1), 0)) < l_valid
    dtc = jnp.where(rvalid, sm, 0.0)
    acs_c = _scan_rows(dtc * (-jnp.exp(arow_ref[...])))
    cvalid = (row0 + lax.broadcasted_iota(jnp.int32, (1, LANE), 1)) < l_valid
    dtt = jnp.where(cvalid, smt, 0.0)
    acs_t = _scan_lanes(dtt * (-jnp.exp(acol_ref[...])))

    lane = lax.broadcasted_iota(jnp.int32, (1, LANE), 1)
    lo = lane < HD
    n_lo = lax.broadcasted_iota(jnp.int32, (LANE, 1), 0) < HD
    causal = (lax.broadcasted_iota(jnp.int32, (q, q), 1)
              <= lax.broadcasted_iota(jnp.int32, (q, q), 0))
    zc = jnp.zeros_like(cc)
    cb = [_nt(jnp.where(lo, cc, zc), bb), _nt(jnp.where(lo, zc, cc), bb)]
    ys = []
    for p in range(H // 2):
        cols = slice(p * LANE, (p + 1) * LANE)
        xp = xs[:, cols]
        xpb = xp.astype(BF16)
        st = st_sc[p]
        y_in = jnp.dot(cc, st.astype(BF16), preferred_element_type=F32)
        y_h, upd, dec = [], [], []
        for hh in range(2):
            h = 2 * p + hh
            g = h // (H // 2)
            arep = jnp.broadcast_to(acs_c[:, _DT_AT + h:_DT_AT + h + 1], (q, LANE))
            ak = acs_t[_DT_AT + h:_DT_AT + h + 1, :q]
            dtr = dtt[_DT_AT + h:_DT_AT + h + 1, :q]
            lm = jnp.exp(jnp.where(causal, arep[:, :q] - ak, NEG))
            gm = (cb[g] * lm * dtr).astype(BF16)
            y_h.append(jnp.dot(gm, xpb, preferred_element_type=F32) + y_in * jnp.exp(arep))
            alast = acs_t[_DT_AT + h:_DT_AT + h + 1, q - 1:q]
            bw = (bbt * (jnp.exp(alast - ak) * dtr)).astype(BF16)
            keep = (lo if hh == 0 else ~lo) & (n_lo if g == 0 else ~n_lo)
            upd.append(jnp.where(keep, jnp.dot(bw, xpb, preferred_element_type=F32), 0.0))
            dec.append(jnp.exp(alast))
        st_new = jnp.where(lo, dec[0], dec[1]) * st + upd[0] + upd[1]
        st_sc[p] = st_new if live is None else jnp.where(live, st_new, st)
        ys.append(jnp.where(lo, y_h[0], y_h[1]) + dsk_ref[:, cols] * xp)
    yc = jnp.concatenate(ys, axis=1) * _silu(z)
    return _rms(yc, gn_ref[...])


def _ssd_kernel(xbc_ref, z_ref, sm_ref, smt_ref, pre_ref, init_ref, cw_ref, cb_ref, arow_ref,
                acol_ref, dsk_ref, gn_ref, y_ref, fin_ref, ext_sc, st_sc, *, q, nseq):
    for s in range(nseq):
        rows = slice(s * q, (s + 1) * q)
        ext_sc[:SUBLANE, :] = pre_ref[s]
        st_sc[...] = init_ref[s]
        y = _ssd_chunk(xbc_ref[rows, :], z_ref[rows, :], sm_ref[rows, :], smt_ref[s], ext_sc, st_sc, cw_ref,
                       cb_ref, arow_ref, acol_ref, dsk_ref, gn_ref, 0, q, q)
        y_ref[rows, :] = y.astype(y_ref.dtype)
        fin_ref[s] = st_sc[...]


def _ssd(xbc, z, sm, smt, prefix, init, w, l, nb, q):
    nseq = 4
    rows = lambda c: pl.BlockSpec((nseq * q, c), lambda b: (b, 0))
    names = ("cw", "cb", "arow", "acol", "dsk", "gn")
    state = pl.BlockSpec((nseq, H // 2, LANE, LANE), lambda b: (b, 0, 0, 0))
    return pl.pallas_call(
        functools.partial(_ssd_kernel, q=q, nseq=nseq), grid=(nb // nseq,),
        in_specs=[rows(D_CONV), rows(D_SSD), rows(LANE), pl.BlockSpec((nseq, 16, LANE), lambda b: (b, 0, 0)),
                  pl.BlockSpec((nseq, 8, D_CONV), lambda b: (b, 0, 0)), state]
                 + [_layer_spec(w[n], l, 1) for n in names],
        out_specs=(rows(D_SSD), state),
        out_shape=(jax.ShapeDtypeStruct((nb * q, D_SSD), BF16),
                   jax.ShapeDtypeStruct((nb, H // 2, LANE, LANE), F32)),
        scratch_shapes=[pltpu.VMEM((q + SUBLANE, D_CONV), F32), pltpu.VMEM((H // 2, LANE, LANE), F32)],
        compiler_params=_cparams(("parallel",)), name="conv_ssd",
    )(xbc, z, sm, smt, prefix, init, *[w[n] for n in names])


def _state_to_wide(s):
    def placed(h):
        g, hh = h // (H // 2), h % 2
        return jnp.pad(jnp.swapaxes(s[:, h].astype(F32), 1, 2),
                       ((0, 0), (g * HD, LANE - (g + 1) * HD), (hh * HD, LANE - (hh + 1) * HD)))
    return jnp.stack([placed(2 * p) + placed(2 * p + 1) for p in range(H // 2)], axis=1)


def _state_from_wide(wide):
    return jnp.stack([jnp.swapaxes(wide[:, h // 2, (h // (H // 2)) * HD:(h // (H // 2) + 1) * HD,
                                        (h % 2) * HD:(h % 2 + 1) * HD], 1, 2) for h in range(H)], axis=1)


def _ffn_tile(x, a, p, s, wo_ref, g1_ref, g2_ref, g3_ref, wg_ref, wu_ref, wd_ref):
    mp = (jnp.dot(a, wo_ref[:D_FOX, :], preferred_element_type=F32)
          + jnp.dot(p, wo_ref[D_FOX:D_FOX + D_POOL, :], preferred_element_type=F32)
          + jnp.dot(s, wo_ref[D_FOX + D_POOL:, :], preferred_element_type=F32))
    x1 = x + _rms(mp, g1_ref[...])
    hb = _rms(x1, g2_ref[...]).astype(BF16)
    gate = jnp.dot(hb, wg_ref[...], preferred_element_type=F32)
    up = jnp.dot(hb, wu_ref[...], preferred_element_type=F32)
    act = (_silu(gate) * up).astype(BF16)
    ff = jnp.dot(act, wd_ref[...], preferred_element_type=F32)
    return x1 + _rms(ff, g3_ref[...])


def _ffn_kernel(x_ref, a_ref, p_ref, s_ref, *rest):
    o_ref = rest[-1]
    o_ref[...] = _ffn_tile(x_ref[...], a_ref[...], p_ref[...], s_ref[...], *rest[:-1])


def _mix_ffn_kernel(x_ref, a_ref, u_ref, xbc_ref, z_ref, sm_ref, smt_ref, wbd_ref, psc_ref,
                    cw_ref, cb_ref, arow_ref, acol_ref, dsk_ref, gn_ref,
                    wo_ref, g1_ref, g2_ref, g3_ref, wg_ref, wu_ref, wd_ref,
                    o_ref, fin_ref, halo_sc, ext_sc, st_sc, pool_sc, ssd_sc, *, tm, nsub, q, per, ng, l_valid):
    i = pl.program_id(0)
    live = i < ng
    ic = jnp.minimum(i, ng - 1)

    @pl.when(i == 0)
    def _():
        pool_sc[...] = jnp.zeros(pool_sc.shape, pool_sc.dtype)
        ssd_sc[...] = jnp.zeros(ssd_sc.shape, ssd_sc.dtype)

    o_ref[...] = _ffn_tile(x_ref[...], a_ref[...], pool_sc[...], ssd_sc[...], wo_ref, g1_ref, g2_ref, g3_ref,
                           wg_ref, wu_ref, wd_ref)

    for s in range(nsub):
        t = ic * nsub + s
        first = (t % per) == 0
        row0 = (t % per) * tm
        tile = slice(s * tm, (s + 1) * tm)
        u = u_ref[tile, :]
        halo = jnp.where(first, 0.0, halo_sc[...])
        pool_sc[tile, :] = _pool_tile(u, halo, wbd_ref, psc_ref, row0, tm).astype(pool_sc.dtype)
        halo_sc[...] = u[tm - 16:]
        ext_sc[:SUBLANE, :] = jnp.where(first, 0.0, ext_sc[:SUBLANE, :])
        st_sc[...] = jnp.where(first, 0.0, st_sc[...])
        for c in range(tm // q):
            rows = slice(s * tm + c * q, s * tm + (c + 1) * q)
            y = _ssd_chunk(xbc_ref[rows, :], z_ref[rows, :], sm_ref[rows, :], smt_ref[:, rows],
                           ext_sc, st_sc, cw_ref, cb_ref, arow_ref, acol_ref, dsk_ref, gn_ref,
                           row0 + c * q, q, l_valid, live)
            ssd_sc[rows, :] = y.astype(ssd_sc.dtype)
        fin_ref[t // per] = st_sc[...]


def _mix_ffn(x, attn, u, xbc, z, sm, smt, w, l, nb, tp, tm, q, l_valid):
    nsub = 2
    per = tp // tm
    assert q == LANE and tm % q == 0 and (nb * per) % nsub == 0
    ng = nb * per // nsub
    rows = nsub * tm
    prev = lambda c: pl.BlockSpec((rows, c), lambda i: (jnp.maximum(i - 1, 0), 0))
    cur = lambda c: pl.BlockSpec((rows, c), lambda i: (jnp.minimum(i, ng - 1), 0))
    names = ("wbd", "pscale", "cw", "cb", "arow", "acol", "dsk", "gn")
    big = ("wo", "g1", "g2", "g3", "wg", "wu", "wd")
    state_shape = (nb, H // 2, LANE, LANE)
    return pl.pallas_call(
        functools.partial(_mix_ffn_kernel, tm=tm, nsub=nsub, q=q, per=per, ng=ng, l_valid=l_valid),
        grid=(ng + 1,),
        in_specs=[prev(D_MODEL), prev(D_FOX), cur(D_POOL), cur(D_CONV), cur(D_SSD), cur(LANE),
                  pl.BlockSpec((16, rows), lambda i: (0, jnp.minimum(i, ng - 1)))]
                 + [_layer_spec(w[n], l, 1) for n in names]
                 + [_layer_spec(w[n], l, 1, single_buffer=True) for n in big],
        out_specs=(prev(D_MODEL), pl.BlockSpec(state_shape, lambda i: (0, 0, 0, 0))),
        out_shape=(jax.ShapeDtypeStruct((ng * rows, D_MODEL), F32), jax.ShapeDtypeStruct(state_shape, F32)),
        scratch_shapes=[pltpu.VMEM((16, D_POOL), F32), pltpu.VMEM((q + SUBLANE, D_CONV), F32),
                        pltpu.VMEM((H // 2, LANE, LANE), F32),
                        pltpu.VMEM((rows, D_POOL), BF16), pltpu.VMEM((rows, D_SSD), BF16)],
        compiler_params=_cparams(("arbitrary",)), name="mix_ffn",
    )(x, attn, u, xbc, z, sm, smt, *[w[n] for n in names], *[w[n] for n in big])


def _ffn(x, attn, pool, ssd, w, l, tm):
    t = x.shape[0]
    row = lambda c: pl.BlockSpec((tm, c), lambda i: (i, 0))
    names = ("wo", "g1", "g2", "g3", "wg", "wu", "wd")
    return pl.pallas_call(
        _ffn_kernel, grid=(t // tm,),
        in_specs=[row(D_MODEL), row(D_FOX), row(D_POOL), row(D_SSD)]
                 + [_layer_spec(w[n], l, 1, single_buffer=True) for n in names],
        out_specs=row(D_MODEL), out_shape=jax.ShapeDtypeStruct((t, D_MODEL), F32),
        compiler_params=_cparams(("parallel",)), name="outproj_ffn",
    )(x, attn, pool, ssd, *[w[n] for n in names])


def _lane_pack(f_vals, dt_vals):
    pieces, at = [], 0
    for start, vals in sorted([(_DT_AT, dt_vals)] + [(c, f_vals) for c in _F_COPIES], key=lambda t: t[0]):
        pieces += [jnp.zeros(vals.shape[:-1] + (start - at,), F32), vals.astype(F32)]
        at = start + H
    pieces.append(jnp.zeros(f_vals.shape[:-1] + (LANE - at,), F32))
    return jnp.concatenate(pieces, axis=-1)


def _prep_weights(w_in, fox_f_bias, pool_w, pool_scale, conv_w, conv_b, dt_bias, a_log, d_skip, ssd_norm,
                  w_out, w_gate, w_up, w_down, ln_pre_mix, ln_post_mix, ln_pre_ffn, ln_post_ffn):
    depth = w_in.shape[0]
    f0 = 3 * D_FOX
    u0 = f0 + H
    dt0 = u0 + D_POOL + D_SSD + D_CONV
    ws = _lane_pack(w_in[:, :, f0:u0], w_in[:, :, dt0:dt0 + H])
    wm = jnp.concatenate([w_in[:, :, :D_FOX], ws, w_in[:, :, u0:dt0]], axis=2).astype(BF16)
    ws = ws.astype(BF16)
    bs = _lane_pack(fox_f_bias, dt_bias)
    wbd = jnp.concatenate([jnp.pad(pool_w[:, g].astype(F32), ((0, 0), (0, 0), (g * 64, D_POOL - (g + 1) * 64)))
                           for g in range(D_POOL // 64)], axis=1)
    alog = _lane_pack(jnp.zeros_like(a_log), a_log)
    row = lambda a: a.astype(F32).reshape(depth, 1, -1)
    tr = lambda a: jnp.transpose(a, (0, 2, 1))
    return dict(
        g_pre=row(ln_pre_mix), wm=wm, wkv=w_in[:, :, D_FOX:f0].astype(BF16),
        wkt=tr(w_in[:, :, D_FOX:2 * D_FOX]).astype(BF16), wvt=tr(w_in[:, :, 2 * D_FOX:f0]).astype(BF16),
        wst=tr(ws[:, :, :16]),
        bs=bs.reshape(depth, 1, LANE), bst=bs[:, :16].reshape(depth, 16, 1),
        wbd=wbd.astype(BF16), pscale=row(pool_scale),
        cw=jnp.pad(conv_w.astype(F32), ((0, 0), (0, SUBLANE - CONV_W), (0, 0))), cb=row(conv_b),
        arow=alog.reshape(depth, 1, LANE), acol=alog[:, :16].reshape(depth, 16, 1),
        dsk=row(jnp.repeat(d_skip, HD, axis=1)), gn=row(ssd_norm),
        wo=w_out.astype(BF16), g1=row(ln_post_mix), g2=row(ln_pre_ffn), g3=row(ln_post_ffn),
        wg=w_gate.astype(BF16), wu=w_up.astype(BF16), wd=w_down.astype(BF16))


def _prompt_layer(x, w, l, nb, tp, lp, tm, q, kv_stacks):
    qb, kb, kt, vt, vtb, eb, u, z, xbc, sm, smt = _inproj(x, w, l, tm, (nb, tp, lp), kv_stacks)
    attn = _fox_prompt(qb, kb, eb, vtb, nb, tp, tm)
    x, fin = _mix_ffn(x, attn, u, xbc, z, sm, smt, w, l, nb, tp, tm, q, lp)
    return x, (kt, vt), (smt, u, xbc, fin)


def _sample_layer(x, w, l, nb, ls, past, tm, kct, vct, lfc, pool_pre, conv_pre, ssd_init):
    qb, k, kb, v, vb, u, z, xbc, sm, smt = _inproj(x, w, l, tm)
    smt_seq = jnp.transpose(smt.reshape(16, nb, ls), (1, 0, 2))
    lf_all = jnp.concatenate([lfc, smt_seq[:, :8], jnp.zeros((nb, 8, LANE - ls), F32)], axis=2)
    attn = _fox_sample(qb, kb, vb, kct, vct, lf_all, l, nb, ls, past)
    pool = _pool(u, pool_pre, w, l, nb, ls, past)
    ssd, fin = _ssd(xbc, z, sm, jnp.pad(smt_seq, ((0, 0), (0, 0), (0, LANE - ls))), conv_pre, ssd_init, w, l,
                    nb, ls)
    x = _ffn(x, attn, pool, ssd, w, l, tm)
    return x, (k, v), (smt, u, xbc, fin)


def kernel(x_prompt, x_sample, cache_fox_k, cache_fox_v, cache_fox_logf, state_pool, state_conv, state_ssd,
           meta_tokens, ln_pre_mix, ln_post_mix, ln_pre_ffn, ln_post_ffn, w_in, fox_f_bias, pool_w, pool_scale,
           conv_w, conv_b, dt_bias, a_log, d_skip, ssd_norm, w_out, w_gate, w_up, w_down):
    nbp, seq, _ = x_prompt.shape
    nbs, ls, _ = x_sample.shape
    depth, _, past = cache_fox_logf.shape[:3]
    lp = N_META + seq
    tq = 256
    q_ssd = 128
    tp = -(-lp // tq) * tq

    w = _prep_weights(w_in, fox_f_bias, pool_w, pool_scale, conv_w, conv_b, dt_bias, a_log, d_skip, ssd_norm,
                      w_out, w_gate, w_up, w_down, ln_pre_mix, ln_post_mix, ln_pre_ffn, ln_post_ffn)

    meta = jnp.broadcast_to(meta_tokens.astype(F32)[None], (nbp, N_META, D_MODEL))
    xp = jnp.concatenate([meta, x_prompt, jnp.zeros((nbp, tp - lp, D_MODEL), F32)], axis=1)
    xp = xp.reshape(nbp * tp, D_MODEL)
    xs = x_sample.reshape(nbs * ls, D_MODEL)

    ssd_init =_state_to_wide(state_ssd.reshape(depth * nbs, H, HD, HD)).reshape(depth, nbs, H // 2, LANE, LANE)

    kct = jnp.transpose(cache_fox_k, (0, 1, 3, 4, 2)).reshape(depth, nbs, D_FOX, past)
    vct = jnp.transpose(cache_fox_v, (0, 1, 3, 4, 2)).reshape(depth, nbs, D_FOX, past)
    lfc = jnp.pad(jnp.transpose(cache_fox_logf.astype(F32), (0, 1, 3, 2)),
                  ((0, 0), (0, 0), (0, 8 - H), (0, 0)))
    pool_pre = jnp.pad(state_pool.astype(F32), ((0, 0), (0, 0), (16 - POOL_BUF, 0), (0, 0)))
    conv_pre = jnp.pad(state_conv.astype(F32), ((0, 0), (0, 0), (8 - (CONV_W - 1), 0), (0, 0)))

    outs_p, outs_s, kv_s = [], [], []
    kv_stacks = tuple(jnp.zeros((depth, nbp, D_FOX, lp), F32) for _ in range(2))
    for l in range(depth):
        xp, kv_stacks, st = _prompt_layer(xp, w, l, nbp, tp, lp, tq, q_ssd, kv_stacks)
        outs_p.append(st)
        xs, kv, st = _sample_layer(xs, w, l, nbs, ls, past, tq, kct, vct, lfc[l], pool_pre[l], conv_pre[l],
                                   ssd_init[l])
        outs_s.append(st)
        kv_s.append(kv)

    def tails(outs, nb, rows, valid):
        seqv = lambda a: a.reshape(nb, rows, a.shape[-1])
        lf = jnp.stack([jnp.transpose(o[0][:H].reshape(H, nb, rows)[:, :, :valid], (1, 2, 0)) for o in outs])
        pn = jnp.stack([seqv(o[1])[:, valid - POOL_BUF:valid] for o in outs])
        cn = jnp.stack([seqv(o[2])[:, valid - (CONV_W - 1):valid] for o in outs])
        sn = jnp.stack([_state_from_wide(o[3]) for o in outs])
        return lf, pn, cn, sn

    def token_minor(a):
        return jnp.transpose(a.reshape(depth, nbp, H, HD, lp), (0, 1, 4, 2, 3))

    def token_major(i):
        return jnp.stack([kv[i].reshape(nbs, ls, H, HD) for kv in kv_s])

    y_prompt = xp.reshape(nbp, tp, D_MODEL)[:, N_META:lp]
    y_sample = xs.reshape(nbs, ls, D_MODEL)
    return ((y_prompt, y_sample, token_minor(kv_stacks[0]), token_minor(kv_stacks[1])) + tails(outs_p, nbp, tp, lp)
            + (token_major(0), token_major(1)) + tails(outs_s, nbs, ls, ls))
```

```python
import functools
import math

import jax
import jax.numpy as jnp
from jax import lax
from jax.experimental import pallas as pl
from jax.experimental.pallas import tpu as pltpu

F32 = jnp.float32
BF16 = jnp.bfloat16

D_MODEL = 1024
N_META = 16
EPS = 1e-6
H = 6
HD = 64
D_FOX = H * HD
D_POOL = 256
POOL_BUF = 15
D_SSD = H * HD
D_BC = 128
D_CONV = D_SSD + 2 * D_BC
CONV_W = 4
LANE = 128
SUBLANE = 8
NEG = -1e30
LOG2E = math.log2(math.e)

_Q0, _S0, _U0, _Z0, _X0, _MAIN = 0, 384, 512, 768, 1152, 1792
_MXU_N = 256
_GROUPS = ((_Q0, _U0), (_U0, _Z0), (_Z0, _X0 + LANE), (_X0 + LANE, _MAIN))
assert all((b - a) % _MXU_N == 0 for a, b in _GROUPS)
_DT_AT, _F_COPIES = 8, (0, 16, 32)
_VROWS = HD + SUBLANE

_VMEM_LIMIT = 56 * 1024 * 1024


def _cparams(sem):
    return pltpu.CompilerParams(dimension_semantics=sem, vmem_limit_bytes=_VMEM_LIMIT)


def _layer_spec(a, l, ngrid, single_buffer=False):
    idx = (l,) + (0,) * (a.ndim - 1)
    kw = dict(pipeline_mode=pl.Buffered(1)) if single_buffer else {}
    return pl.BlockSpec((None,) + a.shape[1:], lambda *_: idx, **kw)


def _rms(x, g):
    ms = jnp.mean(x * x, axis=-1, keepdims=True)
    return x * lax.rsqrt(ms + EPS) * g


def _silu(x):
    return x * jax.nn.sigmoid(x)


def _softplus_tail(x):
    return jnp.log1p(jnp.exp(-jnp.abs(x)))


def _nt(a, b):
    return lax.dot_general(a, b, (((1,), (1,)), ((), ())), preferred_element_type=F32)


def _scan_lanes(x):
    lane = lax.broadcasted_iota(jnp.int32, x.shape, 1)
    s = 1
    while s < x.shape[1]:
        x = x + jnp.where(lane >= s, pltpu.roll(x, s, 1), 0.0)
        s *= 2
    return x


def _scan_rows(x):
    row = lax.broadcasted_iota(jnp.int32, x.shape, 0)
    s = 1
    while s < x.shape[0]:
        x = x + jnp.where(row >= s, pltpu.roll(x, s, 0), 0.0)
        s *= 2
    return x


def _inproj_kernel(x_ref, g_ref, wm_ref, wst_ref, bs_ref, bst_ref, *rest, prompt, qscale, per):
    hb = _rms(x_ref[...], g_ref[...]).astype(BF16)
    hbt = hb.T

    if prompt:
        (wkt_ref, wvt_ref, _, _, qb_ref, kb_ref, kt_ref, vt_ref, vtb_ref, e_ref, u_ref, z_ref, xbc_ref, sm_ref,
         smt_ref, carry_sc) = rest
        kt = jnp.dot(wkt_ref[...], hbt, preferred_element_type=F32)
        kt_ref[...] = kt
        kb_ref[...] = kt.T.astype(BF16)
        vt = jnp.dot(wvt_ref[...], hbt, preferred_element_type=F32)
        vt_ref[...] = vt
        tm = vt.shape[1]
        ones_row = jnp.where(lax.broadcasted_iota(jnp.int32, (SUBLANE, tm), 0) == 0, 1.0, 0.0).astype(BF16)
        for h in range(H):
            vtb_ref[0, h * _VROWS:h * _VROWS + HD, :] = vt[h * HD:(h + 1) * HD].astype(BF16)
            vtb_ref[0, h * _VROWS + HD:(h + 1) * _VROWS, :] = ones_row
    else:
        wkv_ref, qb_ref, k_ref, kb_ref, v_ref, vb_ref, u_ref, z_ref, xbc_ref, sm_ref, smt_ref = rest
        kv = jnp.dot(hb, wkv_ref[...], preferred_element_type=F32)
        k_ref[...] = kv[:, :D_FOX]
        kb_ref[...] = kv[:, :D_FOX].astype(BF16)
        v_ref[...] = kv[:, D_FOX:]
        vb_ref[...] = kv[:, D_FOX:].astype(BF16)
    g0, g1, g2, g3 = (jnp.dot(hb, wm_ref[:, a:b], preferred_element_type=F32) for a, b in _GROUPS)
    qb_ref[...] = (g0[:, :_S0] * qscale).astype(BF16)
    u_ref[...] = g1
    z_ref[...] = g2[:, :_X0 - _Z0]
    xbc_ref[:, :LANE] = g2[:, _X0 - _Z0:]
    xbc_ref[:, LANE:] = g3

    sm = g0[:, _S0:] + bs_ref[...]
    lane = lax.broadcasted_iota(jnp.int32, sm.shape, 1)
    tail = _softplus_tail(sm)
    is_dt = (lane >= _DT_AT) & (lane < _DT_AT + 8)
    sm = jnp.where(is_dt, jnp.maximum(sm, 0.0) + tail, jnp.minimum(sm, 0.0) - tail)
    sm_ref[...] = sm
    if prompt:
        first = (pl.program_id(0) % per) == 0
        c = _scan_rows(sm) + jnp.where(first, 0.0, carry_sc[...])
        carry_sc[...] = c[-1:, :]
        lane1 = lane[:1]
        c2 = c * LOG2E
        hi = c2.astype(BF16).astype(F32)
        mid = (c2 - hi).astype(BF16).astype(F32)
        e = jnp.where(lane1 < 16, hi, jnp.where(lane1 < 32, mid, (c2 - hi) - mid))
        e_ref[...] = jnp.where(((lane1 & 15) < H) & (lane1 < 48), e, 0.0).astype(BF16)
    smt = jnp.dot(wst_ref[...], hbt, preferred_element_type=F32) + bst_ref[...]
    row = lax.broadcasted_iota(jnp.int32, smt.shape, 0)
    tail = _softplus_tail(smt)
    smt_ref[...] = jnp.where(row < _DT_AT, jnp.minimum(smt, 0.0) - tail, jnp.maximum(smt, 0.0) + tail)


def _inproj(x, w, l, tm, prompt_dims=None, kv_stacks=()):
    t = x.shape[0]
    nt = t // tm
    row = lambda c: pl.BlockSpec((tm, c), lambda i: (i, 0))
    sds = jax.ShapeDtypeStruct
    tail_shapes = (sds((t, D_POOL), F32), sds((t, D_SSD), F32), sds((t, D_CONV), F32),
                   sds((t, LANE), F32), sds((16, t), F32))
    tail_specs = (row(D_POOL), row(D_SSD), row(D_CONV), row(LANE), pl.BlockSpec((16, tm), lambda i: (0, i)))
    if prompt_dims is not None:
        nb, tp, lp = prompt_dims
        per = tp // tm
        tmin = pl.BlockSpec((None, None, D_FOX, tm), lambda i: (l, i // per, 0, i % per))
        out_shape = (sds((t, D_FOX), BF16), sds((t, D_FOX), BF16), sds(kv_stacks[0].shape, F32),
                     sds(kv_stacks[1].shape, F32), sds((nt, H * _VROWS, tm), BF16),
                     sds((t, LANE), BF16)) + tail_shapes
        out_specs = (row(D_FOX), row(D_FOX), tmin, tmin,
                     pl.BlockSpec((1, H * _VROWS, tm), lambda i: (i, 0, 0)), row(LANE)) + tail_specs
        qscale = (HD ** -0.5) * LOG2E
        scratch = [pltpu.VMEM((1, LANE), F32)]
    else:
        per = None
        out_shape = (sds((t, D_FOX), BF16), sds((t, D_FOX), F32), sds((t, D_FOX), BF16),
                     sds((t, D_FOX), F32), sds((t, D_FOX), BF16)) + tail_shapes
        out_specs = (row(D_FOX),) * 5 + tail_specs
        qscale = HD ** -0.5
        scratch = []
    names = ("g_pre", "wm", "wst", "bs", "bst") + (("wkt", "wvt") if prompt_dims is not None else ("wkv",))
    n_in = 1 + len(names)
    return pl.pallas_call(
        functools.partial(_inproj_kernel, prompt=prompt_dims is not None, qscale=qscale, per=per), grid=(nt,),
        in_specs=[row(D_MODEL)] + [_layer_spec(w[n], l, 1) for n in names]
                 + [pl.BlockSpec(memory_space=pl.ANY)] * len(kv_stacks),
        out_specs=out_specs, out_shape=out_shape, scratch_shapes=scratch,
        input_output_aliases={n_in + s: 2 + s for s in range(len(kv_stacks))},
        compiler_params=_cparams(("arbitrary",)), name="inproj",
    )(x, *[w[n] for n in names], *kv_stacks)


def _fox_prompt_kernel(q_ref, k_ref, e_ref, vt_ref, o_ref, qa_sc, m_sc, acc_sc, sa_sc, sb_sc, *, tq):
    i = pl.program_id(1)
    causal = (lax.broadcasted_iota(jnp.int32, (tq, tq), 0)
              <= lax.broadcasted_iota(jnp.int32, (tq, tq), 1))
    feat = lax.broadcasted_iota(jnp.int32, (LANE, 1), 0)
    for p in range(H // 2):
        qt = q_ref[:, p * LANE:(p + 1) * LANE].astype(F32).T
        for h in (2 * p, 2 * p + 1):
            mine = (feat < HD) if h % 2 == 0 else (feat >= HD)
            pick = (feat == _F_COPIES[0] + h) | (feat == _F_COPIES[1] + h) | (feat == _F_COPIES[2] + h)
            qa_sc[h, :LANE, :] = jnp.where(mine, qt, 0.0).astype(BF16)
            qa_sc[h, LANE:, :] = jnp.broadcast_to(jnp.where(pick, -1.0, 0.0), (LANE, tq)).astype(BF16)
    m_sc[...] = jnp.full(m_sc.shape, NEG, F32)
    acc_sc[...] = jnp.zeros(acc_sc.shape, F32)

    def logits_h(j, s_sc, h):
        rows = pl.ds(pl.multiple_of(j * tq, tq), tq)
        ka = jnp.concatenate([k_ref[rows, (h // 2) * LANE:(h // 2 + 1) * LANE], e_ref[rows, :]], axis=1)
        s_sc[h] = jnp.dot(ka, qa_sc[h], preferred_element_type=F32)

    def softmax_pv_h(j, s_sc, masked, h):
        s = s_sc[h]
        if masked:
            s = jnp.where(causal, s, NEG)
        m_old = m_sc[h:h + 1, :]
        m_new = jnp.maximum(m_old, jnp.max(s, axis=0, keepdims=True))
        m_sc[h:h + 1, :] = m_new
        pm = jnp.exp2(s - m_new).astype(BF16)
        hs = slice(h * _VROWS, (h + 1) * _VROWS)
        pv = jnp.dot(vt_ref[j, hs, :], pm, preferred_element_type=F32)
        acc_sc[hs, :] = jnp.exp2(m_old - m_new) * acc_sc[hs, :] + pv

    def logits(j, s_sc):
        for h in range(H):
            logits_h(j, s_sc, h)

    def softmax_pv(j, s_sc, masked):
        for h in range(H):
            softmax_pv_h(j, s_sc, masked, h)

    def two_steps(j):
        logits(j + 1, sb_sc)
        softmax_pv(j, sa_sc, False)
        logits(j + 2, sa_sc)
        softmax_pv(j + 1, sb_sc, False)

    def body4(jj, carry):
        two_steps(4 * jj)
        two_steps(4 * jj + 2)
        return carry

    def body2(jj, carry):
        two_steps(4 * (i // 4) + 2 * jj)
        return carry

    logits(0, sa_sc)
    lax.fori_loop(0, i // 4, body4, 0)
    lax.fori_loop(0, (i % 4) // 2, body2, 0)

    @pl.when(i % 2 == 0)
    def _():
        softmax_pv(i, sa_sc, True)

    @pl.when(i % 2 == 1)
    def _():
        logits(i, sb_sc)
        softmax_pv(i - 1, sa_sc, False)
        softmax_pv(i, sb_sc, True)

    out = jnp.concatenate([acc_sc[h * _VROWS:h * _VROWS + HD, :] / acc_sc[h * _VROWS + HD:h * _VROWS + HD + 1, :]
                           for h in range(H)], axis=0)
    o_ref[...] = out.T.astype(o_ref.dtype)


def _fox_prompt(qb, kb, eb, vt, nb, tp, tq):
    nq = tp // tq
    return pl.pallas_call(
        functools.partial(_fox_prompt_kernel, tq=tq), grid=(nb, nq),
        in_specs=[pl.BlockSpec((tq, D_FOX), lambda b, i: (b * nq + i, 0)),
                  pl.BlockSpec((tp, D_FOX), lambda b, i: (b, 0)),
                  pl.BlockSpec((tp, LANE), lambda b, i: (b, 0)),
                  pl.BlockSpec((nq, H * _VROWS, tq), lambda b, i: (b, 0, 0))],
        out_specs=pl.BlockSpec((tq, D_FOX), lambda b, i: (b * nq + i, 0)),
        out_shape=jax.ShapeDtypeStruct((nb * tp, D_FOX), BF16),
        scratch_shapes=[pltpu.VMEM((H, 2 * LANE, tq), BF16), pltpu.VMEM((8, tq), F32),
                        pltpu.VMEM((H * _VROWS, tq), F32),
                        pltpu.VMEM((H, tq, tq), F32), pltpu.VMEM((H, tq, tq), F32)],
        compiler_params=_cparams(("parallel", "arbitrary")), name="fox_prompt",
    )(qb, kb, eb, vt)


def _fox_sample_kernel(q_ref, kn_ref, vn_ref, kct_ref, vct_ref, lf_ref, o_ref, *, past, ls, nseq):
    for s in range(nseq):
        rows = slice(s * ls, (s + 1) * ls)
        _fox_sample_one(q_ref.at[rows], kn_ref.at[rows], vn_ref.at[rows], kct_ref.at[s], vct_ref.at[s],
                        lf_ref.at[s], o_ref.at[rows], past, ls)


def _fox_sample_one(q_ref, kn_ref, vn_ref, kct_ref, vct_ref, lf_ref, o_ref, past, ls):
    causal = (lax.broadcasted_iota(jnp.int32, (ls, ls), 1)
              <= lax.broadcasted_iota(jnp.int32, (ls, ls), 0))
    nblk = lf_ref.shape[1] // LANE
    loc = _scan_lanes(jnp.concatenate([lf_ref[:, j * LANE:(j + 1) * LANE] for j in range(nblk)], axis=0))
    carry = jnp.zeros((8, 1), F32)
    cs = []
    for j in range(nblk):
        blk = loc[j * 8:(j + 1) * 8]
        cs.append(blk + carry)
        carry = carry + blk[:, LANE - 1:LANE]
    c_all = jnp.concatenate(cs, axis=1)

    q = q_ref[...]
    lane = lax.broadcasted_iota(jnp.int32, (1, D_FOX), 1)
    own = [(lane >= h * HD) & (lane < (h + 1) * HD) for h in range(H)]
    q_all = jnp.concatenate([jnp.where(own[h], q, jnp.zeros_like(q)) for h in range(H)], axis=0)
    bias = [c_all[h:h + 1, past - 1:past] - c_all[h:h + 1, :] for h in range(H)]
    bias_c = jnp.concatenate([jnp.broadcast_to(b[:, :past], (ls, past)) for b in bias], axis=0)
    bias_n = jnp.concatenate([jnp.broadcast_to(b[:, past:past + ls], (ls, ls)) for b in bias], axis=0)
    s_c = jnp.dot(q_all, kct_ref[...].astype(BF16), preferred_element_type=F32) + bias_c
    s_n = jnp.where(jnp.concatenate([causal] * H, axis=0), _nt(q_all, kn_ref[...]) + bias_n, NEG)
    m = jnp.maximum(jnp.max(s_c, axis=-1, keepdims=True), jnp.max(s_n, axis=-1, keepdims=True))
    p_c = jnp.exp(s_c - m)
    p_n = jnp.exp(s_n - m)
    den = jnp.sum(p_c, axis=-1, keepdims=True) + jnp.sum(p_n, axis=-1, keepdims=True)
    o_all = (_nt(p_c.astype(BF16), vct_ref[...].astype(BF16))
             + jnp.dot(p_n.astype(BF16), vn_ref[...], preferred_element_type=F32)) / den
    o = jnp.zeros((ls, D_FOX), F32)
    for h in range(H):
        o = jnp.where(own[h], o_all[h * ls:(h + 1) * ls], o)
    o_ref[...] = o.astype(o_ref.dtype)


def _fox_sample(qb, kb, vb, kct, vct, lf_all, l, nb, ls, past):
    nseq = 4
    new = pl.BlockSpec((nseq * ls, D_FOX), lambda b: (b, 0))
    cache = pl.BlockSpec((None, nseq, D_FOX, past), lambda b: (l, b, 0, 0))
    return pl.pallas_call(
        functools.partial(_fox_sample_kernel, past=past, ls=ls, nseq=nseq), grid=(nb // nseq,),
        in_specs=[new, new, new, cache, cache, pl.BlockSpec((nseq, 8, past + LANE), lambda b: (b, 0, 0))],
        out_specs=new, out_shape=jax.ShapeDtypeStruct((nb * ls, D_FOX), BF16),
        compiler_params=_cparams(("parallel",)), name="fox_sample",
    )(qb, kb, vb, kct, vct, lf_all)


def _pool_tile(u, halo, w_ref, sc_ref, pos_start, tm):
    a = jnp.concatenate([halo, u], axis=0)
    e1 = a + pltpu.roll(a, 1, 0)
    e2 = e1 + pltpu.roll(e1, 2, 0)
    e3 = e2 + pltpu.roll(e2, 4, 0)
    e4 = e3 + pltpu.roll(e3, 8, 0)
    lane = lax.broadcasted_iota(jnp.int32, (1, D_POOL), 1)
    win = jnp.where(lane < 64, e1, jnp.where(lane < 128, e2, jnp.where(lane < 192, e3, e4)))[16:]
    wsz = jnp.where(lane < 64, 2.0, jnp.where(lane < 128, 4.0, jnp.where(lane < 192, 8.0, 16.0)))
    pos = (pos_start + lax.broadcasted_iota(jnp.int32, (tm, 1), 0)).astype(F32)
    diff = win / jnp.minimum(pos + 1.0, wsz) - u
    return jnp.dot(diff.astype(BF16), w_ref[...], preferred_element_type=F32) * sc_ref[...]


def _pool_kernel(u_ref, pre_ref, w_ref, sc_ref, o_ref, *, ls, nseq, pos0):
    for s in range(nseq):
        rows = slice(s * ls, (s + 1) * ls)
        o_ref[rows, :] = _pool_tile(u_ref[rows, :], pre_ref[s], w_ref, sc_ref, pos0, ls).astype(o_ref.dtype)


def _pool(u, prefix, w, l, nb, ls, pos0):
    nseq = 8
    tile = pl.BlockSpec((nseq * ls, D_POOL), lambda b: (b, 0))
    return pl.pallas_call(
        functools.partial(_pool_kernel, ls=ls, nseq=nseq, pos0=pos0), grid=(nb // nseq,),
        in_specs=[tile, pl.BlockSpec((None, nseq, 16, D_POOL), lambda b: (l, b, 0, 0)),
                  _layer_spec(w["wbd"], l, 1), _layer_spec(w["pscale"], l, 1)],
        out_specs=tile, out_shape=jax.ShapeDtypeStruct((nb * ls, D_POOL), BF16),
        compiler_params=_cparams(("parallel",)), name="pool_mixer",
    )(u, prefix, w["wbd"], w["pscale"])


def _ssd_chunk(xbc, z, sm, smt, ext_sc, st_sc, cw_ref, cb_ref, arow_ref, acol_ref, dsk_ref, gn_ref,
               row0, q, l_valid, live=None):
    ext_sc[SUBLANE:, :] = xbc
    w = cw_ref[...]
    conv = cb_ref[...]
    for j in range(CONV_W):
        conv = conv + w[j:j + 1] * ext_sc[pl.ds(SUBLANE - (CONV_W - 1) + j, q), :]
    ext_sc[:SUBLANE, :] = ext_sc[q:, :]
    act = _silu(conv)
    xs = act[:, :D_SSD]
    bbf = act[:, D_SSD:D_SSD + D_BC]
    bb = bbf.astype(BF16)
    cc = act[:, D_SSD + D_BC:].astype(BF16)
    bbt = bbf.T

    rvalid = (row0 + lax.broadcasted_iota(jnp.int32, (q, 1), 0)) < l_valid
    dtc = jnp.where(rvalid, sm, 0.0)
    acs_c = _scan_rows(dtc * (-jnp.exp(arow_ref[...])))
    cvalid = (row0 + lax.broadcasted_iota(jnp.int32, (1, LANE), 1)) < l_valid
    dtt = jnp.where(cvalid, smt, 0.0)
    acs_t = _scan_lanes(dtt * (-jnp.exp(acol_ref[...])))

    lane = lax.broadcasted_iota(jnp.int32, (1, LANE), 1)
    lo = lane < HD
    n_lo = lax.broadcasted_iota(jnp.int32, (LANE, 1), 0) < HD
    causal = (lax.broadcasted_iota(jnp.int32, (q, q), 1)
              <= lax.broadcasted_iota(jnp.int32, (q, q), 0))
    zc = jnp.zeros_like(cc)
    cb = [_nt(jnp.where(lo, cc, zc), bb), _nt(jnp.where(lo, zc, cc), bb)]
    ys = []
    for p in range(H // 2):
        cols = slice(p * LANE, (p + 1) * LANE)
        xp = xs[:, cols]
        xpb = xp.astype(BF16)
        st = st_sc[p]
        y_in = jnp.dot(cc, st.astype(BF16), preferred_element_type=F32)
        y_h, upd, dec = [], [], []
        for hh in range(2):
            h = 2 * p + hh
            g = h // (H // 2)
            arep = jnp.broadcast_to(acs_c[:, _DT_AT + h:_DT_AT + h + 1], (q, LANE))
            ak = acs_t[_DT_AT + h:_DT_AT + h + 1, :q]
            dtr = dtt[_DT_AT + h:_DT_AT + h + 1, :q]
            lm = jnp.exp(jnp.where(causal, arep[:, :q] - ak, NEG))
            gm = (cb[g] * lm * dtr).astype(BF16)
            y_h.append(jnp.dot(gm, xpb, preferred_element_type=F32) + y_in * jnp.exp(arep))
            alast = acs_t[_DT_AT + h:_DT_AT + h + 1, q - 1:q]
            bw = (bbt * (jnp.exp(alast - ak) * dtr)).astype(BF16)
            keep = (lo if hh == 0 else ~lo) & (n_lo if g == 0 else ~n_lo)
            upd.append(jnp.where(keep, jnp.dot(bw, xpb, preferred_element_type=F32), 0.0))
            dec.append(jnp.exp(alast))
        st_new = jnp.where(lo, dec[0], dec[1]) * st + upd[0] + upd[1]
        st_sc[p] = st_new if live is None else jnp.where(live, st_new, st)
        ys.append(jnp.where(lo, y_h[0], y_h[1]) + dsk_ref[:, cols] * xp)
    yc = jnp.concatenate(ys, axis=1) * _silu(z)
    return _rms(yc, gn_ref[...])


def _ssd_kernel(xbc_ref, z_ref, sm_ref, smt_ref, pre_ref, init_ref, cw_ref, cb_ref, arow_ref,
                acol_ref, dsk_ref, gn_ref, y_ref, fin_ref, ext_sc, st_sc, *, q, nseq):
    for s in range(nseq):
        rows = slice(s * q, (s + 1) * q)
        ext_sc[:SUBLANE, :] = pre_ref[s]
        st_sc[...] = init_ref[s]
        y = _ssd_chunk(xbc_ref[rows, :], z_ref[rows, :], sm_ref[rows, :], smt_ref[s], ext_sc, st_sc, cw_ref,
                       cb_ref, arow_ref, acol_ref, dsk_ref, gn_ref, 0, q, q)
        y_ref[rows, :] = y.astype(y_ref.dtype)
        fin_ref[s] = st_sc[...]


def _ssd(xbc, z, sm, smt, prefix, init, w, l, nb, q):
    nseq = 4
    rows = lambda c: pl.BlockSpec((nseq * q, c), lambda b: (b, 0))
    names = ("cw", "cb", "arow", "acol", "dsk", "gn")
    state = pl.BlockSpec((nseq, H // 2, LANE, LANE), lambda b: (b, 0, 0, 0))
    return pl.pallas_call(
        functools.partial(_ssd_kernel, q=q, nseq=nseq), grid=(nb // nseq,),
        in_specs=[rows(D_CONV), rows(D_SSD), rows(LANE), pl.BlockSpec((nseq, 16, LANE), lambda b: (b, 0, 0)),
                  pl.BlockSpec((None, nseq, 8, D_CONV), lambda b: (l, b, 0, 0)),
                  pl.BlockSpec((None, nseq, H // 2, LANE, LANE), lambda b: (l, b, 0, 0, 0))]
                 + [_layer_spec(w[n], l, 1) for n in names],
        out_specs=(rows(D_SSD), state),
        out_shape=(jax.ShapeDtypeStruct((nb * q, D_SSD), BF16),
                   jax.ShapeDtypeStruct((nb, H // 2, LANE, LANE), F32)),
        scratch_shapes=[pltpu.VMEM((q + SUBLANE, D_CONV), F32), pltpu.VMEM((H // 2, LANE, LANE), F32)],
        compiler_params=_cparams(("parallel",)), name="conv_ssd",
    )(xbc, z, sm, smt, prefix, init, *[w[n] for n in names])


def _state_to_wide(s):
    def placed(h):
        g, hh = h // (H // 2), h % 2
        return jnp.pad(jnp.swapaxes(s[:, h].astype(F32), 1, 2),
                       ((0, 0), (g * HD, LANE - (g + 1) * HD), (hh * HD, LANE - (hh + 1) * HD)))
    return jnp.stack([placed(2 * p) + placed(2 * p + 1) for p in range(H // 2)], axis=1)


def _state_from_wide(wide):
    return jnp.stack([jnp.swapaxes(wide[:, h // 2, (h // (H // 2)) * HD:(h // (H // 2) + 1) * HD,
                                        (h % 2) * HD:(h % 2 + 1) * HD], 1, 2) for h in range(H)], axis=1)


def _ffn_tile(x, a, p, s, wo_ref, g1_ref, g2_ref, g3_ref, wg_ref, wu_ref, wd_ref):
    mp = (jnp.dot(a, wo_ref[:D_FOX, :], preferred_element_type=F32)
          + jnp.dot(p, wo_ref[D_FOX:D_FOX + D_POOL, :], preferred_element_type=F32)
          + jnp.dot(s, wo_ref[D_FOX + D_POOL:, :], preferred_element_type=F32))
    x1 = x + _rms(mp, g1_ref[...])
    hb = _rms(x1, g2_ref[...]).astype(BF16)
    gate = jnp.dot(hb, wg_ref[...], preferred_element_type=F32)
    up = jnp.dot(hb, wu_ref[...], preferred_element_type=F32)
    act = (_silu(gate) * up).astype(BF16)
    ff = jnp.dot(act, wd_ref[...], preferred_element_type=F32)
    return x1 + _rms(ff, g3_ref[...])


def _ffn_kernel(x_ref, a_ref, p_ref, s_ref, *rest):
    o_ref = rest[-1]
    o_ref[...] = _ffn_tile(x_ref[...], a_ref[...], p_ref[...], s_ref[...], *rest[:-1])


def _mix_ffn_kernel(x_ref, a_ref, u_ref, xbc_ref, z_ref, sm_ref, smt_ref, wbd_ref, psc_ref,
                    cw_ref, cb_ref, arow_ref, acol_ref, dsk_ref, gn_ref,
                    wo_ref, g1_ref, g2_ref, g3_ref, wg_ref, wu_ref, wd_ref,
                    o_ref, fin_ref, halo_sc, ext_sc, st_sc, pool_sc, ssd_sc, *, tm, nsub, q, per, ng, l_valid):
    i = pl.program_id(0)
    live = i < ng
    ic = jnp.minimum(i, ng - 1)

    @pl.when(i == 0)
    def _():
        pool_sc[...] = jnp.zeros(pool_sc.shape, pool_sc.dtype)
        ssd_sc[...] = jnp.zeros(ssd_sc.shape, ssd_sc.dtype)

    o_ref[...] = _ffn_tile(x_ref[...], a_ref[...], pool_sc[...], ssd_sc[...], wo_ref, g1_ref, g2_ref, g3_ref,
                           wg_ref, wu_ref, wd_ref)

    for s in range(nsub):
        t = ic * nsub + s
        first = (t % per) == 0
        row0 = (t % per) * tm
        tile = slice(s * tm, (s + 1) * tm)
        u = u_ref[tile, :]
        halo = jnp.where(first, 0.0, halo_sc[...])
        pool_sc[tile, :] = _pool_tile(u, halo, wbd_ref, psc_ref, row0, tm).astype(pool_sc.dtype)
        halo_sc[...] = u[tm - 16:]
        ext_sc[:SUBLANE, :] = jnp.where(first, 0.0, ext_sc[:SUBLANE, :])
        st_sc[...] = jnp.where(first, 0.0, st_sc[...])
        for c in range(tm // q):
            rows = slice(s * tm + c * q, s * tm + (c + 1) * q)
            y = _ssd_chunk(xbc_ref[rows, :], z_ref[rows, :], sm_ref[rows, :], smt_ref[:, rows],
                           ext_sc, st_sc, cw_ref, cb_ref, arow_ref, acol_ref, dsk_ref, gn_ref,
                           row0 + c * q, q, l_valid, live)
            ssd_sc[rows, :] = y.astype(ssd_sc.dtype)
        fin_ref[t // per] = st_sc[...]


def _mix_ffn(x, attn, u, xbc, z, sm, smt, w, l, nb, tp, tm, q, l_valid):
    nsub = 2
    per = tp // tm
    assert q == LANE and tm % q == 0 and (nb * per) % nsub == 0
    ng = nb * per // nsub
    rows = nsub * tm
    prev = lambda c: pl.BlockSpec((rows, c), lambda i: (jnp.maximum(i - 1, 0), 0))
    cur = lambda c: pl.BlockSpec((rows, c), lambda i: (jnp.minimum(i, ng - 1), 0))
    names = ("wbd", "pscale", "cw", "cb", "arow", "acol", "dsk", "gn")
    big = ("wo", "g1", "g2", "g3", "wg", "wu", "wd")
    state_shape = (nb, H // 2, LANE, LANE)
    return pl.pallas_call(
        functools.partial(_mix_ffn_kernel, tm=tm, nsub=nsub, q=q, per=per, ng=ng, l_valid=l_valid),
        grid=(ng + 1,),
        in_specs=[prev(D_MODEL), prev(D_FOX), cur(D_POOL), cur(D_CONV), cur(D_SSD), cur(LANE),
                  pl.BlockSpec((16, rows), lambda i: (0, jnp.minimum(i, ng - 1)))]
                 + [_layer_spec(w[n], l, 1) for n in names]
                 + [_layer_spec(w[n], l, 1, single_buffer=True) for n in big],
        out_specs=(prev(D_MODEL), pl.BlockSpec(state_shape, lambda i: (0, 0, 0, 0))),
        out_shape=(jax.ShapeDtypeStruct((ng * rows, D_MODEL), F32), jax.ShapeDtypeStruct(state_shape, F32)),
        scratch_shapes=[pltpu.VMEM((16, D_POOL), F32), pltpu.VMEM((q + SUBLANE, D_CONV), F32),
                        pltpu.VMEM((H // 2, LANE, LANE), F32),
                        pltpu.VMEM((rows, D_POOL), BF16), pltpu.VMEM((rows, D_SSD), BF16)],
        compiler_params=_cparams(("arbitrary",)), name="mix_ffn",
    )(x, attn, u, xbc, z, sm, smt, *[w[n] for n in names], *[w[n] for n in big])


def _ffn(x, attn, pool, ssd, w, l, tm):
    t = x.shape[0]
    row = lambda c: pl.BlockSpec((tm, c), lambda i: (i, 0))
    names = ("wo", "g1", "g2", "g3", "wg", "wu", "wd")
    return pl.pallas_call(
        _ffn_kernel, grid=(t // tm,),
        in_specs=[row(D_MODEL), row(D_FOX), row(D_POOL), row(D_SSD)]
                 + [_layer_spec(w[n], l, 1, single_buffer=True) for n in names],
        out_specs=row(D_MODEL), out_shape=jax.ShapeDtypeStruct((t, D_MODEL), F32),
        compiler_params=_cparams(("parallel",)), name="outproj_ffn",
    )(x, attn, pool, ssd, *[w[n] for n in names])


def _lane_pack(f_vals, dt_vals):
    pieces, at = [], 0
    for start, vals in sorted([(_DT_AT, dt_vals)] + [(c, f_vals) for c in _F_COPIES], key=lambda t: t[0]):
        pieces += [jnp.zeros(vals.shape[:-1] + (start - at,), F32), vals.astype(F32)]
        at = start + H
    pieces.append(jnp.zeros(f_vals.shape[:-1] + (LANE - at,), F32))
    return jnp.concatenate(pieces, axis=-1)


def _prep_weights(w_in, fox_f_bias, pool_w, pool_scale, conv_w, conv_b, dt_bias, a_log, d_skip, ssd_norm,
                  w_out, w_gate, w_up, w_down, ln_pre_mix, ln_post_mix, ln_pre_ffn, ln_post_ffn):
    depth = w_in.shape[0]
    f0 = 3 * D_FOX
    u0 = f0 + H
    dt0 = u0 + D_POOL + D_SSD + D_CONV
    ws = _lane_pack(w_in[:, :, f0:u0], w_in[:, :, dt0:dt0 + H])
    wm = jnp.concatenate([w_in[:, :, :D_FOX], ws, w_in[:, :, u0:dt0]], axis=2).astype(BF16)
    ws = ws.astype(BF16)
    bs = _lane_pack(fox_f_bias, dt_bias)
    wbd = jnp.concatenate([jnp.pad(pool_w[:, g].astype(F32), ((0, 0), (0, 0), (g * 64, D_POOL - (g + 1) * 64)))
                           for g in range(D_POOL // 64)], axis=1)
    alog = _lane_pack(jnp.zeros_like(a_log), a_log)
    row = lambda a: a.astype(F32).reshape(depth, 1, -1)
    tr = lambda a: jnp.transpose(a, (0, 2, 1))
    return dict(
        g_pre=row(ln_pre_mix), wm=wm, wkv=w_in[:, :, D_FOX:f0].astype(BF16),
        wkt=tr(w_in[:, :, D_FOX:2 * D_FOX]).astype(BF16), wvt=tr(w_in[:, :, 2 * D_FOX:f0]).astype(BF16),
        wst=tr(ws[:, :, :16]),
        bs=bs.reshape(depth, 1, LANE), bst=bs[:, :16].reshape(depth, 16, 1),
        wbd=wbd.astype(BF16), pscale=row(pool_scale),
        cw=jnp.pad(conv_w.astype(F32), ((0, 0), (0, SUBLANE - CONV_W), (0, 0))), cb=row(conv_b),
        arow=alog.reshape(depth, 1, LANE), acol=alog[:, :16].reshape(depth, 16, 1),
        dsk=row(jnp.repeat(d_skip, HD, axis=1)), gn=row(ssd_norm),
        wo=w_out.astype(BF16), g1=row(ln_post_mix), g2=row(ln_pre_ffn), g3=row(ln_post_ffn),
        wg=w_gate.astype(BF16), wu=w_up.astype(BF16), wd=w_down.astype(BF16))


def _prompt_layer(x, w, l, nb, tp, lp, tm, q, kv_stacks):
    qb, kb, kt, vt, vtb, eb, u, z, xbc, sm, smt = _inproj(x, w, l, tm, (nb, tp, lp), kv_stacks)
    attn = _fox_prompt(qb, kb, eb, vtb, nb, tp, tm)
    x, fin = _mix_ffn(x, attn, u, xbc, z, sm, smt, w, l, nb, tp, tm, q, lp)
    return x, (kt, vt), (smt, u, xbc, fin)


def _sample_layer(x, w, l, nb, ls, past, tm, kct, vct, lfc, pool_pre, conv_pre, ssd_init):
    qb, k, kb, v, vb, u, z, xbc, sm, smt = _inproj(x, w, l, tm)
    smt_seq = jnp.transpose(smt.reshape(16, nb, ls), (1, 0, 2))
    lf_all = jnp.concatenate([lfc, smt_seq[:, :8], jnp.zeros((nb, 8, LANE - ls), F32)], axis=2)
    attn = _fox_sample(qb, kb, vb, kct, vct, lf_all, l, nb, ls, past)
    pool = _pool(u, pool_pre, w, l, nb, ls, past)
    ssd, fin = _ssd(xbc, z, sm, jnp.pad(smt_seq, ((0, 0), (0, 0), (0, LANE - ls))), conv_pre, ssd_init, w, l,
                    nb, ls)
    x = _ffn(x, attn, pool, ssd, w, l, tm)
    return x, (k, v), (smt, u, xbc, fin)


def kernel(x_prompt, x_sample, cache_fox_k, cache_fox_v, cache_fox_logf, state_pool, state_conv, state_ssd,
           meta_tokens, ln_pre_mix, ln_post_mix, ln_pre_ffn, ln_post_ffn, w_in, fox_f_bias, pool_w, pool_scale,
           conv_w, conv_b, dt_bias, a_log, d_skip, ssd_norm, w_out, w_gate, w_up, w_down):
    nbp, seq, _ = x_prompt.shape
    nbs, ls, _ = x_sample.shape
    depth, _, past = cache_fox_logf.shape[:3]
    lp = N_META + seq
    tq = 256
    q_ssd = 128
    tp = -(-lp // tq) * tq

    w = _prep_weights(w_in, fox_f_bias, pool_w, pool_scale, conv_w, conv_b, dt_bias, a_log, d_skip, ssd_norm,
                      w_out, w_gate, w_up, w_down, ln_pre_mix, ln_post_mix, ln_pre_ffn, ln_post_ffn)

    meta = jnp.broadcast_to(meta_tokens.astype(F32)[None], (nbp, N_META, D_MODEL))
    xp = jnp.concatenate([meta, x_prompt, jnp.zeros((nbp, tp - lp, D_MODEL), F32)], axis=1)
    xp = xp.reshape(nbp * tp, D_MODEL)
    xs = x_sample.reshape(nbs * ls, D_MODEL)

    ssd_init =_state_to_wide(state_ssd.reshape(depth * nbs, H, HD, HD)).reshape(depth, nbs, H // 2, LANE, LANE)

    kct = jnp.transpose(cache_fox_k, (0, 1, 3, 4, 2)).reshape(depth, nbs, D_FOX, past)
    vct = jnp.transpose(cache_fox_v, (0, 1, 3, 4, 2)).reshape(depth, nbs, D_FOX, past)
    lfc = jnp.pad(jnp.transpose(cache_fox_logf.astype(F32), (0, 1, 3, 2)),
                  ((0, 0), (0, 0), (0, 8 - H), (0, 0)))
    pool_pre = jnp.pad(state_pool.astype(F32), ((0, 0), (0, 0), (16 - POOL_BUF, 0), (0, 0)))
    conv_pre = jnp.pad(state_conv.astype(F32), ((0, 0), (0, 0), (8 - (CONV_W - 1), 0), (0, 0)))

    outs_p, outs_s, kv_s = [], [], []
    kv_stacks = tuple(jnp.zeros((depth, nbp, D_FOX, lp), F32) for _ in range(2))
    for l in range(depth):
        xp, kv_stacks, st = _prompt_layer(xp, w, l, nbp, tp, lp, tq, q_ssd, kv_stacks)
        outs_p.append(st)
        xs, kv, st = _sample_layer(xs, w, l, nbs, ls, past, tq, kct, vct, lfc[l], pool_pre, conv_pre, ssd_init)
        outs_s.append(st)
        kv_s.append(kv)

    def tails(outs, nb, rows, valid):
        seqv = lambda a: a.reshape(nb, rows, a.shape[-1])
        lf = jnp.stack([o[0][:H] for o in outs]).reshape(depth, H, nb, rows)[:, :, :, :valid]
        pn = jnp.stack([seqv(o[1])[:, valid - POOL_BUF:valid] for o in outs])
        cn = jnp.stack([seqv(o[2])[:, valid - (CONV_W - 1):valid] for o in outs])
        sn = _state_from_wide(jnp.concatenate([o[3] for o in outs], axis=0)).reshape(depth, nb, H, HD, HD)
        return jnp.transpose(lf, (0, 2, 3, 1)), pn, cn, sn

    def token_minor(a):
        return jnp.transpose(a.reshape(depth, nbp, H, HD, lp), (0, 1, 4, 2, 3))

    def token_major(i):
        return jnp.stack([kv[i].reshape(nbs, ls, H, HD) for kv in kv_s])

    y_prompt = xp.reshape(nbp, tp, D_MODEL)[:, N_META:lp]
    y_sample = xs.reshape(nbs, ls, D_MODEL)
    return ((y_prompt, y_sample, token_minor(kv_stacks[0]), token_minor(kv_stacks[1])) + tails(outs_p, nbp, tp, lp)
            + (token_major(0), token_major(1)) + tails(outs_s, nbs, ls, ls))
```

```python
import functools
import math

import jax
import jax.numpy as jnp
from jax import lax
from jax.experimental import pallas as pl
from jax.experimental.pallas import tpu as pltpu

F32 = jnp.float32
BF16 = jnp.bfloat16

D_MODEL = 1024
N_META = 16
EPS = 1e-6
H = 6
HD = 64
D_FOX = H * HD
D_POOL = 256
POOL_BUF = 15
D_SSD = H * HD
D_BC = 128
D_CONV = D_SSD + 2 * D_BC
CONV_W = 4
LANE = 128
SUBLANE = 8
NEG = -1e30
LOG2E = math.log2(math.e)

_Q0, _S0, _U0, _Z0, _X0, _MAIN = 0, 384, 512, 768, 1152, 1792
_MXU_N = 256
_GROUPS = ((_Q0, _U0), (_U0, _Z0), (_Z0, _X0 + LANE), (_X0 + LANE, _MAIN))
assert all((b - a) % _MXU_N == 0 for a, b in _GROUPS)
_DT_AT, _F_COPIES = 8, (0, 16, 32)
_VROWS = HD + SUBLANE

_VMEM_LIMIT = 56 * 1024 * 1024


def _cparams(sem):
    return pltpu.CompilerParams(dimension_semantics=sem, vmem_limit_bytes=_VMEM_LIMIT)


def _layer_spec(a, l, ngrid, single_buffer=False):
    idx = (l,) + (0,) * (a.ndim - 1)
    kw = dict(pipeline_mode=pl.Buffered(1)) if single_buffer else {}
    return pl.BlockSpec((None,) + a.shape[1:], lambda *_: idx, **kw)


def _rms(x, g):
    ms = jnp.mean(x * x, axis=-1, keepdims=True)
    return x * lax.rsqrt(ms + EPS) * g


def _silu(x):
    return x * jax.nn.sigmoid(x)


def _softplus_tail(x):
    return jnp.log1p(jnp.exp(-jnp.abs(x)))


def _nt(a, b):
    return lax.dot_general(a, b, (((1,), (1,)), ((), ())), preferred_element_type=F32)


def _scan_lanes(x):
    lane = lax.broadcasted_iota(jnp.int32, x.shape, 1)
    s = 1
    while s < x.shape[1]:
        x = x + jnp.where(lane >= s, pltpu.roll(x, s, 1), 0.0)
        s *= 2
    return x


def _scan_rows(x):
    row = lax.broadcasted_iota(jnp.int32, x.shape, 0)
    s = 1
    while s < x.shape[0]:
        x = x + jnp.where(row >= s, pltpu.roll(x, s, 0), 0.0)
        s *= 2
    return x


def _inproj_kernel(x_ref, g_ref, wm_ref, wst_ref, bs_ref, bst_ref, *rest, prompt, qscale, per):
    hb = _rms(x_ref[...], g_ref[...]).astype(BF16)
    hbt = hb.T

    if prompt:
        (wkt_ref, wvt_ref, _, _, qb_ref, kb_ref, kt_ref, vt_ref, vtb_ref, e_ref, u_ref, z_ref, xbc_ref, sm_ref,
         smt_ref, carry_sc) = rest
        kt = jnp.dot(wkt_ref[...], hbt, preferred_element_type=F32)
        kt_ref[...] = kt
        kb_ref[...] = kt.T.astype(BF16)
        vt = jnp.dot(wvt_ref[...], hbt, preferred_element_type=F32)
        vt_ref[...] = vt
        tm = vt.shape[1]
        ones_row = jnp.where(lax.broadcasted_iota(jnp.int32, (SUBLANE, tm), 0) == 0, 1.0, 0.0).astype(BF16)
        for h in range(H):
            vtb_ref[0, h * _VROWS:h * _VROWS + HD, :] = vt[h * HD:(h + 1) * HD].astype(BF16)
            vtb_ref[0, h * _VROWS + HD:(h + 1) * _VROWS, :] = ones_row
    else:
        wkv_ref, qb_ref, k_ref, kb_ref, v_ref, vb_ref, u_ref, z_ref, xbc_ref, sm_ref, smt_ref = rest
        kv = jnp.dot(hb, wkv_ref[...], preferred_element_type=F32)
        k_ref[...] = kv[:, :D_FOX]
        kb_ref[...] = kv[:, :D_FOX].astype(BF16)
        v_ref[...] = kv[:, D_FOX:]
        vb_ref[...] = kv[:, D_FOX:].astype(BF16)
    g0, g1, g2, g3 = (jnp.dot(hb, wm_ref[:, a:b], preferred_element_type=F32) for a, b in _GROUPS)
    qb_ref[...] = (g0[:, :_S0] * qscale).astype(BF16)
    u_ref[...] = g1
    z_ref[...] = g2[:, :_X0 - _Z0]
    xbc_ref[:, :LANE] = g2[:, _X0 - _Z0:]
    xbc_ref[:, LANE:] = g3

    sm = g0[:, _S0:] + bs_ref[...]
    lane = lax.broadcasted_iota(jnp.int32, sm.shape, 1)
    tail = _softplus_tail(sm)
    is_dt = (lane >= _DT_AT) & (lane < _DT_AT + 8)
    sm = jnp.where(is_dt, jnp.maximum(sm, 0.0) + tail, jnp.minimum(sm, 0.0) - tail)
    sm_ref[...] = sm
    if prompt:
        first = (pl.program_id(0) % per) == 0
        c = _scan_rows(sm) + jnp.where(first, 0.0, carry_sc[...])
        carry_sc[...] = c[-1:, :]
        lane1 = lane[:1]
        c2 = c * LOG2E
        hi = c2.astype(BF16).astype(F32)
        mid = (c2 - hi).astype(BF16).astype(F32)
        e = jnp.where(lane1 < 16, hi, jnp.where(lane1 < 32, mid, (c2 - hi) - mid))
        e_ref[...] = jnp.where(((lane1 & 15) < H) & (lane1 < 48), e, 0.0).astype(BF16)
    smt = jnp.dot(wst_ref[...], hbt, preferred_element_type=F32) + bst_ref[...]
    row = lax.broadcasted_iota(jnp.int32, smt.shape, 0)
    tail = _softplus_tail(smt)
    smt_ref[...] = jnp.where(row < _DT_AT, jnp.minimum(smt, 0.0) - tail, jnp.maximum(smt, 0.0) + tail)


def _inproj(x, w, l, tm, prompt_dims=None, kv_stacks=()):
    t = x.shape[0]
    nt = t // tm
    row = lambda c: pl.BlockSpec((tm, c), lambda i: (i, 0))
    sds = jax.ShapeDtypeStruct
    tail_shapes = (sds((t, D_POOL), F32), sds((t, D_SSD), F32), sds((t, D_CONV), F32),
                   sds((t, LANE), F32), sds((16, t), F32))
    tail_specs = (row(D_POOL), row(D_SSD), row(D_CONV), row(LANE), pl.BlockSpec((16, tm), lambda i: (0, i)))
    if prompt_dims is not None:
        nb, tp, lp = prompt_dims
        per = tp // tm
        tmin = pl.BlockSpec((None, None, D_FOX, tm), lambda i: (l, i // per, 0, i % per))
        out_shape = (sds((t, D_FOX), BF16), sds((t, D_FOX), BF16), sds(kv_stacks[0].shape, F32),
                     sds(kv_stacks[1].shape, F32), sds((nt, H * _VROWS, tm), BF16),
                     sds((t, LANE), BF16)) + tail_shapes
        out_specs = (row(D_FOX), row(D_FOX), tmin, tmin,
                     pl.BlockSpec((1, H * _VROWS, tm), lambda i: (i, 0, 0)), row(LANE)) + tail_specs
        qscale = (HD ** -0.5) * LOG2E
        scratch = [pltpu.VMEM((1, LANE), F32)]
    else:
        per = None
        out_shape = (sds((t, D_FOX), BF16), sds((t, D_FOX), F32), sds((t, D_FOX), BF16),
                     sds((t, D_FOX), F32), sds((t, D_FOX), BF16)) + tail_shapes
        out_specs = (row(D_FOX),) * 5 + tail_specs
        qscale = HD ** -0.5
        scratch = []
    names = ("g_pre", "wm", "wst", "bs", "bst") + (("wkt", "wvt") if prompt_dims is not None else ("wkv",))
    n_in = 1 + len(names)
    return pl.pallas_call(
        functools.partial(_inproj_kernel, prompt=prompt_dims is not None, qscale=qscale, per=per), grid=(nt,),
        in_specs=[row(D_MODEL)] + [_layer_spec(w[n], l, 1) for n in names]
                 + [pl.BlockSpec(memory_space=pl.ANY)] * len(kv_stacks),
        out_specs=out_specs, out_shape=out_shape, scratch_shapes=scratch,
        input_output_aliases={n_in + s: 2 + s for s in range(len(kv_stacks))},
        compiler_params=_cparams(("arbitrary",)), name="inproj",
    )(x, *[w[n] for n in names], *kv_stacks)


def _fox_prompt_kernel(q_ref, k_ref, e_ref, vt_ref, o_ref, qa_sc, m_sc, acc_sc, sa_sc, sb_sc, *, tq):
    i = pl.program_id(1)
    causal = (lax.broadcasted_iota(jnp.int32, (tq, tq), 0)
              <= lax.broadcasted_iota(jnp.int32, (tq, tq), 1))
    feat = lax.broadcasted_iota(jnp.int32, (LANE, 1), 0)
    for p in range(H // 2):
        qt = q_ref[:, p * LANE:(p + 1) * LANE].astype(F32).T
        for h in (2 * p, 2 * p + 1):
            mine = (feat < HD) if h % 2 == 0 else (feat >= HD)
            pick = (feat == _F_COPIES[0] + h) | (feat == _F_COPIES[1] + h) | (feat == _F_COPIES[2] + h)
            qa_sc[h, :LANE, :] = jnp.where(mine, qt, 0.0).astype(BF16)
            qa_sc[h, LANE:, :] = jnp.broadcast_to(jnp.where(pick, -1.0, 0.0), (LANE, tq)).astype(BF16)
    m_sc[...] = jnp.full(m_sc.shape, NEG, F32)
    acc_sc[...] = jnp.zeros(acc_sc.shape, F32)

    def logits_h(j, s_sc, h):
        rows = pl.ds(pl.multiple_of(j * tq, tq), tq)
        ka = jnp.concatenate([k_ref[rows, (h // 2) * LANE:(h // 2 + 1) * LANE], e_ref[rows, :]], axis=1)
        s_sc[h] = jnp.dot(ka, qa_sc[h], preferred_element_type=F32)

    def softmax_pv_h(j, s_sc, masked, h):
        s = s_sc[h]
        if masked:
            s = jnp.where(causal, s, NEG)
        m_old = m_sc[h:h + 1, :]
        m_new = jnp.maximum(m_old, jnp.max(s, axis=0, keepdims=True))
        m_sc[h:h + 1, :] = m_new
        pm = jnp.exp2(s - m_new).astype(BF16)
        hs = slice(h * _VROWS, (h + 1) * _VROWS)
        pv = jnp.dot(vt_ref[j, hs, :], pm, preferred_element_type=F32)
        acc_sc[hs, :] = jnp.exp2(m_old - m_new) * acc_sc[hs, :] + pv

    def logits(j, s_sc):
        for h in range(H):
            logits_h(j, s_sc, h)

    def softmax_pv(j, s_sc, masked):
        for h in range(H):
            softmax_pv_h(j, s_sc, masked, h)

    def two_steps(j):
        logits(j + 1, sb_sc)
        softmax_pv(j, sa_sc, False)
        logits(j + 2, sa_sc)
        softmax_pv(j + 1, sb_sc, False)

    def body4(jj, carry):
        two_steps(4 * jj)
        two_steps(4 * jj + 2)
        return carry

    def body2(jj, carry):
        two_steps(4 * (i // 4) + 2 * jj)
        return carry

    logits(0, sa_sc)
    lax.fori_loop(0, i // 4, body4, 0)
    lax.fori_loop(0, (i % 4) // 2, body2, 0)

    @pl.when(i % 2 == 0)
    def _():
        softmax_pv(i, sa_sc, True)

    @pl.when(i % 2 == 1)
    def _():
        logits(i, sb_sc)
        softmax_pv(i - 1, sa_sc, False)
        softmax_pv(i, sb_sc, True)

    out = jnp.concatenate([acc_sc[h * _VROWS:h * _VROWS + HD, :] / acc_sc[h * _VROWS + HD:h * _VROWS + HD + 1, :]
                           for h in range(H)], axis=0)
    o_ref[...] = out.astype(o_ref.dtype).T


def _fox_prompt(qb, kb, eb, vt, nb, tp, tq):
    nq = tp // tq
    return pl.pallas_call(
        functools.partial(_fox_prompt_kernel, tq=tq), grid=(nb, nq),
        in_specs=[pl.BlockSpec((tq, D_FOX), lambda b, i: (b * nq + i, 0)),
                  pl.BlockSpec((tp, D_FOX), lambda b, i: (b, 0)),
                  pl.BlockSpec((tp, LANE), lambda b, i: (b, 0)),
                  pl.BlockSpec((nq, H * _VROWS, tq), lambda b, i: (b, 0, 0))],
        out_specs=pl.BlockSpec((tq, D_FOX), lambda b, i: (b * nq + i, 0)),
        out_shape=jax.ShapeDtypeStruct((nb * tp, D_FOX), BF16),
        scratch_shapes=[pltpu.VMEM((H, 2 * LANE, tq), BF16), pltpu.VMEM((8, tq), F32),
                        pltpu.VMEM((H * _VROWS, tq), F32),
                        pltpu.VMEM((H, tq, tq), F32), pltpu.VMEM((H, tq, tq), F32)],
        compiler_params=_cparams(("parallel", "arbitrary")), name="fox_prompt",
    )(qb, kb, eb, vt)


def _fox_sample_kernel(q_ref, kn_ref, vn_ref, kct_ref, vct_ref, lf_ref, o_ref, *, past, ls, nseq):
    for s in range(nseq):
        rows = slice(s * ls, (s + 1) * ls)
        _fox_sample_one(q_ref.at[rows], kn_ref.at[rows], vn_ref.at[rows], kct_ref.at[s], vct_ref.at[s],
                        lf_ref.at[s], o_ref.at[rows], past, ls)


def _fox_sample_one(q_ref, kn_ref, vn_ref, kct_ref, vct_ref, lf_ref, o_ref, past, ls):
    causal = (lax.broadcasted_iota(jnp.int32, (ls, ls), 1)
              <= lax.broadcasted_iota(jnp.int32, (ls, ls), 0))
    nblk = lf_ref.shape[1] // LANE
    loc = _scan_lanes(jnp.concatenate([lf_ref[:, j * LANE:(j + 1) * LANE] for j in range(nblk)], axis=0))
    carry = jnp.zeros((8, 1), F32)
    cs = []
    for j in range(nblk):
        blk = loc[j * 8:(j + 1) * 8]
        cs.append(blk + carry)
        carry = carry + blk[:, LANE - 1:LANE]
    c_all = jnp.concatenate(cs, axis=1)

    q = q_ref[...]
    lane = lax.broadcasted_iota(jnp.int32, (1, D_FOX), 1)
    own = [(lane >= h * HD) & (lane < (h + 1) * HD) for h in range(H)]
    q_all = jnp.concatenate([jnp.where(own[h], q, jnp.zeros_like(q)) for h in range(H)], axis=0)
    bias = [c_all[h:h + 1, past - 1:past] - c_all[h:h + 1, :] for h in range(H)]
    bias_c = jnp.concatenate([jnp.broadcast_to(b[:, :past], (ls, past)) for b in bias], axis=0)
    bias_n = jnp.concatenate([jnp.broadcast_to(b[:, past:past + ls], (ls, ls)) for b in bias], axis=0)
    s_c = jnp.dot(q_all, kct_ref[...].astype(BF16), preferred_element_type=F32) + bias_c
    s_n = jnp.where(jnp.concatenate([causal] * H, axis=0), _nt(q_all, kn_ref[...]) + bias_n, NEG)
    m = jnp.maximum(jnp.max(s_c, axis=-1, keepdims=True), jnp.max(s_n, axis=-1, keepdims=True))
    p_c = jnp.exp(s_c - m)
    p_n = jnp.exp(s_n - m)
    den = jnp.sum(p_c, axis=-1, keepdims=True) + jnp.sum(p_n, axis=-1, keepdims=True)
    o_all = (_nt(p_c.astype(BF16), vct_ref[...].astype(BF16))
             + jnp.dot(p_n.astype(BF16), vn_ref[...], preferred_element_type=F32)) / den
    o = jnp.zeros((ls, D_FOX), F32)
    for h in range(H):
        o = jnp.where(own[h], o_all[h * ls:(h + 1) * ls], o)
    o_ref[...] = o.astype(o_ref.dtype)


def _fox_sample(qb, kb, vb, kct, vct, lf_all, l, nb, ls, past):
    nseq = 4
    new = pl.BlockSpec((nseq * ls, D_FOX), lambda b: (b, 0))
    cache = pl.BlockSpec((None, nseq, D_FOX, past), lambda b: (l, b, 0, 0))
    return pl.pallas_call(
        functools.partial(_fox_sample_kernel, past=past, ls=ls, nseq=nseq), grid=(nb // nseq,),
        in_specs=[new, new, new, cache, cache, pl.BlockSpec((nseq, 8, past + LANE), lambda b: (b, 0, 0))],
        out_specs=new, out_shape=jax.ShapeDtypeStruct((nb * ls, D_FOX), BF16),
        compiler_params=_cparams(("parallel",)), name="fox_sample",
    )(qb, kb, vb, kct, vct, lf_all)


def _pool_tile(u, halo, w_ref, sc_ref, pos_start, tm):
    a = jnp.concatenate([halo, u], axis=0)
    e1 = a + pltpu.roll(a, 1, 0)
    e2 = e1 + pltpu.roll(e1, 2, 0)
    e3 = e2 + pltpu.roll(e2, 4, 0)
    e4 = e3 + pltpu.roll(e3, 8, 0)
    lane = lax.broadcasted_iota(jnp.int32, (1, D_POOL), 1)
    win = jnp.where(lane < 64, e1, jnp.where(lane < 128, e2, jnp.where(lane < 192, e3, e4)))[16:]
    wsz = jnp.where(lane < 64, 2.0, jnp.where(lane < 128, 4.0, jnp.where(lane < 192, 8.0, 16.0)))
    pos = (pos_start + lax.broadcasted_iota(jnp.int32, (tm, 1), 0)).astype(F32)
    diff = win / jnp.minimum(pos + 1.0, wsz) - u
    return jnp.dot(diff.astype(BF16), w_ref[...], preferred_element_type=F32) * sc_ref[...]


def _pool_kernel(u_ref, pre_ref, w_ref, sc_ref, o_ref, *, ls, nseq, pos0):
    for s in range(nseq):
        rows = slice(s * ls, (s + 1) * ls)
        o_ref[rows, :] = _pool_tile(u_ref[rows, :], pre_ref[s], w_ref, sc_ref, pos0, ls).astype(o_ref.dtype)


def _pool(u, prefix, w, l, nb, ls, pos0):
    nseq = 8
    tile = pl.BlockSpec((nseq * ls, D_POOL), lambda b: (b, 0))
    return pl.pallas_call(
        functools.partial(_pool_kernel, ls=ls, nseq=nseq, pos0=pos0), grid=(nb // nseq,),
        in_specs=[tile, pl.BlockSpec((nseq, 16, D_POOL), lambda b: (b, 0, 0)),
                  _layer_spec(w["wbd"], l, 1), _layer_spec(w["pscale"], l, 1)],
        out_specs=tile, out_shape=jax.ShapeDtypeStruct((nb * ls, D_POOL), BF16),
        compiler_params=_cparams(("parallel",)), name="pool_mixer",
    )(u, prefix, w["wbd"], w["pscale"])


def _ssd_chunk(xbc, z, sm, smt, ext_sc, st_sc, cw_ref, cb_ref, arow_ref, acol_ref, dsk_ref, gn_ref,
               row0, q, l_valid, live=None):
    ext_sc[SUBLANE:, :] = xbc
    w = cw_ref[...]
    conv = cb_ref[...]
    for j in range(CONV_W):
        conv = conv + w[j:j + 1] * ext_sc[pl.ds(SUBLANE - (CONV_W - 1) + j, q), :]
    ext_sc[:SUBLANE, :] = ext_sc[q:, :]
    act = _silu(conv)
    xs = act[:, :D_SSD]
    bbf = act[:, D_SSD:D_SSD + D_BC]
    bb = bbf.astype(BF16)
    cc = act[:, D_SSD + D_BC:].astype(BF16)
    bbt = bbf.T

    rvalid = (row0 + lax.broadcasted_iota(jnp.int32, (q, 1), 0)) < l_valid
    dtc = jnp.where(rvalid, sm, 0.0)
    acs_c = _scan_rows(dtc * (-jnp.exp(arow_ref[...])))
    cvalid = (row0 + lax.broadcasted_iota(jnp.int32, (1, LANE), 1)) < l_valid
    dtt = jnp.where(cvalid, smt, 0.0)
    acs_t = _scan_lanes(dtt * (-jnp.exp(acol_ref[...])))

    lane = lax.broadcasted_iota(jnp.int32, (1, LANE), 1)
    lo = lane < HD
    n_lo = lax.broadcasted_iota(jnp.int32, (LANE, 1), 0) < HD
    causal = (lax.broadcasted_iota(jnp.int32, (q, q), 1)
              <= lax.broadcasted_iota(jnp.int32, (q, q), 0))
    zc = jnp.zeros_like(cc)
    cb = [_nt(jnp.where(lo, cc, zc), bb), _nt(jnp.where(lo, zc, cc), bb)]
    ys = []
    for p in range(H // 2):
        cols = slice(p * LANE, (p + 1) * LANE)
        xp = xs[:, cols]
        xpb = xp.astype(BF16)
        st = st_sc[p]
        y_in = jnp.dot(cc, st.astype(BF16), preferred_element_type=F32)
        y_h, upd, dec = [], [], []
        for hh in range(2):
            h = 2 * p + hh
            g = h // (H // 2)
            arep = jnp.broadcast_to(acs_c[:, _DT_AT + h:_DT_AT + h + 1], (q, LANE))
            ak = acs_t[_DT_AT + h:_DT_AT + h + 1, :q]
            dtr = dtt[_DT_AT + h:_DT_AT + h + 1, :q]
            lm = jnp.exp(jnp.where(causal, arep[:, :q] - ak, NEG))
            gm = (cb[g] * lm * dtr).astype(BF16)
            y_h.append(jnp.dot(gm, xpb, preferred_element_type=F32) + y_in * jnp.exp(arep))
            alast = acs_t[_DT_AT + h:_DT_AT + h + 1, q - 1:q]
            bw = (bbt * (jnp.exp(alast - ak) * dtr)).astype(BF16)
            keep = (lo if hh == 0 else ~lo) & (n_lo if g == 0 else ~n_lo)
            upd.append(jnp.where(keep, jnp.dot(bw, xpb, preferred_element_type=F32), 0.0))
            dec.append(jnp.exp(alast))
        st_new = jnp.where(lo, dec[0], dec[1]) * st + upd[0] + upd[1]
        st_sc[p] = st_new if live is None else jnp.where(live, st_new, st)
        ys.append(jnp.where(lo, y_h[0], y_h[1]) + dsk_ref[:, cols] * xp)
    yc = jnp.concatenate(ys, axis=1) * _silu(z)
    return _rms(yc, gn_ref[...])


def _ssd_kernel(xbc_ref, z_ref, sm_ref, smt_ref, pre_ref, init_ref, cw_ref, cb_ref, arow_ref,
                acol_ref, dsk_ref, gn_ref, y_ref, fin_ref, ext_sc, st_sc, *, q, nseq):
    for s in range(nseq):
        rows = slice(s * q, (s + 1) * q)
        ext_sc[:SUBLANE, :] = pre_ref[s]
        st_sc[...] = init_ref[s]
        y = _ssd_chunk(xbc_ref[rows, :], z_ref[rows, :], sm_ref[rows, :], smt_ref[s], ext_sc, st_sc, cw_ref,
                       cb_ref, arow_ref, acol_ref, dsk_ref, gn_ref, 0, q, q)
        y_ref[rows, :] = y.astype(y_ref.dtype)
        fin_ref[s] = st_sc[...]


def _ssd(xbc, z, sm, smt, prefix, init, w, l, nb, q):
    nseq = 4
    rows = lambda c: pl.BlockSpec((nseq * q, c), lambda b: (b, 0))
    names = ("cw", "cb", "arow", "acol", "dsk", "gn")
    state = pl.BlockSpec((nseq, H // 2, LANE, LANE), lambda b: (b, 0, 0, 0))
    return pl.pallas_call(
        functools.partial(_ssd_kernel, q=q, nseq=nseq), grid=(nb // nseq,),
        in_specs=[rows(D_CONV), rows(D_SSD), rows(LANE), pl.BlockSpec((nseq, 16, LANE), lambda b: (b, 0, 0)),
                  pl.BlockSpec((nseq, 8, D_CONV), lambda b: (b, 0, 0)), state]
                 + [_layer_spec(w[n], l, 1) for n in names],
        out_specs=(rows(D_SSD), state),
        out_shape=(jax.ShapeDtypeStruct((nb * q, D_SSD), BF16),
                   jax.ShapeDtypeStruct((nb, H // 2, LANE, LANE), F32)),
        scratch_shapes=[pltpu.VMEM((q + SUBLANE, D_CONV), F32), pltpu.VMEM((H // 2, LANE, LANE), F32)],
        compiler_params=_cparams(("parallel",)), name="conv_ssd",
    )(xbc, z, sm, smt, prefix, init, *[w[n] for n in names])


def _state_to_wide(s):
    def placed(h):
        g, hh = h // (H // 2), h % 2
        return jnp.pad(jnp.swapaxes(s[:, h].astype(F32), 1, 2),
                       ((0, 0), (g * HD, LANE - (g + 1) * HD), (hh * HD, LANE - (hh + 1) * HD)))
    return jnp.stack([placed(2 * p) + placed(2 * p + 1) for p in range(H // 2)], axis=1)


def _state_from_wide(wide):
    return jnp.stack([jnp.swapaxes(wide[:, h // 2, (h // (H // 2)) * HD:(h // (H // 2) + 1) * HD,
                                        (h % 2) * HD:(h % 2 + 1) * HD], 1, 2) for h in range(H)], axis=1)


def _ffn_tile(x, a, p, s, wo_ref, g1_ref, g2_ref, g3_ref, wg_ref, wu_ref, wd_ref):
    mp = (jnp.dot(a, wo_ref[:D_FOX, :], preferred_element_type=F32)
          + jnp.dot(p, wo_ref[D_FOX:D_FOX + D_POOL, :], preferred_element_type=F32)
          + jnp.dot(s, wo_ref[D_FOX + D_POOL:, :], preferred_element_type=F32))
    x1 = x + _rms(mp, g1_ref[...])
    hb = _rms(x1, g2_ref[...]).astype(BF16)
    gate = jnp.dot(hb, wg_ref[...], preferred_element_type=F32)
    up = jnp.dot(hb, wu_ref[...], preferred_element_type=F32)
    act = (_silu(gate) * up).astype(BF16)
    ff = jnp.dot(act, wd_ref[...], preferred_element_type=F32)
    return x1 + _rms(ff, g3_ref[...])


def _ffn_kernel(x_ref, a_ref, p_ref, s_ref, *rest):
    o_ref = rest[-1]
    o_ref[...] = _ffn_tile(x_ref[...], a_ref[...], p_ref[...], s_ref[...], *rest[:-1])


def _mix_ffn_kernel(x_ref, a_ref, u_ref, xbc_ref, z_ref, sm_ref, smt_ref, wbd_ref, psc_ref,
                    cw_ref, cb_ref, arow_ref, acol_ref, dsk_ref, gn_ref,
                    wo_ref, g1_ref, g2_ref, g3_ref, wg_ref, wu_ref, wd_ref,
                    o_ref, fin_ref, halo_sc, ext_sc, st_sc, pool_sc, ssd_sc, *, tm, nsub, q, per, ng, l_valid):
    i = pl.program_id(0)
    live = i < ng
    ic = jnp.minimum(i, ng - 1)

    @pl.when(i == 0)
    def _():
        pool_sc[...] = jnp.zeros(pool_sc.shape, pool_sc.dtype)
        ssd_sc[...] = jnp.zeros(ssd_sc.shape, ssd_sc.dtype)

    o_ref[...] = _ffn_tile(x_ref[...], a_ref[...], pool_sc[...], ssd_sc[...], wo_ref, g1_ref, g2_ref, g3_ref,
                           wg_ref, wu_ref, wd_ref)

    for s in range(nsub):
        t = ic * nsub + s
        first = (t % per) == 0
        row0 = (t % per) * tm
        tile = slice(s * tm, (s + 1) * tm)
        u = u_ref[tile, :]
        halo = jnp.where(first, 0.0, halo_sc[...])
        pool_sc[tile, :] = _pool_tile(u, halo, wbd_ref, psc_ref, row0, tm).astype(pool_sc.dtype)
        halo_sc[...] = u[tm - 16:]
        ext_sc[:SUBLANE, :] = jnp.where(first, 0.0, ext_sc[:SUBLANE, :])
        st_sc[...] = jnp.where(first, 0.0, st_sc[...])
        for c in range(tm // q):
            rows = slice(s * tm + c * q, s * tm + (c + 1) * q)
            y = _ssd_chunk(xbc_ref[rows, :], z_ref[rows, :], sm_ref[rows, :], smt_ref[:, rows],
                           ext_sc, st_sc, cw_ref, cb_ref, arow_ref, acol_ref, dsk_ref, gn_ref,
                           row0 + c * q, q, l_valid, live)
            ssd_sc[rows, :] = y.astype(ssd_sc.dtype)
        fin_ref[t // per] = st_sc[...]


def _mix_ffn(x, attn, u, xbc, z, sm, smt, w, l, nb, tp, tm, q, l_valid):
    nsub = 2
    per = tp // tm
    assert q == LANE and tm % q == 0 and (nb * per) % nsub == 0
    ng = nb * per // nsub
    rows = nsub * tm
    prev = lambda c: pl.BlockSpec((rows, c), lambda i: (jnp.maximum(i - 1, 0), 0))
    cur = lambda c: pl.BlockSpec((rows, c), lambda i: (jnp.minimum(i, ng - 1), 0))
    names = ("wbd", "pscale", "cw", "cb", "arow", "acol", "dsk", "gn")
    big = ("wo", "g1", "g2", "g3", "wg", "wu", "wd")
    state_shape = (nb, H // 2, LANE, LANE)
    return pl.pallas_call(
        functools.partial(_mix_ffn_kernel, tm=tm, nsub=nsub, q=q, per=per, ng=ng, l_valid=l_valid),
        grid=(ng + 1,),
        in_specs=[prev(D_MODEL), prev(D_FOX), cur(D_POOL), cur(D_CONV), cur(D_SSD), cur(LANE),
                  pl.BlockSpec((16, rows), lambda i: (0, jnp.minimum(i, ng - 1)))]
                 + [_layer_spec(w[n], l, 1) for n in names]
                 + [_layer_spec(w[n], l, 1, single_buffer=True) for n in big],
        out_specs=(prev(D_MODEL), pl.BlockSpec(state_shape, lambda i: (0, 0, 0, 0))),
        out_shape=(jax.ShapeDtypeStruct((ng * rows, D_MODEL), F32), jax.ShapeDtypeStruct(state_shape, F32)),
        scratch_shapes=[pltpu.VMEM((16, D_POOL), F32), pltpu.VMEM((q + SUBLANE, D_CONV), F32),
                        pltpu.VMEM((H // 2, LANE, LANE), F32),
                        pltpu.VMEM((rows, D_POOL), BF16), pltpu.VMEM((rows, D_SSD), BF16)],
        compiler_params=_cparams(("arbitrary",)), name="mix_ffn",
    )(x, attn, u, xbc, z, sm, smt, *[w[n] for n in names], *[w[n] for n in big])


def _ffn(x, attn, pool, ssd, w, l, tm):
    t = x.shape[0]
    row = lambda c: pl.BlockSpec((tm, c), lambda i: (i, 0))
    names = ("wo", "g1", "g2", "g3", "wg", "wu", "wd")
    return pl.pallas_call(
        _ffn_kernel, grid=(t // tm,),
        in_specs=[row(D_MODEL), row(D_FOX), row(D_POOL), row(D_SSD)]
                 + [_layer_spec(w[n], l, 1, single_buffer=True) for n in names],
        out_specs=row(D_MODEL), out_shape=jax.ShapeDtypeStruct((t, D_MODEL), F32),
        compiler_params=_cparams(("parallel",)), name="outproj_ffn",
    )(x, attn, pool, ssd, *[w[n] for n in names])


def _lane_pack(f_vals, dt_vals):
    pieces, at = [], 0
    for start, vals in sorted([(_DT_AT, dt_vals)] + [(c, f_vals) for c in _F_COPIES], key=lambda t: t[0]):
        pieces += [jnp.zeros(vals.shape[:-1] + (start - at,), F32), vals.astype(F32)]
        at = start + H
    pieces.append(jnp.zeros(f_vals.shape[:-1] + (LANE - at,), F32))
    return jnp.concatenate(pieces, axis=-1)


def _prep_weights(w_in, fox_f_bias, pool_w, pool_scale, conv_w, conv_b, dt_bias, a_log, d_skip, ssd_norm,
                  w_out, w_gate, w_up, w_down, ln_pre_mix, ln_post_mix, ln_pre_ffn, ln_post_ffn):
    depth = w_in.shape[0]
    f0 = 3 * D_FOX
    u0 = f0 + H
    dt0 = u0 + D_POOL + D_SSD + D_CONV
    ws = _lane_pack(w_in[:, :, f0:u0], w_in[:, :, dt0:dt0 + H])
    wm = jnp.concatenate([w_in[:, :, :D_FOX], ws, w_in[:, :, u0:dt0]], axis=2).astype(BF16)
    ws = ws.astype(BF16)
    bs = _lane_pack(fox_f_bias, dt_bias)
    wbd = jnp.concatenate([jnp.pad(pool_w[:, g].astype(F32), ((0, 0), (0, 0), (g * 64, D_POOL - (g + 1) * 64)))
                           for g in range(D_POOL // 64)], axis=1)
    alog = _lane_pack(jnp.zeros_like(a_log), a_log)
    row = lambda a: a.astype(F32).reshape(depth, 1, -1)
    tr = lambda a: jnp.transpose(a, (0, 2, 1))
    return dict(
        g_pre=row(ln_pre_mix), wm=wm, wkv=w_in[:, :, D_FOX:f0].astype(BF16),
        wkt=tr(w_in[:, :, D_FOX:2 * D_FOX]).astype(BF16), wvt=tr(w_in[:, :, 2 * D_FOX:f0]).astype(BF16),
        wst=tr(ws[:, :, :16]),
        bs=bs.reshape(depth, 1, LANE), bst=bs[:, :16].reshape(depth, 16, 1),
        wbd=wbd.astype(BF16), pscale=row(pool_scale),
        cw=jnp.pad(conv_w.astype(F32), ((0, 0), (0, SUBLANE - CONV_W), (0, 0))), cb=row(conv_b),
        arow=alog.reshape(depth, 1, LANE), acol=alog[:, :16].reshape(depth, 16, 1),
        dsk=row(jnp.repeat(d_skip, HD, axis=1)), gn=row(ssd_norm),
        wo=w_out.astype(BF16), g1=row(ln_post_mix), g2=row(ln_pre_ffn), g3=row(ln_post_ffn),
        wg=w_gate.astype(BF16), wu=w_up.astype(BF16), wd=w_down.astype(BF16))


def _prompt_layer(x, w, l, nb, tp, lp, tm, q, kv_stacks):
    qb, kb, kt, vt, vtb, eb, u, z, xbc, sm, smt = _inproj(x, w, l, tm, (nb, tp, lp), kv_stacks)
    attn = _fox_prompt(qb, kb, eb, vtb, nb, tp, tm)
    x, fin = _mix_ffn(x, attn, u, xbc, z, sm, smt, w, l, nb, tp, tm, q, lp)
    return x, (kt, vt), (smt, u, xbc, fin)


def _sample_layer(x, w, l, nb, ls, past, tm, kct, vct, lfc, pool_pre, conv_pre, ssd_init):
    qb, k, kb, v, vb, u, z, xbc, sm, smt = _inproj(x, w, l, tm)
    smt_seq = jnp.transpose(smt.reshape(16, nb, ls), (1, 0, 2))
    lf_all = jnp.concatenate([lfc, smt_seq[:, :8], jnp.zeros((nb, 8, LANE - ls), F32)], axis=2)
    attn = _fox_sample(qb, kb, vb, kct, vct, lf_all, l, nb, ls, past)
    pool = _pool(u, pool_pre, w, l, nb, ls, past)
    ssd, fin = _ssd(xbc, z, sm, jnp.pad(smt_seq, ((0, 0), (0, 0), (0, LANE - ls))), conv_pre, ssd_init, w, l,
                    nb, ls)
    x = _ffn(x, attn, pool, ssd, w, l, tm)
    return x, (k, v), (smt, u, xbc, fin)


def kernel(x_prompt, x_sample, cache_fox_k, cache_fox_v, cache_fox_logf, state_pool, state_conv, state_ssd,
           meta_tokens, ln_pre_mix, ln_post_mix, ln_pre_ffn, ln_post_ffn, w_in, fox_f_bias, pool_w, pool_scale,
           conv_w, conv_b, dt_bias, a_log, d_skip, ssd_norm, w_out, w_gate, w_up, w_down):
    nbp, seq, _ = x_prompt.shape
    nbs, ls, _ = x_sample.shape
    depth, _, past = cache_fox_logf.shape[:3]
    lp = N_META + seq
    tq = 256
    q_ssd = 128
    tp = -(-lp // tq) * tq

    w = _prep_weights(w_in, fox_f_bias, pool_w, pool_scale, conv_w, conv_b, dt_bias, a_log, d_skip, ssd_norm,
                      w_out, w_gate, w_up, w_down, ln_pre_mix, ln_post_mix, ln_pre_ffn, ln_post_ffn)

    meta = jnp.broadcast_to(meta_tokens.astype(F32)[None], (nbp, N_META, D_MODEL))
    xp = jnp.concatenate([meta, x_prompt, jnp.zeros((nbp, tp - lp, D_MODEL), F32)], axis=1)
    xp = xp.reshape(nbp * tp, D_MODEL)
    xs = x_sample.reshape(nbs * ls, D_MODEL)

    ssd_init =_state_to_wide(state_ssd.reshape(depth * nbs, H, HD, HD)).reshape(depth, nbs, H // 2, LANE, LANE)

    kct = jnp.transpose(cache_fox_k, (0, 1, 3, 4, 2)).reshape(depth, nbs, D_FOX, past)
    vct = jnp.transpose(cache_fox_v, (0, 1, 3, 4, 2)).reshape(depth, nbs, D_FOX, past)
    lfc = jnp.pad(jnp.transpose(cache_fox_logf.astype(F32), (0, 1, 3, 2)),
                  ((0, 0), (0, 0), (0, 8 - H), (0, 0)))
    pool_pre = jnp.pad(state_pool.astype(F32), ((0, 0), (0, 0), (16 - POOL_BUF, 0), (0, 0)))
    conv_pre = jnp.pad(state_conv.astype(F32), ((0, 0), (0, 0), (8 - (CONV_W - 1), 0), (0, 0)))

    outs_p, outs_s, kv_s = [], [], []
    kv_stacks = tuple(jnp.zeros((depth, nbp, D_FOX, lp), F32) for _ in range(2))
    for l in range(depth):
        xp, kv_stacks, st = _prompt_layer(xp, w, l, nbp, tp, lp, tq, q_ssd, kv_stacks)
        outs_p.append(st)
        xs, kv, st = _sample_layer(xs, w, l, nbs, ls, past, tq, kct, vct, lfc[l], pool_pre[l], conv_pre[l],
                                   ssd_init[l])
        outs_s.append(st)
        kv_s.append(kv)

    def tails(outs, nb, rows, valid):
        seqv = lambda a: a.reshape(nb, rows, a.shape[-1])
        lf = jnp.stack([jnp.transpose(o[0][:H].reshape(H, nb, rows)[:, :, :valid], (1, 2, 0)) for o in outs])
        pn = jnp.stack([seqv(o[1])[:, valid - POOL_BUF:valid] for o in outs])
        cn = jnp.stack([seqv(o[2])[:, valid - (CONV_W - 1):valid] for o in outs])
        sn = jnp.stack([_state_from_wide(o[3]) for o in outs])
        return lf, pn, cn, sn

    def token_minor(a):
        return jnp.transpose(a.reshape(depth, nbp, H, HD, lp), (0, 1, 4, 2, 3))

    def token_major(i):
        return jnp.stack([kv[i].reshape(nbs, ls, H, HD) for kv in kv_s])

    y_prompt = xp.reshape(nbp, tp, D_MODEL)[:, N_META:lp]
    y_sample = xs.reshape(nbs, ls, D_MODEL)
    return ((y_prompt, y_sample, token_minor(kv_stacks[0]), token_minor(kv_stacks[1])) + tails(outs_p, nbp, tp, lp)
            + (token_major(0), token_major(1)) + tails(outs_s, nbs, ls, ls))
```
